```python
import jax, jax.numpy as jnp
from jax import lax
import numpy as np

D_MODEL = 1024
BATCH = 8
SEQ = 2048
DEPTH = 4
DEC_BATCH = 8
DEC_SEQ = 4096
PAST_LEN = 128

HEAD_DIM = 64
DILATED_GROUPS = ((128, 1), (512, 4), (2048, 16))
HEADS_PER_GROUP = 4
N_ATT_HEADS = HEADS_PER_GROUP * len(DILATED_GROUPS)
ATT_W = N_ATT_HEADS * HEAD_DIM
POOL_WINDOWS = (2, 4, 8, 16)
POOL_GROUP_W = 64
POOL_W = POOL_GROUP_W * len(POOL_WINDOWS)
MIX_W = ATT_W + POOL_W
IN_W = 3 * ATT_W + POOL_W
ROPE_THETA = 10000.0
Q_BLOCK = 128
RMS_EPS = 1e-6
N_EXPERTS = 16
N_EXPERT_GROUPS = 4
EXPERTS_PER_GROUP = N_EXPERTS // N_EXPERT_GROUPS
TOP_K = 2
D_EXPERT = 512

kernel_name = "hybrid_dilated_pool_moe_encoder"


def rmsnorm(x, g):
    xf = x.astype(jnp.float32)
    y = xf * lax.rsqrt(jnp.mean(xf * xf, axis=-1, keepdims=True) + RMS_EPS)
    return (y * g.astype(jnp.float32)).astype(x.dtype)


def rope(t, S):
    half = HEAD_DIM // 2
    inv = 1.0 / (ROPE_THETA ** (jnp.arange(0, HEAD_DIM, 2, dtype=jnp.float32) / HEAD_DIM))
    ang = jnp.arange(S, dtype=jnp.float32)[:, None] * inv[None, :]
    cos = jnp.cos(ang)[None, :, None, :]
    sin = jnp.sin(ang)[None, :, None, :]
    tf = t.astype(jnp.float32)
    t1, t2 = tf[..., :half], tf[..., half:]
    return jnp.concatenate([t1 * cos - t2 * sin, t2 * cos + t1 * sin], axis=-1).astype(t.dtype)


def dilated_attention(q, k, v, window, dilation):
    B, S, H, Dh = q.shape
    n_side = window // (2 * dilation)
    offsets = dilation * jnp.arange(-n_side, n_side + 1)
    nb = S // Q_BLOCK
    qb = q.reshape(B, nb, Q_BLOCK, H, Dh).swapaxes(0, 1)
    starts = jnp.arange(nb) * Q_BLOCK
    scale = HEAD_DIM ** -0.5

    def block(args):
        qi, start = args
        kpos = (start + jnp.arange(Q_BLOCK))[:, None] + offsets[None, :]
        valid = (kpos >= 0) & (kpos < S)
        kc = jnp.clip(kpos, 0, S - 1)
        kg = k[:, kc]
        vg = v[:, kc]
        s = jnp.einsum('bqhd,bqjhd->bqhj', qi, kg).astype(jnp.float32) * scale
        s = jnp.where(valid[None, :, None, :], s, -jnp.inf)
        m = jnp.max(s, axis=-1, keepdims=True)
        p = jnp.exp(s - m)
        l = jnp.sum(p, axis=-1, keepdims=True)
        o = jnp.einsum('bqhj,bqjhd->bqhd', (p / l).astype(v.dtype), vg)
        lse = (m + jnp.log(l))[..., 0]
        return o, lse

    o, lse = lax.map(block, (qb, starts))
    o = o.swapaxes(0, 1).reshape(B, S, H, Dh)
    lse = lse.swapaxes(0, 1).reshape(B, S, H)
    return o, lse


def pooling_mixer(u, w_pool_l, pool_scale_l):
    B, S, _ = u.shape
    t = jnp.arange(S)
    outs = []
    for g, w in enumerate(POOL_WINDOWS):
        ug = u[..., g * POOL_GROUP_W:(g + 1) * POOL_GROUP_W]
        uf = ug.astype(jnp.float32)
        cs = jnp.concatenate([jnp.zeros((B, 1, POOL_GROUP_W), jnp.float32), jnp.cumsum(uf, axis=1)], axis=1)
        lo = jnp.clip(t - w // 2, 0, S)
        hi = jnp.clip(t + w - w // 2, 0, S)
        cnt = (hi - lo).astype(jnp.float32)[None, :, None]
        mean = (cs[:, hi] - cs[:, lo]) / cnt
        outs.append((mean - uf).astype(u.dtype))
    z = jnp.stack(outs, axis=2)
    z = jnp.einsum('bsgc,gcd->bsgd', z, w_pool_l).reshape(B, S, POOL_W)
    return z * pool_scale_l


def token_mixer(h, w_in_l, w_pool_l, pool_scale_l, w_out_l):
    B, S, _ = h.shape
    proj = h @ w_in_l
    q = rope(proj[..., :ATT_W].reshape(B, S, N_ATT_HEADS, HEAD_DIM), S)
    k = rope(proj[..., ATT_W:2 * ATT_W].reshape(B, S, N_ATT_HEADS, HEAD_DIM), S)
    v = proj[..., 2 * ATT_W:3 * ATT_W].reshape(B, S, N_ATT_HEADS, HEAD_DIM)
    u = proj[..., 3 * ATT_W:]
    outs, lses = [], []
    for g, (win, dil) in enumerate(DILATED_GROUPS):
        sl = slice(g * HEADS_PER_GROUP, (g + 1) * HEADS_PER_GROUP)
        o, lse = dilated_attention(q[:, :, sl], k[:, :, sl], v[:, :, sl], win, dil)
        outs.append(o)
        lses.append(lse)
    alpha = jax.nn.softmax(jnp.stack(lses, axis=2), axis=2).astype(h.dtype)
    attn = jnp.concatenate([outs[g] * alpha[:, :, g, :, None] for g in range(len(DILATED_GROUPS))],
                           axis=2).reshape(B, S, ATT_W)
    pool = pooling_mixer(u, w_pool_l, pool_scale_l)
    return jnp.concatenate([attn, pool], axis=-1) @ w_out_l


def moe(h, w_router, router_bias, w_gate_l, w_up_l, w_down_l):
    B, S, D = h.shape
    xt = h.reshape(-1, D)
    aff = jax.nn.sigmoid(xt.astype(jnp.float32) @ w_router.astype(jnp.float32))
    sel = aff + router_bias.astype(jnp.float32)
    gscore = lax.top_k(sel.reshape(-1, N_EXPERT_GROUPS, EXPERTS_PER_GROUP), 2)[0].sum(-1)
    top_g = jnp.argmax(gscore, axis=-1)
    in_group = (jnp.arange(N_EXPERTS) // EXPERTS_PER_GROUP)[None, :] == top_g[:, None]
    _, idx = lax.top_k(jnp.where(in_group, sel, -jnp.inf), TOP_K)
    w = jnp.take_along_axis(aff, idx, axis=-1)
    w = w / jnp.sum(w, axis=-1, keepdims=True)
    gates = jnp.einsum('nk,nke->ne', w, jax.nn.one_hot(idx, N_EXPERTS, dtype=jnp.float32)).astype(h.dtype)
    y = jnp.zeros_like(xt)
    for e in range(N_EXPERTS):
        hid = jax.nn.silu(xt @ w_gate_l[e]) * (xt @ w_up_l[e])
        y = y + gates[:, e:e + 1] * (hid @ w_down_l[e])
    return y.reshape(B, S, D)


def trunk(x, c, norm_mix, w_mod, b_mod, w_in, w_pool, pool_scale, w_out, norm_ffn,
          w_router, router_bias, w_gate, w_up, w_down, norm_final):
    sc = jax.nn.silu(c)
    for l in range(DEPTH):
        mod = sc @ w_mod[l] + b_mod[l]
        sh1, sc1, g1, sh2, sc2, g2 = [m[:, None, :] for m in jnp.split(mod, 6, axis=-1)]
        h = rmsnorm(x, norm_mix[l]) * (1 + sc1) + sh1
        x = x + g1 * token_mixer(h, w_in[l], w_pool[l], pool_scale[l], w_out[l])
        h = rmsnorm(x, norm_ffn[l]) * (1 + sc2) + sh2
        x = x + g2 * moe(h, w_router, router_bias, w_gate[l], w_up[l], w_down[l])
    return rmsnorm(x, norm_final)


def setup_inputs(seed: int = 0) -> dict:
    key = jax.random.key(seed)
    ks = jax.random.split(key, 20)
    f = jnp.float32
    nrm = lambda k, shape, s: jax.random.normal(k, shape, f) * s
    return {
        "x_prompt": nrm(ks[0], (BATCH, SEQ, D_MODEL), 1.0),
        "x_sample": nrm(ks[1], (DEC_BATCH, DEC_SEQ, D_MODEL), 1.0),
        "c_prompt": nrm(ks[2], (BATCH, D_MODEL), 1.0),
        "c_sample": nrm(ks[3], (DEC_BATCH, D_MODEL), 1.0),
        "norm_mix": 1.0 + nrm(ks[4], (DEPTH, D_MODEL), 0.02),
        "w_mod": nrm(ks[5], (DEPTH, D_MODEL, 6 * D_MODEL), 0.5 * D_MODEL ** -0.5),
        "b_mod": nrm(ks[6], (DEPTH, 6 * D_MODEL), 0.01),
        "w_in": nrm(ks[7], (DEPTH, D_MODEL, IN_W), D_MODEL ** -0.5),
        "w_pool": nrm(ks[8], (DEPTH, len(POOL_WINDOWS), POOL_GROUP_W, POOL_GROUP_W), POOL_GROUP_W ** -0.5),
        "pool_scale": 1.0 + nrm(ks[9], (DEPTH, POOL_W), 0.1),
        "w_out": nrm(ks[10], (DEPTH, MIX_W, D_MODEL), MIX_W ** -0.5),
        "norm_ffn": 1.0 + nrm(ks[11], (DEPTH, D_MODEL), 0.02),
        "w_router": nrm(ks[12], (D_MODEL, N_EXPERTS), D_MODEL ** -0.5),
        "router_bias": nrm(ks[13], (N_EXPERTS,), 0.01),
        "w_gate": nrm(ks[14], (DEPTH, N_EXPERTS, D_MODEL, D_EXPERT), D_MODEL ** -0.5),
        "w_up": nrm(ks[15], (DEPTH, N_EXPERTS, D_MODEL, D_EXPERT), D_MODEL ** -0.5),
        "w_down": nrm(ks[16], (DEPTH, N_EXPERTS, D_EXPERT, D_MODEL), D_EXPERT ** -0.5),
        "norm_final": 1.0 + nrm(ks[17], (D_MODEL,), 0.02),
    }


def reference(x_prompt, x_sample, c_prompt, c_sample, norm_mix, w_mod, b_mod, w_in, w_pool, pool_scale,
              w_out, norm_ffn, w_router, router_bias, w_gate, w_up, w_down, norm_final):
    y_prompt = trunk(x_prompt, c_prompt, norm_mix, w_mod, b_mod, w_in, w_pool, pool_scale, w_out, norm_ffn,
                     w_router, router_bias, w_gate, w_up, w_down, norm_final)
    y_sample = trunk(x_sample, c_sample, norm_mix, w_mod, b_mod, w_in, w_pool, pool_scale, w_out, norm_ffn,
                     w_router, router_bias, w_gate, w_up, w_down, norm_final)
    return (y_prompt, y_sample)
```

```python
import functools

import jax
import jax.numpy as jnp
import numpy as np
from jax import lax
from jax.experimental import pallas as pl
from jax.experimental.pallas import tpu as pltpu

D_MODEL = 1024
DEPTH = 4
HEAD_DIM = 64
HALF_HEAD = HEAD_DIM // 2
DILATED_GROUPS = ((128, 1), (512, 4), (2048, 16))
N_GROUPS = len(DILATED_GROUPS)
HEADS_PER_GROUP = 4
GROUP_W = HEADS_PER_GROUP * HEAD_DIM
ATT_W = N_GROUPS * GROUP_W
POOL_WINDOWS = (2, 4, 8, 16)
POOL_GROUP_W = 64
POOL_W = POOL_GROUP_W * len(POOL_WINDOWS)
POOL_HALO = max(POOL_WINDOWS) // 2
IN_W = 3 * ATT_W + POOL_W
ROPE_THETA = 10000.0
RMS_EPS = 1e-6
N_EXPERTS = 16
N_EXPERT_GROUPS = 4
EXPERTS_PER_GROUP = N_EXPERTS // N_EXPERT_GROUPS
D_EXPERT = 512

LANES = 128
SUBLANES = 8
Q_BLOCK = 128
VMEM_LIMIT = 48 * 1024 * 1024

TOKEN_TILE = 512
MOE_TILE = 512

_BF16 = jnp.bfloat16
_F32 = jnp.float32


def _params(semantics):
    return pltpu.CompilerParams(dimension_semantics=semantics, vmem_limit_bytes=VMEM_LIMIT)


def _mod_kernel(c_ref, w_ref, b_ref, o_ref):
    c = c_ref[...]
    sc = c / (1.0 + jnp.exp(-c))
    o_ref[0] = jnp.dot(sc, w_ref[0], preferred_element_type=_F32,
                       precision=lax.Precision.HIGHEST) + b_ref[0]


def _modulation(c, w_mod, b_mod):
    nb = c.shape[0]
    col = D_MODEL
    n_col = w_mod.shape[2] // col
    return pl.pallas_call(
        _mod_kernel,
        grid=(DEPTH, n_col),
        in_specs=[
            pl.BlockSpec((nb, D_MODEL), lambda l, j: (0, 0)),
            pl.BlockSpec((1, D_MODEL, col), lambda l, j: (l, 0, j)),
            pl.BlockSpec((1, 1, col), lambda l, j: (l, 0, j)),
        ],
        out_specs=pl.BlockSpec((1, nb, col), lambda l, j: (l, 0, j)),
        out_shape=jax.ShapeDtypeStruct((DEPTH, nb, w_mod.shape[2]), _F32),
        compiler_params=_params(("arbitrary", "arbitrary")),
        name="modulation",
    )(c, w_mod, b_mod.reshape(DEPTH, 1, -1))


def _rmsnorm_mod(x, gain, scale, shift):
    y = x * lax.rsqrt(jnp.mean(x * x, axis=-1, keepdims=True) + RMS_EPS)
    return (y * gain) * (1.0 + scale) + shift


def _rope_chunk(t, cos, sin_signed, first_half):
    fwd = pltpu.roll(t, HALF_HEAD, axis=1)
    bwd = pltpu.roll(t, LANES - HALF_HEAD, axis=1)
    return t * cos + jnp.where(first_half, bwd, fwd) * sin_signed


def _in_proj_kernel(x_ref, mod_ref, gain_ref, w_ref, cos_ref, sin_ref, *out_refs):
    q_refs, k_refs, v_refs, u_ref = out_refs[0:3], out_refs[3:6], out_refs[6:9], out_refs[9]
    mod = mod_ref[0]
    h = _rmsnorm_mod(x_ref[0], gain_ref[...], mod[1:2], mod[0:1])
    proj = jnp.dot(h.astype(_BF16), w_ref[...], preferred_element_type=_F32)
    cos = cos_ref[...]
    sin_signed = sin_ref[...]
    lane = lax.broadcasted_iota(jnp.int32, (1, LANES), 1)
    first_half = (lane % HEAD_DIM) < HALF_HEAD
    q_scale = HEAD_DIM ** -0.5
    for g in range(N_GROUPS):
        for c in range(GROUP_W // LANES):
            lo = g * GROUP_W + c * LANES
            dst = slice(c * LANES, (c + 1) * LANES)
            q = _rope_chunk(proj[:, lo:lo + LANES], cos, sin_signed, first_half)
            q_refs[g][0, :, dst] = (q * q_scale).astype(_BF16)
            k = _rope_chunk(proj[:, ATT_W + lo:ATT_W + lo + LANES], cos, sin_signed, first_half)
            k_refs[g][0, :, dst] = k.astype(_BF16)
        v_refs[g][0] = proj[:, 2 * ATT_W + g * GROUP_W:2 * ATT_W + (g + 1) * GROUP_W].astype(_BF16)
    u_ref[0] = proj[:, 3 * ATT_W:]


def _in_proj(x, mod_l, gain, w_in_bf16, cos, sin_signed, batch_offset):
    B, S, _ = x.shape
    tm = TOKEN_TILE
    grp = jax.ShapeDtypeStruct((B, S, GROUP_W), _BF16)
    grp_spec = pl.BlockSpec((1, tm, GROUP_W), lambda b, i: (b, i, 0))
    return pl.pallas_call(
        _in_proj_kernel,
        grid=(B, S // tm),
        in_specs=[
            pl.BlockSpec((1, tm, D_MODEL), lambda b, i: (b, i, 0)),
            pl.BlockSpec((1, 6, D_MODEL), lambda b, i: (b + batch_offset, 0, 0)),
            pl.BlockSpec((1, D_MODEL), lambda b, i: (0, 0)),
            pl.BlockSpec((D_MODEL, IN_W), lambda b, i: (0, 0)),
            pl.BlockSpec((tm, LANES), lambda b, i: (i, 0)),
            pl.BlockSpec((tm, LANES), lambda b, i: (i, 0)),
        ],
        out_specs=[grp_spec] * 9 + [pl.BlockSpec((1, tm, POOL_W), lambda b, i: (b, i, 0))],
        out_shape=[grp] * 9 + [jax.ShapeDtypeStruct((B, S, POOL_W), _F32)],
        compiler_params=_params(("arbitrary", "arbitrary")),
        name="in_proj",
    )(x, mod_l, gain, w_in_bf16, cos, sin_signed)


def _attn_kernel(q_ref, k_ref, v_ref, o_ref, lse_ref, *, length, n_side):
    r = pl.program_id(1)
    key_w = min(length, Q_BLOCK + 2 * n_side)
    n_qb = length // Q_BLOCK

    @pl.when(r == 0)
    def _():
        lse_ref[...] = jnp.zeros_like(lse_ref)

    lane = lax.broadcasted_iota(jnp.int32, (1, LANES), 1)
    head_mask = [(lane < HEAD_DIM).astype(_BF16), (lane >= HEAD_DIM).astype(_BF16)]
    lse_lane = lax.broadcasted_iota(jnp.int32, (1, lse_ref.shape[2]), 1)

    def q_block(qb, carry):
        q0 = pl.multiple_of(qb * Q_BLOCK, Q_BLOCK)
        k0 = pl.multiple_of(jnp.clip(q0 - n_side, 0, length - key_w), n_side)
        qi = q0 + lax.broadcasted_iota(jnp.int32, (Q_BLOCK, 1), 0)
        kj = k0 + lax.broadcasted_iota(jnp.int32, (1, key_w), 1)
        valid = jnp.abs(qi - kj) <= n_side
        lse_rows = lse_ref[0, pl.ds(q0, Q_BLOCK), :]
        for pair in range(GROUP_W // LANES):
            cols = slice(pair * LANES, (pair + 1) * LANES)
            q2 = q_ref[0, pl.ds(q0, Q_BLOCK), cols]
            k2 = k_ref[0, pl.ds(k0, key_w), cols]
            v2 = v_ref[0, pl.ds(k0, key_w), cols]
            outs = []
            for hh in range(2):
                s = lax.dot_general(q2 * head_mask[hh], k2, (((1,), (1,)), ((), ())),
                                    preferred_element_type=_F32)
                s = jnp.where(valid, s, -jnp.inf)
                m = jnp.max(s, axis=1, keepdims=True)
                p = jnp.exp(s - m)
                l = jnp.sum(p, axis=1, keepdims=True)
                o = jnp.dot(p.astype(_BF16), v2, preferred_element_type=_F32)
                outs.append(o * (1.0 / l))
                col = r * HEADS_PER_GROUP + pair * 2 + hh
                lse_rows = jnp.where(lse_lane == col, m + jnp.log(l), lse_rows)
            o_ref[0, pl.ds(q0, Q_BLOCK), cols] = jnp.where(lane < HEAD_DIM, outs[0], outs[1]).astype(_BF16)
        lse_ref[0, pl.ds(q0, Q_BLOCK), :] = lse_rows
        return carry

    lax.fori_loop(0, n_qb, q_block, 0)


def _attention(q, k, v, window, dilation):
    B, S, _ = q.shape
    d = dilation
    L = S // d
    n_side = window // (2 * d)
    view = lambda t: t.reshape(B, L, d * GROUP_W)
    spec = pl.BlockSpec((1, L, GROUP_W), lambda b, r: (b, 0, r))
    o, lse = pl.pallas_call(
        functools.partial(_attn_kernel, length=L, n_side=n_side),
        grid=(B, d),
        in_specs=[spec, spec, spec],
        out_specs=[spec, pl.BlockSpec((1, L, d * HEADS_PER_GROUP), lambda b, r: (b, 0, 0))],
        out_shape=[jax.ShapeDtypeStruct((B, L, d * GROUP_W), _BF16),
                   jax.ShapeDtypeStruct((B, L, d * HEADS_PER_GROUP), _F32)],
        compiler_params=_params(("arbitrary", "arbitrary")),
        name=f"attention_d{d}",
    )(view(q), view(k), view(v))
    return o.reshape(B, S, GROUP_W), lse.reshape(B, S, HEADS_PER_GROUP)


def _route(sel, aff):
    s = [sel[e:e + 1, :] for e in range(N_EXPERTS)]
    a = [aff[e:e + 1, :] for e in range(N_EXPERTS)]
    group_score = []
    for g in range(N_EXPERT_GROUPS):
        c0, c1, c2, c3 = s[EXPERTS_PER_GROUP * g:EXPERTS_PER_GROUP * (g + 1)]
        m1, n1 = jnp.maximum(c0, c1), jnp.minimum(c0, c1)
        m2, n2 = jnp.maximum(c2, c3), jnp.minimum(c2, c3)
        group_score.append(jnp.maximum(m1, m2) + jnp.maximum(jnp.minimum(m1, m2), jnp.maximum(n1, n2)))
    best, top_g = group_score[0], jnp.zeros_like(group_score[0], dtype=jnp.int32)
    for g in range(1, N_EXPERT_GROUPS):
        better = group_score[g] > best
        best = jnp.where(better, group_score[g], best)
        top_g = jnp.where(better, g, top_g)

    def pick(rows, j):
        out = rows[j]
        for g in range(1, N_EXPERT_GROUPS):
            out = jnp.where(top_g == g, rows[EXPERTS_PER_GROUP * g + j], out)
        return out

    cs = [pick(s, j) for j in range(EXPERTS_PER_GROUP)]
    ca = [pick(a, j) for j in range(EXPERTS_PER_GROUP)]
    s1, a1, i1 = cs[0], ca[0], jnp.zeros_like(top_g)
    for j in range(1, EXPERTS_PER_GROUP):
        better = cs[j] > s1
        s1 = jnp.where(better, cs[j], s1)
        a1 = jnp.where(better, ca[j], a1)
        i1 = jnp.where(better, j, i1)
    s2 = jnp.full_like(s1, -jnp.inf)
    a2, i2 = jnp.zeros_like(a1), jnp.full_like(i1, -1)
    for j in range(EXPERTS_PER_GROUP):
        better = (cs[j] > s2) & (i1 != j)
        s2 = jnp.where(better, cs[j], s2)
        a2 = jnp.where(better, ca[j], a2)
        i2 = jnp.where(better, j, i2)
    den = a1 + a2
    e1 = top_g * EXPERTS_PER_GROUP + i1
    e2 = top_g * EXPERTS_PER_GROUP + i2
    eid = lax.broadcasted_iota(jnp.int32, sel.shape, 0)
    return jnp.where(eid == e1, a1 / den, 0.0) + jnp.where(eid == e2, a2 / den, 0.0)


def _mix_out_kernel(x_ref, o0_ref, o1_ref, o2_ref, l0_ref, l1_ref, l2_ref, u_ref, up_ref, un_ref,
                    mod_ref, wpool_ref, pscale_ref, wout_ref, gain_ref, wr_hi_ref, wr_lo_ref, rbias_ref,
                    x1_ref, h2_ref, gates_ref, ext_ref, lvl_a_ref, lvl_b_ref, *, seq_len):
    i = pl.program_id(1)
    n_i = pl.num_programs(1)
    tm = x_ref.shape[1]
    mod = mod_ref[0]

    lses = [l0_ref[0], l1_ref[0], l2_ref[0]]
    top = jnp.maximum(jnp.maximum(lses[0], lses[1]), lses[2])
    es = [jnp.exp(t - top) for t in lses]
    den = es[0] + es[1] + es[2]
    pieces = []
    for g, o_ref in enumerate((o0_ref, o1_ref, o2_ref)):
        alpha = es[g] / den
        wide = jnp.concatenate([jnp.broadcast_to(alpha[:, h:h + 1], (tm, HEAD_DIM))
                                for h in range(HEADS_PER_GROUP)], axis=1)
        pieces.append((o_ref[0].astype(_F32) * wide).astype(_BF16))

    pad = 2 * SUBLANES
    zeros_pad = jnp.zeros((SUBLANES, POOL_W), _F32)
    for ref in (ext_ref, lvl_a_ref, lvl_b_ref):
        ref[0:SUBLANES, :] = zeros_pad
        ref[tm + pad + SUBLANES:tm + 2 * pad, :] = zeros_pad
    u = u_ref[0]
    ext_ref[SUBLANES:pad, :] = jnp.where(i > 0, up_ref[0], 0.0)
    ext_ref[pad:pad + tm, :] = u
    ext_ref[pad + tm:pad + tm + SUBLANES, :] = jnp.where(i < n_i - 1, un_ref[0], 0.0)
    span = tm + 2 * SUBLANES
    win = lambda ref, off: ref[SUBLANES + off:SUBLANES + off + span, :]
    lvl_a_ref[SUBLANES:SUBLANES + span, :] = win(ext_ref, -1) + win(ext_ref, 0)
    s2 = lvl_a_ref[pad:pad + tm, :]
    lvl_b_ref[SUBLANES:SUBLANES + span, :] = win(lvl_a_ref, -1) + win(lvl_a_ref, 1)
    s4 = lvl_b_ref[pad:pad + tm, :]
    lvl_a_ref[SUBLANES:SUBLANES + span, :] = win(lvl_b_ref, -2) + win(lvl_b_ref, 2)
    s8 = lvl_a_ref[pad:pad + tm, :]
    s16 = lvl_a_ref[pad - 4:pad - 4 + tm, :] + lvl_a_ref[pad + 4:pad + 4 + tm, :]
    lane = lax.broadcasted_iota(jnp.int32, (1, POOL_W), 1)
    wgroup = lane // POOL_GROUP_W
    wsum = jnp.where(wgroup == 0, s2, jnp.where(wgroup == 1, s4, jnp.where(wgroup == 2, s8, s16)))
    half_w = jnp.where(wgroup == 0, 1, jnp.where(wgroup == 1, 2, jnp.where(wgroup == 2, 4, 8)))
    pos = i * tm + lax.broadcasted_iota(jnp.int32, (tm, 1), 0)
    cnt = jnp.minimum(pos + half_w, seq_len) - jnp.maximum(pos - half_w, 0)
    z = wsum / cnt.astype(_F32) - u
    pool = jnp.dot(z.astype(_BF16), wpool_ref[...], preferred_element_type=_F32) * pscale_ref[...]
    pieces.append(pool.astype(_BF16))

    mix = jnp.dot(jnp.concatenate(pieces, axis=1), wout_ref[...], preferred_element_type=_F32)
    x1 = x_ref[0] + mod[2:3] * mix
    x1_ref[0] = x1

    h2 = _rmsnorm_mod(x1, gain_ref[...], mod[4:5], mod[3:4])
    h_hi = h2.astype(_BF16)
    h2_ref[0] = h_hi
    h_lo = (h2 - h_hi.astype(_F32)).astype(_BF16)
    nt = (((1,), (1,)), ((), ()))
    logits = (lax.dot_general(wr_hi_ref[...], h_hi, nt, preferred_element_type=_F32)
              + lax.dot_general(wr_lo_ref[...], h_hi, nt, preferred_element_type=_F32)
              + lax.dot_general(wr_hi_ref[...], h_lo, nt, preferred_element_type=_F32))
    aff = 1.0 / (1.0 + jnp.exp(-logits))
    gates = _route(aff + rbias_ref[...], aff)
    gates_ref[0] = gates


def _mix_out(x, attn, lses, u, mod_l, wpool_bd, pool_scale, w_out_bf16, gain, wr_hi, wr_lo, rbias, batch_offset):
    B, S, _ = x.shape
    tm = TOKEN_TILE
    n_i = S // tm
    halo_blocks = tm // SUBLANES
    tok = lambda w: pl.BlockSpec((1, tm, w), lambda b, i: (b, i, 0))
    const = lambda shape: pl.BlockSpec(shape, lambda b, i: tuple(0 for _ in shape))
    rows = tm + 4 * SUBLANES
    return pl.pallas_call(
        functools.partial(_mix_out_kernel, seq_len=S),
        grid=(B, n_i),
        in_specs=[
            tok(D_MODEL), tok(GROUP_W), tok(GROUP_W), tok(GROUP_W),
            tok(HEADS_PER_GROUP), tok(HEADS_PER_GROUP), tok(HEADS_PER_GROUP),
            tok(POOL_W),
            pl.BlockSpec((1, SUBLANES, POOL_W), lambda b, i: (b, jnp.maximum(i * halo_blocks - 1, 0), 0)),
            pl.BlockSpec((1, SUBLANES, POOL_W),
                         lambda b, i: (b, jnp.minimum((i + 1) * halo_blocks, S // SUBLANES - 1), 0)),
            pl.BlockSpec((1, 6, D_MODEL), lambda b, i: (b + batch_offset, 0, 0)),
            const((POOL_W, POOL_W)), const((1, POOL_W)), const((D_MODEL, D_MODEL)), const((1, D_MODEL)),
            const((N_EXPERTS, D_MODEL)), const((N_EXPERTS, D_MODEL)), const((N_EXPERTS, 1)),
        ],
        out_specs=[tok(D_MODEL), tok(D_MODEL), pl.BlockSpec((1, N_EXPERTS, tm), lambda b, i: (b, 0, i))],
        out_shape=[jax.ShapeDtypeStruct((B, S, D_MODEL), _F32),
                   jax.ShapeDtypeStruct((B, S, D_MODEL), _BF16),
                   jax.ShapeDtypeStruct((B, N_EXPERTS, S), _F32)],
        scratch_shapes=[pltpu.VMEM((rows, POOL_W), _F32)] * 3,
        compiler_params=_params(("arbitrary", "arbitrary")),
        name="mix_out",
    )(x, *attn, *lses, u, u, u, mod_l, wpool_bd, pool_scale, w_out_bf16, gain, wr_hi, wr_lo, rbias)


def _moe_kernel(h_ref, x1_ref, gates_ref, mod_ref, wg_ref, wu_ref, wd_ref, gfin_ref, o_ref, acc_ref, gt_ref,
                *, final_norm):
    e = pl.program_id(2)

    @pl.when(e == 0)
    def _():
        acc_ref[...] = jnp.zeros_like(acc_ref)
        tm = acc_ref.shape[0]
        padded = jnp.concatenate([gates_ref[0], jnp.zeros((LANES - N_EXPERTS, tm), _F32)], axis=0)
        gt_ref[...] = padded.T

    h = h_ref[0]
    a = jnp.dot(h, wg_ref[0], preferred_element_type=_F32)
    b = jnp.dot(h, wu_ref[0], preferred_element_type=_F32)
    hid = (a / (1.0 + jnp.exp(-a))) * b
    y = jnp.dot(hid.astype(_BF16), wd_ref[0], preferred_element_type=_F32)
    lane = lax.broadcasted_iota(jnp.int32, (1, LANES), 1)
    gate = jnp.sum(jnp.where(lane == e, gt_ref[...], 0.0), axis=1, keepdims=True)
    acc_ref[...] += gate * y

    @pl.when(e == N_EXPERTS - 1)
    def _():
        x2 = x1_ref[0] + mod_ref[0][5:6] * acc_ref[...]
        if final_norm:
            x2 = x2 * lax.rsqrt(jnp.mean(x2 * x2, axis=-1, keepdims=True) + RMS_EPS) * gfin_ref[...]
        o_ref[0] = x2


def _moe(h2, x1, gates, mod_l, wg, wu, wd, gain_final, batch_offset, final_norm):
    B, S, _ = x1.shape
    tm = MOE_TILE
    tok = lambda w: pl.BlockSpec((1, tm, w), lambda b, i, e: (b, i, 0))
    return pl.pallas_call(
        functools.partial(_moe_kernel, final_norm=final_norm),
        grid=(B, S // tm, N_EXPERTS),
        in_specs=[
            tok(D_MODEL), tok(D_MODEL),
            pl.BlockSpec((1, N_EXPERTS, tm), lambda b, i, e: (b, 0, i)),
            pl.BlockSpec((1, 6, D_MODEL), lambda b, i, e: (b + batch_offset, 0, 0)),
            pl.BlockSpec((1, D_MODEL, D_EXPERT), lambda b, i, e: (e, 0, 0)),
            pl.BlockSpec((1, D_MODEL, D_EXPERT), lambda b, i, e: (e, 0, 0)),
            pl.BlockSpec((1, D_EXPERT, D_MODEL), lambda b, i, e: (e, 0, 0)),
            pl.BlockSpec((1, D_MODEL), lambda b, i, e: (0, 0)),
        ],
        out_specs=tok(D_MODEL),
        out_shape=jax.ShapeDtypeStruct((B, S, D_MODEL), _F32),
        scratch_shapes=[pltpu.VMEM((tm, D_MODEL), _F32), pltpu.VMEM((tm, LANES), _F32)],
        compiler_params=_params(("arbitrary", "arbitrary", "arbitrary")),
        name="moe",
    )(h2, x1, gates, mod_l, wg, wu, wd, gain_final)


def _rope_tables(S):
    inv = 1.0 / (ROPE_THETA ** (jnp.arange(0, HEAD_DIM, 2, dtype=_F32) / HEAD_DIM))
    ang = jnp.arange(S, dtype=_F32)[:, None] * inv[None, :]
    cos, sin = jnp.cos(ang), jnp.sin(ang)
    reps = LANES // HEAD_DIM
    return (jnp.tile(jnp.concatenate([cos, cos], axis=1), (1, reps)),
            jnp.tile(jnp.concatenate([-sin, sin], axis=1), (1, reps)))


def _trunk(x, batch_offset, mod, w):
    B, S, _ = x.shape
    cos, sin_signed = _rope_tables(S)
    for l in range(DEPTH):
        mod_l = mod[l]
        outs = _in_proj(x, mod_l, w["norm_mix"][l], w["w_in"][l], cos, sin_signed, batch_offset)
        q, k, v, u = outs[0:3], outs[3:6], outs[6:9], outs[9]
        attn, lses = [], []
        for g, (window, dilation) in enumerate(DILATED_GROUPS):
            o, lse = _attention(q[g], k[g], v[g], window, dilation)
            attn.append(o)
            lses.append(lse)
        x1, h2, gates = _mix_out(x, attn, lses, u, mod_l, w["wpool_bd"][l], w["pool_scale"][l], w["w_out"][l],
                                 w["norm_ffn"][l], w["wr_hi"], w["wr_lo"], w["rbias"], batch_offset)
        x = _moe(h2, x1, gates, mod_l, w["w_gate"][l], w["w_up"][l], w["w_down"][l], w["norm_final"],
                 batch_offset, final_norm=(l == DEPTH - 1))
    return x


def kernel(x_prompt, x_sample, c_prompt, c_sample, norm_mix, w_mod, b_mod, w_in, w_pool, pool_scale, w_out,
           norm_ffn, w_router, router_bias, w_gate, w_up, w_down, norm_final):
    n_prompt = x_prompt.shape[0]
    c = jnp.concatenate([c_prompt, c_sample], axis=0)
    mod = _modulation(c, w_mod, b_mod).reshape(DEPTH, c.shape[0], 6, D_MODEL)
    n_pool = len(POOL_WINDOWS)
    eye = jnp.eye(n_pool, dtype=w_pool.dtype)
    wpool_bd = (w_pool[:, :, :, None, :] * eye[None, :, None, :, None]).reshape(DEPTH, POOL_W, POOL_W)
    wr_t = w_router.T.astype(_F32)
    wr_hi = wr_t.astype(_BF16)
    w = {
        "norm_mix": norm_mix.reshape(DEPTH, 1, D_MODEL),
        "norm_ffn": norm_ffn.reshape(DEPTH, 1, D_MODEL),
        "norm_final": norm_final.reshape(1, D_MODEL),
        "w_in": w_in.astype(_BF16),
        "w_out": w_out.astype(_BF16),
        "wpool_bd": wpool_bd.astype(_BF16),
        "pool_scale": pool_scale.reshape(DEPTH, 1, POOL_W),
        "wr_hi": wr_hi,
        "wr_lo": (wr_t - wr_hi.astype(_F32)).astype(_BF16),
        "rbias": router_bias.astype(_F32).reshape(N_EXPERTS, 1),
        "w_gate": w_gate.astype(_BF16),
        "w_up": w_up.astype(_BF16),
        "w_down": w_down.astype(_BF16),
    }
    y_prompt = _trunk(x_prompt, 0, mod, w)
    y_sample = _trunk(x_sample, n_prompt, mod, w)
    return (y_prompt, y_sample)
```

```python
import functools

import jax
import jax.numpy as jnp
from jax import lax
from jax.experimental import pallas as pl
from jax.experimental.pallas import tpu as pltpu

D_MODEL = 1024
DEPTH = 4
HEAD_DIM = 64
HALF_HEAD = HEAD_DIM // 2
DILATED_GROUPS = ((128, 1), (512, 4), (2048, 16))
N_GROUPS = len(DILATED_GROUPS)
HEADS_PER_GROUP = 4
GROUP_W = HEADS_PER_GROUP * HEAD_DIM
ATT_W = N_GROUPS * GROUP_W
POOL_WINDOWS = (2, 4, 8, 16)
POOL_GROUP_W = 64
POOL_W = POOL_GROUP_W * len(POOL_WINDOWS)
IN_W = 3 * ATT_W + POOL_W
ROPE_THETA = 10000.0
RMS_EPS = 1e-6
N_EXPERTS = 16
N_EXPERT_GROUPS = 4
EXPERTS_PER_GROUP = N_EXPERTS // N_EXPERT_GROUPS
PAIRS_PER_GROUP = EXPERTS_PER_GROUP * (EXPERTS_PER_GROUP - 1) // 2
N_CLASSES = N_EXPERT_GROUPS * PAIRS_PER_GROUP
D_EXPERT = 512

LANES = 128
SUBLANES = 8
Q_BLOCK = 128
VMEM_LIMIT = 48 * 1024 * 1024

TOKEN_TILE = 512
MOE_TILE = 256
PLAN_CHUNK = 512
CLASS_ROWS = 32
ROW_CHUNKS = D_MODEL // LANES

_BF16 = jnp.bfloat16
_F32 = jnp.float32
_I32 = jnp.int32


def _params(semantics):
    return pltpu.CompilerParams(dimension_semantics=semantics, vmem_limit_bytes=VMEM_LIMIT)


def _mod_kernel(c_ref, w_ref, b_ref, o_ref):
    c = c_ref[...]
    sc = c / (1.0 + jnp.exp(-c))
    o_ref[0] = jnp.dot(sc, w_ref[0], preferred_element_type=_F32,
                       precision=lax.Precision.HIGHEST) + b_ref[0]


def _modulation(c, w_mod, b_mod):
    nb = c.shape[0]
    col = D_MODEL
    n_col = w_mod.shape[2] // col
    return pl.pallas_call(
        _mod_kernel,
        grid=(DEPTH, n_col),
        in_specs=[
            pl.BlockSpec((nb, D_MODEL), lambda l, j: (0, 0)),
            pl.BlockSpec((1, D_MODEL, col), lambda l, j: (l, 0, j)),
            pl.BlockSpec((1, 1, col), lambda l, j: (l, 0, j)),
        ],
        out_specs=pl.BlockSpec((1, nb, col), lambda l, j: (l, 0, j)),
        out_shape=jax.ShapeDtypeStruct((DEPTH, nb, w_mod.shape[2]), _F32),
        compiler_params=_params(("arbitrary", "arbitrary")),
        name="modulation",
    )(c, w_mod, b_mod.reshape(DEPTH, 1, -1))


def _rmsnorm_mod(x, gain, scale, shift):
    y = x * lax.rsqrt(jnp.mean(x * x, axis=-1, keepdims=True) + RMS_EPS)
    return (y * gain) * (1.0 + scale) + shift


def _rope_chunk(t, cos, sin_signed, first_half):
    fwd = pltpu.roll(t, HALF_HEAD, axis=1)
    bwd = pltpu.roll(t, LANES - HALF_HEAD, axis=1)
    return t * cos + jnp.where(first_half, bwd, fwd) * sin_signed


def _in_proj_kernel(x_ref, mod_ref, gain_ref, w_ref, cos_ref, sin_ref, *out_refs):
    q_refs, k_refs, v_refs, u_ref = out_refs[0:3], out_refs[3:6], out_refs[6:9], out_refs[9]
    mod = mod_ref[0]
    h = _rmsnorm_mod(x_ref[0], gain_ref[...], mod[1:2], mod[0:1])
    proj = jnp.dot(h.astype(_BF16), w_ref[...], preferred_element_type=_F32)
    cos = cos_ref[...]
    sin_signed = sin_ref[...]
    lane = lax.broadcasted_iota(_I32, (1, LANES), 1)
    first_half = (lane % HEAD_DIM) < HALF_HEAD
    q_scale = HEAD_DIM ** -0.5
    for g in range(N_GROUPS):
        for c in range(GROUP_W // LANES):
            lo = g * GROUP_W + c * LANES
            dst = slice(c * LANES, (c + 1) * LANES)
            q = _rope_chunk(proj[:, lo:lo + LANES], cos, sin_signed, first_half)
            q_refs[g][0, :, dst] = (q * q_scale).astype(_BF16)
            k = _rope_chunk(proj[:, ATT_W + lo:ATT_W + lo + LANES], cos, sin_signed, first_half)
            k_refs[g][0, :, dst] = k.astype(_BF16)
        v_refs[g][0] = proj[:, 2 * ATT_W + g * GROUP_W:2 * ATT_W + (g + 1) * GROUP_W].astype(_BF16)
    u_ref[0] = proj[:, 3 * ATT_W:]


def _in_proj(x, mod_l, gain, w_in_bf16, cos, sin_signed, batch_offset):
    B, S, _ = x.shape
    tm = TOKEN_TILE
    grp = jax.ShapeDtypeStruct((B, S, GROUP_W), _BF16)
    grp_spec = pl.BlockSpec((1, tm, GROUP_W), lambda b, i: (b, i, 0))
    return pl.pallas_call(
        _in_proj_kernel,
        grid=(B, S // tm),
        in_specs=[
            pl.BlockSpec((1, tm, D_MODEL), lambda b, i: (b, i, 0)),
            pl.BlockSpec((1, 6, D_MODEL), lambda b, i: (b + batch_offset, 0, 0)),
            pl.BlockSpec((1, D_MODEL), lambda b, i: (0, 0)),
            pl.BlockSpec((D_MODEL, IN_W), lambda b, i: (0, 0)),
            pl.BlockSpec((tm, LANES), lambda b, i: (i, 0)),
            pl.BlockSpec((tm, LANES), lambda b, i: (i, 0)),
        ],
        out_specs=[grp_spec] * 9 + [pl.BlockSpec((1, tm, POOL_W), lambda b, i: (b, i, 0))],
        out_shape=[grp] * 9 + [jax.ShapeDtypeStruct((B, S, POOL_W), _F32)],
        compiler_params=_params(("arbitrary", "arbitrary")),
        name="in_proj",
    )(x, mod_l, gain, w_in_bf16, cos, sin_signed)


def _attn_kernel(q_ref, k_ref, v_ref, o_ref, lse_ref, *, length, n_side):
    r = pl.program_id(1)
    key_w = min(length, Q_BLOCK + 2 * n_side)
    n_qb = length // Q_BLOCK

    @pl.when(r == 0)
    def _():
        lse_ref[...] = jnp.zeros_like(lse_ref)

    lane = lax.broadcasted_iota(_I32, (1, LANES), 1)
    head_mask = [(lane < HEAD_DIM).astype(_BF16), (lane >= HEAD_DIM).astype(_BF16)]
    lse_lane = lax.broadcasted_iota(_I32, (1, lse_ref.shape[2]), 1)

    def q_block(qb, carry):
        q0 = pl.multiple_of(qb * Q_BLOCK, Q_BLOCK)
        k0 = pl.multiple_of(jnp.clip(q0 - n_side, 0, length - key_w), n_side)
        qi = q0 + lax.broadcasted_iota(_I32, (Q_BLOCK, 1), 0)
        kj = k0 + lax.broadcasted_iota(_I32, (1, key_w), 1)
        valid = jnp.abs(qi - kj) <= n_side
        lse_rows = lse_ref[0, pl.ds(q0, Q_BLOCK), :]
        for pair in range(GROUP_W // LANES):
            cols = slice(pair * LANES, (pair + 1) * LANES)
            q2 = q_ref[0, pl.ds(q0, Q_BLOCK), cols]
            k2 = k_ref[0, pl.ds(k0, key_w), cols]
            v2 = v_ref[0, pl.ds(k0, key_w), cols]
            outs = []
            for hh in range(2):
                s = lax.dot_general(q2 * head_mask[hh], k2, (((1,), (1,)), ((), ())),
                                    preferred_element_type=_F32)
                s = jnp.where(valid, s, -jnp.inf)
                m = jnp.max(s, axis=1, keepdims=True)
                p = jnp.exp(s - m)
                l = jnp.sum(p, axis=1, keepdims=True)
                o = jnp.dot(p.astype(_BF16), v2, preferred_element_type=_F32)
                outs.append(o * (1.0 / l))
                col = r * HEADS_PER_GROUP + pair * 2 + hh
                lse_rows = jnp.where(lse_lane == col, m + jnp.log(l), lse_rows)
            o_ref[0, pl.ds(q0, Q_BLOCK), cols] = jnp.where(lane < HEAD_DIM, outs[0], outs[1]).astype(_BF16)
        lse_ref[0, pl.ds(q0, Q_BLOCK), :] = lse_rows
        return carry

    lax.fori_loop(0, n_qb, q_block, 0)


def _attention(q, k, v, window, dilation):
    B, S, _ = q.shape
    d = dilation
    L = S // d
    n_side = window // (2 * d)
    view = lambda t: t.reshape(B, L, d * GROUP_W)
    spec = pl.BlockSpec((1, L, GROUP_W), lambda b, r: (b, 0, r))
    o, lse = pl.pallas_call(
        functools.partial(_attn_kernel, length=L, n_side=n_side),
        grid=(B, d),
        in_specs=[spec, spec, spec],
        out_specs=[spec, pl.BlockSpec((1, L, d * HEADS_PER_GROUP), lambda b, r: (b, 0, 0))],
        out_shape=[jax.ShapeDtypeStruct((B, L, d * GROUP_W), _BF16),
                   jax.ShapeDtypeStruct((B, L, d * HEADS_PER_GROUP), _F32)],
        compiler_params=_params(("arbitrary", "arbitrary")),
        name=f"attention_d{d}",
    )(view(q), view(k), view(v))
    return o.reshape(B, S, GROUP_W), lse.reshape(B, S, HEADS_PER_GROUP)


def _route(sel):
    s = [sel[e:e + 1, :] for e in range(N_EXPERTS)]
    group_score = []
    for g in range(N_EXPERT_GROUPS):
        c0, c1, c2, c3 = s[EXPERTS_PER_GROUP * g:EXPERTS_PER_GROUP * (g + 1)]
        m1, n1 = jnp.maximum(c0, c1), jnp.minimum(c0, c1)
        m2, n2 = jnp.maximum(c2, c3), jnp.minimum(c2, c3)
        group_score.append(jnp.maximum(m1, m2) + jnp.maximum(jnp.minimum(m1, m2), jnp.maximum(n1, n2)))
    best, top_g = group_score[0], jnp.zeros(group_score[0].shape, _I32)
    for g in range(1, N_EXPERT_GROUPS):
        better = group_score[g] > best
        best = jnp.where(better, group_score[g], best)
        top_g = jnp.where(better, g, top_g)
    cs = []
    for j in range(EXPERTS_PER_GROUP):
        c = s[j]
        for g in range(1, N_EXPERT_GROUPS):
            c = jnp.where(top_g == g, s[EXPERTS_PER_GROUP * g + j], c)
        cs.append(c)
    s1, i1 = cs[0], jnp.zeros_like(top_g)
    for j in range(1, EXPERTS_PER_GROUP):
        better = cs[j] > s1
        s1 = jnp.where(better, cs[j], s1)
        i1 = jnp.where(better, j, i1)
    i2 = jnp.where(i1 == 0, 1, 0)
    s2 = jnp.where(i1 == 0, cs[1], cs[0])
    for j in range(1, EXPERTS_PER_GROUP):
        better = (cs[j] > s2) & (i1 != j)
        s2 = jnp.where(better, cs[j], s2)
        i2 = jnp.where(better, j, i2)
    lo, hi = jnp.minimum(i1, i2), jnp.maximum(i1, i2)
    pair = jnp.where(lo == 0, 0, jnp.where(lo == 1, 3, 5)) + (hi - lo - 1)
    return top_g * PAIRS_PER_GROUP + pair


def _mix_out_kernel(x_ref, o0_ref, o1_ref, o2_ref, l0_ref, l1_ref, l2_ref, u_ref, up_ref, un_ref,
                    mod_ref, wpool_ref, pscale_ref, wout_ref, gain_ref, wr_hi_ref, wr_lo_ref, rbias_ref,
                    x1_ref, h2_ref, cls_ref, ext_ref, lvl_a_ref, lvl_b_ref, *, seq_len):
    i = pl.program_id(1)
    n_i = pl.num_programs(1)
    tm = x_ref.shape[1]
    mod = mod_ref[0]

    lses = [l0_ref[0], l1_ref[0], l2_ref[0]]
    top = jnp.maximum(jnp.maximum(lses[0], lses[1]), lses[2])
    es = [jnp.exp(t - top) for t in lses]
    den = es[0] + es[1] + es[2]
    pieces = []
    for g, o_ref in enumerate((o0_ref, o1_ref, o2_ref)):
        alpha = es[g] / den
        wide = jnp.concatenate([jnp.broadcast_to(alpha[:, h:h + 1], (tm, HEAD_DIM))
                                for h in range(HEADS_PER_GROUP)], axis=1)
        pieces.append((o_ref[0].astype(_F32) * wide).astype(_BF16))

    pad = 2 * SUBLANES
    zeros_pad = jnp.zeros((SUBLANES, POOL_W), _F32)
    for ref in (ext_ref, lvl_a_ref, lvl_b_ref):
        ref[0:SUBLANES, :] = zeros_pad
        ref[tm + pad + SUBLANES:tm + 2 * pad, :] = zeros_pad
    u = u_ref[0]
    ext_ref[SUBLANES:pad, :] = jnp.where(i > 0, up_ref[0], 0.0)
    ext_ref[pad:pad + tm, :] = u
    ext_ref[pad + tm:pad + tm + SUBLANES, :] = jnp.where(i < n_i - 1, un_ref[0], 0.0)
    span = tm + 2 * SUBLANES
    win = lambda ref, off: ref[SUBLANES + off:SUBLANES + off + span, :]
    lvl_a_ref[SUBLANES:SUBLANES + span, :] = win(ext_ref, -1) + win(ext_ref, 0)
    s2 = lvl_a_ref[pad:pad + tm, :]
    lvl_b_ref[SUBLANES:SUBLANES + span, :] = win(lvl_a_ref, -1) + win(lvl_a_ref, 1)
    s4 = lvl_b_ref[pad:pad + tm, :]
    lvl_a_ref[SUBLANES:SUBLANES + span, :] = win(lvl_b_ref, -2) + win(lvl_b_ref, 2)
    s8 = lvl_a_ref[pad:pad + tm, :]
    s16 = lvl_a_ref[pad - 4:pad - 4 + tm, :] + lvl_a_ref[pad + 4:pad + 4 + tm, :]
    lane = lax.broadcasted_iota(_I32, (1, POOL_W), 1)
    wgroup = lane // POOL_GROUP_W
    wsum = jnp.where(wgroup == 0, s2, jnp.where(wgroup == 1, s4, jnp.where(wgroup == 2, s8, s16)))
    half_w = jnp.where(wgroup == 0, 1, jnp.where(wgroup == 1, 2, jnp.where(wgroup == 2, 4, 8)))
    pos = i * tm + lax.broadcasted_iota(_I32, (tm, 1), 0)
    cnt = jnp.minimum(pos + half_w, seq_len) - jnp.maximum(pos - half_w, 0)
    z = wsum / cnt.astype(_F32) - u
    pool = jnp.dot(z.astype(_BF16), wpool_ref[...], preferred_element_type=_F32) * pscale_ref[...]
    pieces.append(pool.astype(_BF16))

    mix = jnp.dot(jnp.concatenate(pieces, axis=1), wout_ref[...], preferred_element_type=_F32)
    x1 = x_ref[0] + mod[2:3] * mix
    x1_ref[0] = x1

    h2 = _rmsnorm_mod(x1, gain_ref[...], mod[4:5], mod[3:4])
    h_hi = h2.astype(_BF16)
    h2_ref[0] = h_hi
    h_lo = (h2 - h_hi.astype(_F32)).astype(_BF16)
    nt = (((1,), (1,)), ((), ()))
    logits = (lax.dot_general(wr_hi_ref[...], h_hi, nt, preferred_element_type=_F32)
              + lax.dot_general(wr_lo_ref[...], h_hi, nt, preferred_element_type=_F32)
              + lax.dot_general(wr_hi_ref[...], h_lo, nt, preferred_element_type=_F32))
    aff = 1.0 / (1.0 + jnp.exp(-logits))
    cls_ref[0] = _route(aff + rbias_ref[...])


def _mix_out(x, attn, lses, u, mod_l, wpool_bd, pool_scale, w_out_bf16, gain, wr_hi, wr_lo, rbias, batch_offset):
    B, S, _ = x.shape
    tm = TOKEN_TILE
    n_i = S // tm
    halo_blocks = tm // SUBLANES
    tok = lambda w: pl.BlockSpec((1, tm, w), lambda b, i: (b, i, 0))
    const = lambda shape: pl.BlockSpec(shape, lambda b, i: tuple(0 for _ in shape))
    rows = tm + 4 * SUBLANES
    return pl.pallas_call(
        functools.partial(_mix_out_kernel, seq_len=S),
        grid=(B, n_i),
        in_specs=[
            tok(D_MODEL), tok(GROUP_W), tok(GROUP_W), tok(GROUP_W),
            tok(HEADS_PER_GROUP), tok(HEADS_PER_GROUP), tok(HEADS_PER_GROUP),
            tok(POOL_W),
            pl.BlockSpec((1, SUBLANES, POOL_W), lambda b, i: (b, jnp.maximum(i * halo_blocks - 1, 0), 0)),
            pl.BlockSpec((1, SUBLANES, POOL_W),
                         lambda b, i: (b, jnp.minimum((i + 1) * halo_blocks, S // SUBLANES - 1), 0)),
            pl.BlockSpec((1, 6, D_MODEL), lambda b, i: (b + batch_offset, 0, 0)),
            const((POOL_W, POOL_W)), const((1, POOL_W)), const((D_MODEL, D_MODEL)), const((1, D_MODEL)),
            const((N_EXPERTS, D_MODEL)), const((N_EXPERTS, D_MODEL)), const((N_EXPERTS, 1)),
        ],
        out_specs=[tok(D_MODEL), tok(D_MODEL), pl.BlockSpec((1, 1, tm), lambda b, i: (b, 0, i))],
        out_shape=[jax.ShapeDtypeStruct((B, S, D_MODEL), _F32),
                   jax.ShapeDtypeStruct((B, S, D_MODEL), _BF16),
                   jax.ShapeDtypeStruct((B, 1, S), _I32)],
        scratch_shapes=[pltpu.VMEM((rows, POOL_W), _F32)] * 3,
        compiler_params=_params(("arbitrary", "arbitrary")),
        name="mix_out",
    )(x, *attn, *lses, u, u, u, mod_l, wpool_bd, pool_scale, w_out_bf16, gain, wr_hi, wr_lo, rbias)


TBL_EXPERT_A, TBL_EXPERT_B, TBL_TILE, TBL_USED, TBL_LAST_TILE_ROW = range(5)


def _plan_kernel(cls_ref, slot_ref, tbl_ref, cnt_ref, run_ref, base_ref, *, tile):
    phase, b, j = pl.program_id(0), pl.program_id(1), pl.program_id(2)
    first = (b == 0) & (j == 0)
    chunk = cls_ref.shape[2]
    n_tbl = tbl_ref.shape[1]
    onehot = lax.broadcasted_iota(_I32, (CLASS_ROWS, chunk), 0) == cls_ref[0]
    hits = jnp.sum(onehot.astype(_F32), axis=1, keepdims=True)

    @pl.when((phase == 0) & first)
    def _():
        cnt_ref[...] = jnp.zeros_like(cnt_ref)

    @pl.when(phase == 0)
    def _():
        cnt_ref[...] += hits

    @pl.when((phase == 1) & first)
    def _():
        padded = jnp.ceil(cnt_ref[...] * (1.0 / tile)) * tile
        r_i = lax.broadcasted_iota(_I32, (CLASS_ROWS, CLASS_ROWS), 0)
        c_i = lax.broadcasted_iota(_I32, (CLASS_ROWS, CLASS_ROWS), 1)
        padded_row = jnp.sum(jnp.where(r_i == c_i, padded, 0.0), axis=0, keepdims=True)
        base_col = jnp.sum(jnp.where(c_i < r_i, padded_row, 0.0), axis=1, keepdims=True)
        base_row = jnp.sum(jnp.where(r_i < c_i, padded, 0.0), axis=0, keepdims=True)
        total = jnp.sum(padded_row, axis=1, keepdims=True)
        base_ref[...] = base_col
        run_ref[...] = jnp.zeros_like(run_ref)

        k_row = lax.broadcasted_iota(_I32, (1, n_tbl), 1).astype(_F32)
        start = k_row * tile
        inside = (base_col <= start) & (start < base_col + padded)
        cls_id = lax.broadcasted_iota(_I32, (CLASS_ROWS, n_tbl), 0).astype(_F32)
        tile_cls = jnp.sum(jnp.where(inside, cls_id, 0.0), axis=0, keepdims=True)
        n_used = total * (1.0 / tile)
        last = jnp.maximum(n_used - 1.0, 0.0)
        last_cls = jnp.sum(jnp.where(k_row == last, tile_cls, 0.0), axis=1, keepdims=True)
        used = k_row < n_used
        tile_cls = jnp.where(used, tile_cls, last_cls)
        grp = sum(jnp.where(tile_cls >= PAIRS_PER_GROUP * g, 1.0, 0.0) for g in range(1, N_EXPERT_GROUPS))
        pair = tile_cls - grp * PAIRS_PER_GROUP
        a = jnp.where(pair < 3, 0.0, jnp.where(pair < 5, 1.0, 2.0))
        bb = jnp.where(pair < 3, pair + 1.0, jnp.where(pair < 5, pair - 1.0, 3.0))
        last_tile_row = jnp.where(padded_row > 0, base_row + padded_row - tile, -1.0)
        last_tile_row = jnp.concatenate(
            [last_tile_row, jnp.full((1, n_tbl - CLASS_ROWS), -1.0, _F32)], axis=1)
        zero = jnp.zeros((1, n_tbl), _F32)
        tbl_ref[...] = jnp.concatenate(
            [grp * EXPERTS_PER_GROUP + a, grp * EXPERTS_PER_GROUP + bb, jnp.where(used, k_row, last),
             jnp.where(used, 1.0, 0.0), last_tile_row, zero, zero, zero], axis=0).astype(_I32)

    @pl.when(phase == 1)
    def _():
        before = lax.broadcasted_iota(_I32, (chunk, chunk), 0) < lax.broadcasted_iota(_I32, (chunk, chunk), 1)
        earlier = jnp.dot(jnp.where(onehot, 1.0, 0.0).astype(_BF16), jnp.where(before, 1.0, 0.0).astype(_BF16),
                          preferred_element_type=_F32)
        dest = base_ref[...] + run_ref[...] + earlier
        slot_ref[0] = jnp.sum(jnp.where(onehot, dest, 0.0), axis=0, keepdims=True).astype(_I32)
        run_ref[...] += hits


def _moe_plan(cls, tile):
    B, _, S = cls.shape
    chunk = PLAN_CHUNK
    n_tiles = (B * S) // tile + N_CLASSES
    n_tbl = -(-n_tiles // LANES) * LANES
    slot, tbl = pl.pallas_call(
        functools.partial(_plan_kernel, tile=tile),
        grid=(2, B, S // chunk),
        in_specs=[pl.BlockSpec((1, 1, chunk), lambda p, b, j: (b, 0, j))],
        out_specs=[pl.BlockSpec((1, 1, chunk), lambda p, b, j: (b * p, 0, j * p)),
                   pl.BlockSpec((SUBLANES, n_tbl), lambda p, b, j: (0, 0))],
        out_shape=[jax.ShapeDtypeStruct((B, 1, S), _I32), jax.ShapeDtypeStruct((SUBLANES, n_tbl), _I32)],
        scratch_shapes=[pltpu.VMEM((CLASS_ROWS, 1), _F32)] * 3,
        compiler_params=_params(("arbitrary", "arbitrary", "arbitrary")),
        name="moe_plan",
    )(cls)
    return slot, tbl, n_tiles


def _row_copy(src, src_row, dst, dst_row, sem):
    return pltpu.make_async_copy(src.at[pl.ds(pl.multiple_of(src_row * ROW_CHUNKS, ROW_CHUNKS), ROW_CHUNKS), :],
                                 dst.at[pl.ds(pl.multiple_of(dst_row * ROW_CHUNKS, ROW_CHUNKS), ROW_CHUNKS), :],
                                 sem)


def _dispatch_kernel(tbl_ref, slot_ref, h_ref, hs_ref, stage_ref, zero_ref, sem, zsem, *, tile):
    first = (pl.program_id(0) == 0) & (pl.program_id(1) == 0)
    tm = h_ref.shape[1]
    tile_rows = tile * ROW_CHUNKS

    @pl.when(first)
    def _():
        zero_ref[...] = jnp.zeros_like(zero_ref)
        n_tiles = hs_ref.shape[0] // tile_rows
        for wait in (False, True):
            for c in range(N_CLASSES):
                row = tbl_ref[TBL_LAST_TILE_ROW, c]

                @pl.when(row >= 0)
                def _():
                    at = pl.multiple_of(row * ROW_CHUNKS, ROW_CHUNKS)
                    cp = pltpu.make_async_copy(zero_ref, hs_ref.at[pl.ds(at, tile_rows), :], zsem)
                    cp.wait() if wait else cp.start()

            for k in range(n_tiles - N_CLASSES, n_tiles):
                @pl.when(tbl_ref[TBL_USED, k] == 0)
                def _():
                    cp = pltpu.make_async_copy(zero_ref, hs_ref.at[pl.ds(k * tile_rows, tile_rows), :], zsem)
                    cp.wait() if wait else cp.start()

    h = h_ref[0]
    for c in range(ROW_CHUNKS):
        stage_ref[pl.ds(c, tm, stride=ROW_CHUNKS), :] = h[:, c * LANES:(c + 1) * LANES].astype(_F32)

    def issue(g, carry):
        for k in range(SUBLANES):
            r = g * SUBLANES + k
            _row_copy(stage_ref, r, hs_ref, slot_ref[0, 0, r], sem).start()
        return carry

    lax.fori_loop(0, tm // SUBLANES, issue, 0)
    pltpu.make_async_copy(stage_ref, hs_ref.at[pl.ds(0, tm * ROW_CHUNKS), :], sem).wait()


def _moe_dispatch(tbl, slot, h2, n_tiles, tile):
    B, S, _ = h2.shape
    tm = TOKEN_TILE
    return pl.pallas_call(
        functools.partial(_dispatch_kernel, tile=tile),
        grid_spec=pltpu.PrefetchScalarGridSpec(
            num_scalar_prefetch=1,
            grid=(B, S // tm),
            in_specs=[pl.BlockSpec((1, 1, tm), lambda b, i, tbl: (b, 0, i), memory_space=pltpu.SMEM),
                      pl.BlockSpec((1, tm, D_MODEL), lambda b, i, tbl: (b, i, 0))],
            out_specs=pl.BlockSpec(memory_space=pl.ANY),
            scratch_shapes=[pltpu.VMEM((tm * ROW_CHUNKS, LANES), _F32),
                            pltpu.VMEM((tile * ROW_CHUNKS, LANES), _F32),
                            pltpu.SemaphoreType.DMA(()), pltpu.SemaphoreType.DMA(())],
        ),
        out_shape=jax.ShapeDtypeStruct((n_tiles * tile * ROW_CHUNKS, LANES), _F32),
        compiler_params=_params(("arbitrary", "arbitrary")),
        name="moe_dispatch",
    )(tbl, slot, h2)


def _experts_kernel(tbl_ref, hs_ref, wra_hi_ref, wra_lo_ref, wrb_hi_ref, wrb_lo_ref,
                    wga_ref, wua_ref, wda_ref, wgb_ref, wub_ref, wdb_ref, ys_ref):
    k = pl.program_id(0)
    tile = hs_ref.shape[0] // ROW_CHUNKS

    @pl.when(tbl_ref[TBL_USED, k] == 0)
    def _():
        ys_ref[...] = jnp.zeros_like(ys_ref)

    @pl.when(tbl_ref[TBL_USED, k] == 1)
    def _():
        x = jnp.concatenate([hs_ref[pl.ds(c, tile, stride=ROW_CHUNKS), :].astype(_BF16)
                             for c in range(ROW_CHUNKS)], axis=1)
        row_id = lax.broadcasted_iota(_I32, (2 * SUBLANES, D_MODEL), 0)
        rows = jnp.zeros((2 * SUBLANES, D_MODEL), _F32)
        for n, ref in enumerate((wra_hi_ref, wra_lo_ref, wrb_hi_ref, wrb_lo_ref)):
            rows = jnp.where(row_id == n, ref[0], rows)
        lg = lax.dot_general(x, rows.astype(_BF16), (((1,), (1,)), ((), ())),
                             preferred_element_type=_F32)
        aff_a = 1.0 / (1.0 + jnp.exp(-(lg[:, 0:1] + lg[:, 1:2])))
        aff_b = 1.0 / (1.0 + jnp.exp(-(lg[:, 2:3] + lg[:, 3:4])))
        den = aff_a + aff_b

        def hidden(wg_ref, wu_ref, gate):
            a = jnp.dot(x, wg_ref[0], preferred_element_type=_F32)
            b = jnp.dot(x, wu_ref[0], preferred_element_type=_F32)
            return ((a / (1.0 + jnp.exp(-a))) * b * gate).astype(_BF16)

        y = (jnp.dot(hidden(wga_ref, wua_ref, aff_a / den), wda_ref[0], preferred_element_type=_F32)
             + jnp.dot(hidden(wgb_ref, wub_ref, aff_b / den), wdb_ref[0], preferred_element_type=_F32))
        for c in range(ROW_CHUNKS):
            ys_ref[pl.ds(c, tile, stride=ROW_CHUNKS), :] = y[:, c * LANES:(c + 1) * LANES]


def _moe_experts(tbl, hs, wr_hi, wr_lo, wg, wu, wd, n_tiles, tile):
    rows = tile * ROW_CHUNKS
    tile_spec = pl.BlockSpec((rows, LANES), lambda k, tbl: (tbl[TBL_TILE, k], 0))
    router = lambda row: pl.BlockSpec((1, 1, D_MODEL), lambda k, tbl: (tbl[row, k], 0, 0))
    up = lambda row: pl.BlockSpec((1, D_MODEL, D_EXPERT), lambda k, tbl: (tbl[row, k], 0, 0))
    down = lambda row: pl.BlockSpec((1, D_EXPERT, D_MODEL), lambda k, tbl: (tbl[row, k], 0, 0))
    a, b = TBL_EXPERT_A, TBL_EXPERT_B
    return pl.pallas_call(
        _experts_kernel,
        grid_spec=pltpu.PrefetchScalarGridSpec(
            num_scalar_prefetch=1,
            grid=(n_tiles,),
            in_specs=[tile_spec, router(a), router(a), router(b), router(b),
                      up(a), up(a), down(a), up(b), up(b), down(b)],
            out_specs=pl.BlockSpec((rows, LANES), lambda k, tbl: (k, 0)),
        ),
        out_shape=jax.ShapeDtypeStruct(hs.shape, _F32),
        compiler_params=_params(("arbitrary",)),
        name="moe_experts",
    )(tbl, hs, wr_hi, wr_lo, wr_hi, wr_lo, wg, wu, wd, wg, wu, wd)


def _combine_kernel(slot_ref, x1_ref, mod_ref, gfin_ref, ys_ref, o_ref, buf_ref, sem, *, final_norm):
    tm = x1_ref.shape[1]

    def issue(g, carry):
        for k in range(SUBLANES):
            r = g * SUBLANES + k
            _row_copy(ys_ref, slot_ref[0, 0, r], buf_ref, r, sem).start()
        return carry

    lax.fori_loop(0, tm // SUBLANES, issue, 0)
    pltpu.make_async_copy(ys_ref.at[pl.ds(0, tm * ROW_CHUNKS), :], buf_ref, sem).wait()

    gate = mod_ref[0][5:6]
    x1 = x1_ref[0]
    x2 = jnp.concatenate(
        [x1[:, c * LANES:(c + 1) * LANES]
         + gate[:, c * LANES:(c + 1) * LANES] * buf_ref[pl.ds(c, tm, stride=ROW_CHUNKS), :]
         for c in range(ROW_CHUNKS)], axis=1)
    if final_norm:
        x2 = x2 * lax.rsqrt(jnp.mean(x2 * x2, axis=-1, keepdims=True) + RMS_EPS) * gfin_ref[...]
    o_ref[0] = x2


def _moe_combine(slot, x1, mod_l, gain_final, ys, batch_offset, final_norm):
    B, S, _ = x1.shape
    tm = TOKEN_TILE
    return pl.pallas_call(
        functools.partial(_combine_kernel, final_norm=final_norm),
        grid=(B, S // tm),
        in_specs=[pl.BlockSpec((1, 1, tm), lambda b, i: (b, 0, i), memory_space=pltpu.SMEM),
                  pl.BlockSpec((1, tm, D_MODEL), lambda b, i: (b, i, 0)),
                  pl.BlockSpec((1, 6, D_MODEL), lambda b, i: (b + batch_offset, 0, 0)),
                  pl.BlockSpec((1, D_MODEL), lambda b, i: (0, 0)),
                  pl.BlockSpec(memory_space=pl.ANY)],
        out_specs=pl.BlockSpec((1, tm, D_MODEL), lambda b, i: (b, i, 0)),
        out_shape=jax.ShapeDtypeStruct((B, S, D_MODEL), _F32),
        scratch_shapes=[pltpu.VMEM((tm * ROW_CHUNKS, LANES), _F32), pltpu.SemaphoreType.DMA(())],
        compiler_params=_params(("arbitrary", "arbitrary")),
        name="moe_combine",
    )(slot, x1, mod_l, gain_final, ys)


def _rope_tables(S):
    inv = 1.0 / (ROPE_THETA ** (jnp.arange(0, HEAD_DIM, 2, dtype=_F32) / HEAD_DIM))
    ang = jnp.arange(S, dtype=_F32)[:, None] * inv[None, :]
    cos, sin = jnp.cos(ang), jnp.sin(ang)
    reps = LANES // HEAD_DIM
    return (jnp.tile(jnp.concatenate([cos, cos], axis=1), (1, reps)),
            jnp.tile(jnp.concatenate([-sin, sin], axis=1), (1, reps)))


def _trunk(x, batch_offset, mod, w):
    B, S, _ = x.shape
    cos, sin_signed = _rope_tables(S)
    for l in range(DEPTH):
        mod_l = mod[l]
        outs = _in_proj(x, mod_l, w["norm_mix"][l], w["w_in"][l], cos, sin_signed, batch_offset)
        q, k, v, u = outs[0:3], outs[3:6], outs[6:9], outs[9]
        attn, lses = [], []
        for g, (window, dilation) in enumerate(DILATED_GROUPS):
            o, lse = _attention(q[g], k[g], v[g], window, dilation)
            attn.append(o)
            lses.append(lse)
        x1, h2, cls = _mix_out(x, attn, lses, u, mod_l, w["wpool_bd"][l], w["pool_scale"][l], w["w_out"][l],
                               w["norm_ffn"][l], w["wr_hi"], w["wr_lo"], w["rbias"], batch_offset)
        slot, tbl, n_tiles = _moe_plan(cls, MOE_TILE)
        hs = _moe_dispatch(tbl, slot, h2, n_tiles, MOE_TILE)
        ys = _moe_experts(tbl, hs, w["wr_hi3"], w["wr_lo3"], w["w_gate"][l], w["w_up"][l], w["w_down"][l],
                          n_tiles, MOE_TILE)
        x = _moe_combine(slot, x1, mod_l, w["norm_final"], ys, batch_offset, final_norm=(l == DEPTH - 1))
    return x


def kernel(x_prompt, x_sample, c_prompt, c_sample, norm_mix, w_mod, b_mod, w_in, w_pool, pool_scale, w_out,
           norm_ffn, w_router, router_bias, w_gate, w_up, w_down, norm_final):
    n_prompt = x_prompt.shape[0]
    c = jnp.concatenate([c_prompt, c_sample], axis=0)
    mod = _modulation(c, w_mod, b_mod).reshape(DEPTH, c.shape[0], 6, D_MODEL)
    n_pool = len(POOL_WINDOWS)
    eye = jnp.eye(n_pool, dtype=w_pool.dtype)
    wpool_bd = (w_pool[:, :, :, None, :] * eye[None, :, None, :, None]).reshape(DEPTH, POOL_W, POOL_W)
    wr_t = w_router.T.astype(_F32)
    wr_hi = wr_t.astype(_BF16)
    wr_lo = (wr_t - wr_hi.astype(_F32)).astype(_BF16)
    w = {
        "norm_mix": norm_mix.reshape(DEPTH, 1, D_MODEL),
        "norm_ffn": norm_ffn.reshape(DEPTH, 1, D_MODEL),
        "norm_final": norm_final.reshape(1, D_MODEL),
        "w_in": w_in.astype(_BF16),
        "w_out": w_out.astype(_BF16),
        "wpool_bd": wpool_bd.astype(_BF16),
        "pool_scale": pool_scale.reshape(DEPTH, 1, POOL_W),
        "wr_hi": wr_hi,
        "wr_lo": wr_lo,
        "wr_hi3": wr_hi.astype(_F32).reshape(N_EXPERTS, 1, D_MODEL),
        "wr_lo3": wr_lo.astype(_F32).reshape(N_EXPERTS, 1, D_MODEL),
        "rbias": router_bias.astype(_F32).reshape(N_EXPERTS, 1),
        "w_gate": w_gate.astype(_BF16),
        "w_up": w_up.astype(_BF16),
        "w_down": w_down.astype(_BF16),
    }
    y_prompt = _trunk(x_prompt, 0, mod, w)
    y_sample = _trunk(x_sample, n_prompt, mod, w)
    return (y_prompt, y_sample)
```

```python
import functools

import jax
import jax.numpy as jnp
from jax import lax
from jax.experimental import pallas as pl
from jax.experimental.pallas import tpu as pltpu

D_MODEL = 1024
DEPTH = 4
HEAD_DIM = 64
HALF_HEAD = HEAD_DIM // 2
DILATED_GROUPS = ((128, 1), (512, 4), (2048, 16))
N_GROUPS = len(DILATED_GROUPS)
HEADS_PER_GROUP = 4
GROUP_W = HEADS_PER_GROUP * HEAD_DIM
ATT_W = N_GROUPS * GROUP_W
POOL_WINDOWS = (2, 4, 8, 16)
POOL_GROUP_W = 64
POOL_W = POOL_GROUP_W * len(POOL_WINDOWS)
IN_W = 3 * ATT_W + POOL_W
ROPE_THETA = 10000.0
RMS_EPS = 1e-6
N_EXPERTS = 16
N_EXPERT_GROUPS = 4
EXPERTS_PER_GROUP = N_EXPERTS // N_EXPERT_GROUPS
PAIRS_PER_GROUP = EXPERTS_PER_GROUP * (EXPERTS_PER_GROUP - 1) // 2
N_CLASSES = N_EXPERT_GROUPS * PAIRS_PER_GROUP
D_EXPERT = 512

LANES = 128
SUBLANES = 8
Q_BLOCK = 128
VMEM_LIMIT = 48 * 1024 * 1024

TOKEN_TILE = 512
MOE_TILE = 256
PLAN_CHUNK = 512
CLASS_ROWS = 32
ROW_CHUNKS = D_MODEL // LANES

_BF16 = jnp.bfloat16
_F32 = jnp.float32
_I32 = jnp.int32


def _params(semantics):
    return pltpu.CompilerParams(dimension_semantics=semantics, vmem_limit_bytes=VMEM_LIMIT)


def _mod_kernel(c_ref, w_ref, b_ref, o_ref):
    c = c_ref[...]
    sc = c / (1.0 + jnp.exp(-c))
    o_ref[0] = jnp.dot(sc, w_ref[0], preferred_element_type=_F32,
                       precision=lax.Precision.HIGHEST) + b_ref[0]


def _modulation(c, w_mod, b_mod):
    nb = c.shape[0]
    col = D_MODEL
    n_col = w_mod.shape[2] // col
    return pl.pallas_call(
        _mod_kernel,
        grid=(DEPTH, n_col),
        in_specs=[
            pl.BlockSpec((nb, D_MODEL), lambda l, j: (0, 0)),
            pl.BlockSpec((1, D_MODEL, col), lambda l, j: (l, 0, j)),
            pl.BlockSpec((1, 1, col), lambda l, j: (l, 0, j)),
        ],
        out_specs=pl.BlockSpec((1, nb, col), lambda l, j: (l, 0, j)),
        out_shape=jax.ShapeDtypeStruct((DEPTH, nb, w_mod.shape[2]), _F32),
        compiler_params=_params(("arbitrary", "arbitrary")),
        name="modulation",
    )(c, w_mod, b_mod.reshape(DEPTH, 1, -1))


def _rmsnorm_mod(x, gain, scale, shift):
    y = x * lax.rsqrt(jnp.mean(x * x, axis=-1, keepdims=True) + RMS_EPS)
    return (y * gain) * (1.0 + scale) + shift


def _rope_chunk(t, cos, sin_signed, first_half):
    fwd = pltpu.roll(t, HALF_HEAD, axis=1)
    bwd = pltpu.roll(t, LANES - HALF_HEAD, axis=1)
    return t * cos + jnp.where(first_half, bwd, fwd) * sin_signed


def _store_by_residue(out_ref, chunk, value, dilation, stage_ref):
    if dilation == 1:
        out_ref[0, :, chunk * LANES:(chunk + 1) * LANES] = value.astype(_BF16)
        return
    rows = value.shape[0] // dilation
    stage_ref[...] = value
    for r in range(dilation):
        lo = r * GROUP_W + chunk * LANES
        out_ref[0, :, lo:lo + LANES] = stage_ref[pl.ds(r, rows, stride=dilation), :].astype(_BF16)


def _in_proj_kernel(x_ref, mod_ref, gain_ref, w_ref, cos_ref, sin_ref, *refs):
    q_refs, k_refs, v_refs, u_ref, stage_ref = refs[0:3], refs[3:6], refs[6:9], refs[9], refs[10]
    mod = mod_ref[0]
    h = _rmsnorm_mod(x_ref[0], gain_ref[...], mod[1:2], mod[0:1])
    proj = jnp.dot(h.astype(_BF16), w_ref[...], preferred_element_type=_F32)
    cos = cos_ref[...]
    sin_signed = sin_ref[...]
    lane = lax.broadcasted_iota(_I32, (1, LANES), 1)
    first_half = (lane % HEAD_DIM) < HALF_HEAD
    q_scale = HEAD_DIM ** -0.5
    for g, (_, dilation) in enumerate(DILATED_GROUPS):
        for c in range(GROUP_W // LANES):
            lo = g * GROUP_W + c * LANES
            q = _rope_chunk(proj[:, lo:lo + LANES], cos, sin_signed, first_half)
            _store_by_residue(q_refs[g], c, q * q_scale, dilation, stage_ref)
            k = _rope_chunk(proj[:, ATT_W + lo:ATT_W + lo + LANES], cos, sin_signed, first_half)
            _store_by_residue(k_refs[g], c, k, dilation, stage_ref)
            _store_by_residue(v_refs[g], c, proj[:, 2 * ATT_W + lo:2 * ATT_W + lo + LANES], dilation, stage_ref)
    u_ref[0] = proj[:, 3 * ATT_W:]


def _in_proj(x, mod_l, gain, w_in_bf16, cos, sin_signed, batch_offset):
    B, S, _ = x.shape
    tm = TOKEN_TILE
    grp = [jax.ShapeDtypeStruct((B, S // d, d * GROUP_W), _BF16) for _, d in DILATED_GROUPS]
    grp_spec = [pl.BlockSpec((1, tm // d, d * GROUP_W), lambda b, i: (b, i, 0)) for _, d in DILATED_GROUPS]
    return pl.pallas_call(
        _in_proj_kernel,
        grid=(B, S // tm),
        in_specs=[
            pl.BlockSpec((1, tm, D_MODEL), lambda b, i: (b, i, 0)),
            pl.BlockSpec((1, 6, D_MODEL), lambda b, i: (b + batch_offset, 0, 0)),
            pl.BlockSpec((1, D_MODEL), lambda b, i: (0, 0)),
            pl.BlockSpec((D_MODEL, IN_W), lambda b, i: (0, 0)),
            pl.BlockSpec((tm, LANES), lambda b, i: (i, 0)),
            pl.BlockSpec((tm, LANES), lambda b, i: (i, 0)),
        ],
        out_specs=grp_spec * 3 + [pl.BlockSpec((1, tm, POOL_W), lambda b, i: (b, i, 0))],
        out_shape=grp * 3 + [jax.ShapeDtypeStruct((B, S, POOL_W), _F32)],
        scratch_shapes=[pltpu.VMEM((tm, LANES), _F32)],
        compiler_params=_params(("arbitrary", "arbitrary")),
        name="in_proj",
    )(x, mod_l, gain, w_in_bf16, cos, sin_signed)


def _attn_kernel(q_ref, k_ref, v_ref, o_ref, lse_ref, bias_ref, *, length, n_side, n_res, q_unroll):
    rb = pl.program_id(1)
    key_w = min(length, Q_BLOCK + 2 * n_side)
    n_qb = length // Q_BLOCK

    @pl.when((pl.program_id(0) == 0) & (rb == 0))
    def _():
        i = lax.broadcasted_iota(_I32, (2 * Q_BLOCK, key_w), 0) % Q_BLOCK
        j = lax.broadcasted_iota(_I32, (2 * Q_BLOCK, key_w), 1)
        for n in range(3):
            bias_ref[n] = jnp.where(jnp.abs(i - j + n * n_side) <= n_side, 0.0, -jnp.inf)

    @pl.when(rb == 0)
    def _():
        lse_ref[...] = jnp.zeros_like(lse_ref)

    lane = lax.broadcasted_iota(_I32, (1, LANES), 1)
    head_mask = [(lane < HEAD_DIM).astype(_BF16), (lane >= HEAD_DIM).astype(_BF16)]
    lse_lane = lax.broadcasted_iota(_I32, (1, lse_ref.shape[2]), 1)

    def trip(it, carry):
        for u in range(q_unroll):
            q0 = pl.multiple_of((it * q_unroll + u) * Q_BLOCK, Q_BLOCK)
            k0 = pl.multiple_of(jnp.clip(q0 - n_side, 0, length - key_w), n_side)
            bias = bias_ref[(q0 - k0) // n_side]
            lse_rows = lse_ref[0, pl.ds(q0, Q_BLOCK), :]
            for rr in range(n_res):
                for pair in range(GROUP_W // LANES):
                    lo = rr * GROUP_W + pair * LANES
                    cols = slice(lo, lo + LANES)
                    q2 = q_ref[0, pl.ds(q0, Q_BLOCK), cols]
                    k2 = k_ref[0, pl.ds(k0, key_w), cols]
                    v2 = v_ref[0, pl.ds(k0, key_w), cols]
                    qs = jnp.concatenate([q2 * head_mask[0], q2 * head_mask[1]], axis=0)
                    s = lax.dot_general(qs, k2, (((1,), (1,)), ((), ())), preferred_element_type=_F32) + bias
                    m = jnp.max(s, axis=1, keepdims=True)
                    p = jnp.exp(s - m)
                    l = jnp.sum(p, axis=1, keepdims=True)
                    o = jnp.dot(p.astype(_BF16), v2, preferred_element_type=_F32) * (1.0 / l)
                    lse = m + jnp.log(l)
                    for hh in range(2):
                        col = (rb * n_res + rr) * HEADS_PER_GROUP + pair * 2 + hh
                        lse_rows = jnp.where(lse_lane == col, lse[hh * Q_BLOCK:(hh + 1) * Q_BLOCK], lse_rows)
                    o_ref[0, pl.ds(q0, Q_BLOCK), cols] = jnp.where(lane < HEAD_DIM, o[:Q_BLOCK], o[Q_BLOCK:]).astype(_BF16)
            lse_ref[0, pl.ds(q0, Q_BLOCK), :] = lse_rows
        return carry

    lax.fori_loop(0, n_qb // q_unroll, trip, 0)


ATTN_BLOCKS_PER_TRIP = 4


def _attention(q, k, v, window, dilation):
    B, L, _ = q.shape
    d = dilation
    n_side = window // (2 * d)
    n_res = min(d, ATTN_BLOCKS_PER_TRIP)
    q_unroll = min(ATTN_BLOCKS_PER_TRIP // n_res, L // Q_BLOCK)
    key_w = min(L, Q_BLOCK + 2 * n_side)
    spec = pl.BlockSpec((1, L, n_res * GROUP_W), lambda b, r: (b, 0, r))
    return pl.pallas_call(
        functools.partial(_attn_kernel, length=L, n_side=n_side, n_res=n_res, q_unroll=q_unroll),
        grid=(B, d // n_res),
        in_specs=[spec, spec, spec],
        out_specs=[spec, pl.BlockSpec((1, L, d * HEADS_PER_GROUP), lambda b, r: (b, 0, 0))],
        out_shape=[jax.ShapeDtypeStruct((B, L, d * GROUP_W), _BF16),
                   jax.ShapeDtypeStruct((B, L, d * HEADS_PER_GROUP), _F32)],
        scratch_shapes=[pltpu.VMEM((3, 2 * Q_BLOCK, key_w), _F32)],
        compiler_params=_params(("arbitrary", "arbitrary")),
        name=f"attention_d{d}",
    )(q, k, v)


def _route(sel):
    s = [sel[e:e + 1, :] for e in range(N_EXPERTS)]
    group_score = []
    for g in range(N_EXPERT_GROUPS):
        c0, c1, c2, c3 = s[EXPERTS_PER_GROUP * g:EXPERTS_PER_GROUP * (g + 1)]
        m1, n1 = jnp.maximum(c0, c1), jnp.minimum(c0, c1)
        m2, n2 = jnp.maximum(c2, c3), jnp.minimum(c2, c3)
        group_score.append(jnp.maximum(m1, m2) + jnp.maximum(jnp.minimum(m1, m2), jnp.maximum(n1, n2)))
    best, top_g = group_score[0], jnp.zeros(group_score[0].shape, _I32)
    for g in range(1, N_EXPERT_GROUPS):
        better = group_score[g] > best
        best = jnp.where(better, group_score[g], best)
        top_g = jnp.where(better, g, top_g)
    cs = []
    for j in range(EXPERTS_PER_GROUP):
        c = s[j]
        for g in range(1, N_EXPERT_GROUPS):
            c = jnp.where(top_g == g, s[EXPERTS_PER_GROUP * g + j], c)
        cs.append(c)
    s1, i1 = cs[0], jnp.zeros_like(top_g)
    for j in range(1, EXPERTS_PER_GROUP):
        better = cs[j] > s1
        s1 = jnp.where(better, cs[j], s1)
        i1 = jnp.where(better, j, i1)
    i2 = jnp.where(i1 == 0, 1, 0)
    s2 = jnp.where(i1 == 0, cs[1], cs[0])
    for j in range(1, EXPERTS_PER_GROUP):
        better = (cs[j] > s2) & (i1 != j)
        s2 = jnp.where(better, cs[j], s2)
        i2 = jnp.where(better, j, i2)
    lo, hi = jnp.minimum(i1, i2), jnp.maximum(i1, i2)
    pair = jnp.where(lo == 0, 0, jnp.where(lo == 1, 3, 5)) + (hi - lo - 1)
    return top_g * PAIRS_PER_GROUP + pair


def _mix_out_kernel(x_ref, o0_ref, o1_ref, o2_ref, l0_ref, l1_ref, l2_ref, u_ref, up_ref, un_ref,
                    mod_ref, wpool_ref, pscale_ref, wout_ref, gain_ref, wr_hi_ref, wr_lo_ref, rbias_ref,
                    x1_ref, h2_ref, cls_ref, ext_ref, lvl_a_ref, lvl_b_ref, nat_o1a_ref, nat_o1b_ref,
                    nat_o2a_ref, nat_o2b_ref, nat_l1_ref, nat_l2_ref, *, seq_len):
    nat_o_refs = ((nat_o1a_ref, nat_o1b_ref), (nat_o2a_ref, nat_o2b_ref))
    nat_l_refs = (nat_l1_ref, nat_l2_ref)
    i = pl.program_id(1)
    n_i = pl.num_programs(1)
    tm = x_ref.shape[1]
    mod = mod_ref[0]

    outs, lses = [], []
    for g, (o_ref, l_ref) in enumerate(((o0_ref, l0_ref), (o1_ref, l1_ref), (o2_ref, l2_ref))):
        dilation = DILATED_GROUPS[g][1]
        if dilation == 1:
            outs.append(o_ref[0].astype(_F32))
            lses.append(l_ref[0])
            continue
        rows = tm // dilation
        lv = l_ref[0]
        lv = jnp.concatenate([lv, jnp.zeros((rows, LANES - lv.shape[1]), _F32)], axis=1)
        halves = nat_o_refs[g - 1]
        for r in range(dilation):
            for c, half_ref in enumerate(halves):
                lo = r * GROUP_W + c * LANES
                half_ref[pl.ds(r, rows, stride=dilation), :] = o_ref[0, :, lo:lo + LANES].astype(_F32)
            shifted = lv if r == 0 else pltpu.roll(lv, LANES - r * HEADS_PER_GROUP, axis=1)
            nat_l_refs[g - 1][pl.ds(r, rows, stride=dilation), :] = shifted
        outs.append(jnp.concatenate([half_ref[...] for half_ref in halves], axis=1))
        lses.append(nat_l_refs[g - 1][:, 0:HEADS_PER_GROUP])

    top = jnp.maximum(jnp.maximum(lses[0], lses[1]), lses[2])
    es = [jnp.exp(t - top) for t in lses]
    den = es[0] + es[1] + es[2]
    pieces = []
    for g in range(N_GROUPS):
        alpha = es[g] / den
        wide = jnp.concatenate([jnp.broadcast_to(alpha[:, h:h + 1], (tm, HEAD_DIM))
                                for h in range(HEADS_PER_GROUP)], axis=1)
        pieces.append((outs[g] * wide).astype(_BF16))

    pad = 2 * SUBLANES
    zeros_pad = jnp.zeros((SUBLANES, POOL_W), _F32)
    for ref in (ext_ref, lvl_a_ref, lvl_b_ref):
        ref[0:SUBLANES, :] = zeros_pad
        ref[tm + pad + SUBLANES:tm + 2 * pad, :] = zeros_pad
    u = u_ref[0]
    ext_ref[SUBLANES:pad, :] = jnp.where(i > 0, up_ref[0], 0.0)
    ext_ref[pad:pad + tm, :] = u
    ext_ref[pad + tm:pad + tm + SUBLANES, :] = jnp.where(i < n_i - 1, un_ref[0], 0.0)
    span = tm + 2 * SUBLANES
    win = lambda ref, off: ref[SUBLANES + off:SUBLANES + off + span, :]
    lvl_a_ref[SUBLANES:SUBLANES + span, :] = win(ext_ref, -1) + win(ext_ref, 0)
    s2 = lvl_a_ref[pad:pad + tm, :]
    lvl_b_ref[SUBLANES:SUBLANES + span, :] = win(lvl_a_ref, -1) + win(lvl_a_ref, 1)
    s4 = lvl_b_ref[pad:pad + tm, :]
    lvl_a_ref[SUBLANES:SUBLANES + span, :] = win(lvl_b_ref, -2) + win(lvl_b_ref, 2)
    s8 = lvl_a_ref[pad:pad + tm, :]
    s16 = lvl_a_ref[pad - 4:pad - 4 + tm, :] + lvl_a_ref[pad + 4:pad + 4 + tm, :]
    lane = lax.broadcasted_iota(_I32, (1, POOL_W), 1)
    wgroup = lane // POOL_GROUP_W
    wsum = jnp.where(wgroup == 0, s2, jnp.where(wgroup == 1, s4, jnp.where(wgroup == 2, s8, s16)))
    half_w = jnp.where(wgroup == 0, 1, jnp.where(wgroup == 1, 2, jnp.where(wgroup == 2, 4, 8)))
    pos = i * tm + lax.broadcasted_iota(_I32, (tm, 1), 0)
    cnt = jnp.minimum(pos + half_w, seq_len) - jnp.maximum(pos - half_w, 0)
    z = wsum / cnt.astype(_F32) - u
    pool = jnp.dot(z.astype(_BF16), wpool_ref[...], preferred_element_type=_F32) * pscale_ref[...]
    pieces.append(pool.astype(_BF16))

    mix = jnp.dot(jnp.concatenate(pieces, axis=1), wout_ref[...], preferred_element_type=_F32)
    x1 = x_ref[0] + mod[2:3] * mix
    x1_ref[0] = x1

    h2 = _rmsnorm_mod(x1, gain_ref[...], mod[4:5], mod[3:4])
    h_hi = h2.astype(_BF16)
    h2_ref[0] = h_hi
    h_lo = (h2 - h_hi.astype(_F32)).astype(_BF16)
    nt = (((1,), (1,)), ((), ()))
    logits = (lax.dot_general(wr_hi_ref[...], h_hi, nt, preferred_element_type=_F32)
              + lax.dot_general(wr_lo_ref[...], h_hi, nt, preferred_element_type=_F32)
              + lax.dot_general(wr_hi_ref[...], h_lo, nt, preferred_element_type=_F32))
    aff = 1.0 / (1.0 + jnp.exp(-logits))
    cls_ref[0] = _route(aff + rbias_ref[...])


def _mix_out(x, attn, lses, u, mod_l, wpool_bd, pool_scale, w_out_bf16, gain, wr_hi, wr_lo, rbias, batch_offset):
    B, S, _ = x.shape
    tm = TOKEN_TILE
    n_i = S // tm
    halo_blocks = tm // SUBLANES
    tok = lambda w: pl.BlockSpec((1, tm, w), lambda b, i: (b, i, 0))
    const = lambda shape: pl.BlockSpec(shape, lambda b, i: tuple(0 for _ in shape))
    rows = tm + 4 * SUBLANES
    return pl.pallas_call(
        functools.partial(_mix_out_kernel, seq_len=S),
        grid=(B, n_i),
        in_specs=[
            tok(D_MODEL),
            *[pl.BlockSpec((1, tm // d, d * GROUP_W), lambda b, i: (b, i, 0)) for _, d in DILATED_GROUPS],
            *[pl.BlockSpec((1, tm // d, d * HEADS_PER_GROUP), lambda b, i: (b, i, 0)) for _, d in DILATED_GROUPS],
            tok(POOL_W),
            pl.BlockSpec((1, SUBLANES, POOL_W), lambda b, i: (b, jnp.maximum(i * halo_blocks - 1, 0), 0)),
            pl.BlockSpec((1, SUBLANES, POOL_W),
                         lambda b, i: (b, jnp.minimum((i + 1) * halo_blocks, S // SUBLANES - 1), 0)),
            pl.BlockSpec((1, 6, D_MODEL), lambda b, i: (b + batch_offset, 0, 0)),
            const((POOL_W, POOL_W)), const((1, POOL_W)), const((D_MODEL, D_MODEL)), const((1, D_MODEL)),
            const((N_EXPERTS, D_MODEL)), const((N_EXPERTS, D_MODEL)), const((N_EXPERTS, 1)),
        ],
        out_specs=[tok(D_MODEL), tok(D_MODEL), pl.BlockSpec((1, 1, tm), lambda b, i: (b, 0, i))],
        out_shape=[jax.ShapeDtypeStruct((B, S, D_MODEL), _F32),
                   jax.ShapeDtypeStruct((B, S, D_MODEL), _BF16),
                   jax.ShapeDtypeStruct((B, 1, S), _I32)],
        scratch_shapes=[pltpu.VMEM((rows, POOL_W), _F32)] * 3 + [pltpu.VMEM((tm, LANES), _F32)] * 6,
        compiler_params=_params(("arbitrary", "arbitrary")),
        name="mix_out",
    )(x, *attn, *lses, u, u, u, mod_l, wpool_bd, pool_scale, w_out_bf16, gain, wr_hi, wr_lo, rbias)


TBL_EXPERT_A, TBL_EXPERT_B, TBL_TILE, TBL_USED, TBL_LAST_TILE_ROW = range(5)


def _plan_kernel(cls_ref, slot_ref, tbl_ref, cnt_ref, run_ref, base_ref, *, tile):
    phase, b, j = pl.program_id(0), pl.program_id(1), pl.program_id(2)
    first = (b == 0) & (j == 0)
    chunk = cls_ref.shape[2]
    n_tbl = tbl_ref.shape[1]
    onehot = lax.broadcasted_iota(_I32, (CLASS_ROWS, chunk), 0) == cls_ref[0]
    hits = jnp.sum(onehot.astype(_F32), axis=1, keepdims=True)

    @pl.when((phase == 0) & first)
    def _():
        cnt_ref[...] = jnp.zeros_like(cnt_ref)

    @pl.when(phase == 0)
    def _():
        cnt_ref[...] += hits

    @pl.when((phase == 1) & first)
    def _():
        padded = jnp.ceil(cnt_ref[...] * (1.0 / tile)) * tile
        r_i = lax.broadcasted_iota(_I32, (CLASS_ROWS, CLASS_ROWS), 0)
        c_i = lax.broadcasted_iota(_I32, (CLASS_ROWS, CLASS_ROWS), 1)
        padded_row = jnp.sum(jnp.where(r_i == c_i, padded, 0.0), axis=0, keepdims=True)
        base_col = jnp.sum(jnp.where(c_i < r_i, padded_row, 0.0), axis=1, keepdims=True)
        base_row = jnp.sum(jnp.where(r_i < c_i, padded, 0.0), axis=0, keepdims=True)
        total = jnp.sum(padded_row, axis=1, keepdims=True)
        base_ref[...] = base_col
        run_ref[...] = jnp.zeros_like(run_ref)

        k_row = lax.broadcasted_iota(_I32, (1, n_tbl), 1).astype(_F32)
        start = k_row * tile
        inside = (base_col <= start) & (start < base_col + padded)
        cls_id = lax.broadcasted_iota(_I32, (CLASS_ROWS, n_tbl), 0).astype(_F32)
        tile_cls = jnp.sum(jnp.where(inside, cls_id, 0.0), axis=0, keepdims=True)
        n_used = total * (1.0 / tile)
        last = jnp.maximum(n_used - 1.0, 0.0)
        last_cls = jnp.sum(jnp.where(k_row == last, tile_cls, 0.0), axis=1, keepdims=True)
        used = k_row < n_used
        tile_cls = jnp.where(used, tile_cls, last_cls)
        grp = sum(jnp.where(tile_cls >= PAIRS_PER_GROUP * g, 1.0, 0.0) for g in range(1, N_EXPERT_GROUPS))
        pair = tile_cls - grp * PAIRS_PER_GROUP
        a = jnp.where(pair < 3, 0.0, jnp.where(pair < 5, 1.0, 2.0))
        bb = jnp.where(pair < 3, pair + 1.0, jnp.where(pair < 5, pair - 1.0, 3.0))
        last_tile_row = jnp.where(padded_row > 0, base_row + padded_row - tile, -1.0)
        last_tile_row = jnp.concatenate(
            [last_tile_row, jnp.full((1, n_tbl - CLASS_ROWS), -1.0, _F32)], axis=1)
        zero = jnp.zeros((1, n_tbl), _F32)
        tbl_ref[...] = jnp.concatenate(
            [grp * EXPERTS_PER_GROUP + a, grp * EXPERTS_PER_GROUP + bb, jnp.where(used, k_row, last),
             jnp.where(used, 1.0, 0.0), last_tile_row, zero, zero, zero], axis=0).astype(_I32)

    @pl.when(phase == 1)
    def _():
        before = lax.broadcasted_iota(_I32, (chunk, chunk), 0) < lax.broadcasted_iota(_I32, (chunk, chunk), 1)
        earlier = jnp.dot(jnp.where(onehot, 1.0, 0.0).astype(_BF16), jnp.where(before, 1.0, 0.0).astype(_BF16),
                          preferred_element_type=_F32)
        dest = base_ref[...] + run_ref[...] + earlier
        slot_ref[0] = jnp.sum(jnp.where(onehot, dest, 0.0), axis=0, keepdims=True).astype(_I32)
        run_ref[...] += hits


def _moe_plan(cls, tile):
    B, _, S = cls.shape
    chunk = PLAN_CHUNK
    n_tiles = (B * S) // tile + N_CLASSES
    n_tbl = -(-n_tiles // LANES) * LANES
    slot, tbl = pl.pallas_call(
        functools.partial(_plan_kernel, tile=tile),
        grid=(2, B, S // chunk),
        in_specs=[pl.BlockSpec((1, 1, chunk), lambda p, b, j: (b, 0, j))],
        out_specs=[pl.BlockSpec((1, 1, chunk), lambda p, b, j: (b * p, 0, j * p)),
                   pl.BlockSpec((SUBLANES, n_tbl), lambda p, b, j: (0, 0))],
        out_shape=[jax.ShapeDtypeStruct((B, 1, S), _I32), jax.ShapeDtypeStruct((SUBLANES, n_tbl), _I32)],
        scratch_shapes=[pltpu.VMEM((CLASS_ROWS, 1), _F32)] * 3,
        compiler_params=_params(("arbitrary", "arbitrary", "arbitrary")),
        name="moe_plan",
    )(cls)
    return slot, tbl, n_tiles


def _row_copy(src, src_row, dst, dst_row, sem):
    return pltpu.make_async_copy(src.at[pl.ds(pl.multiple_of(src_row * ROW_CHUNKS, ROW_CHUNKS), ROW_CHUNKS), :],
                                 dst.at[pl.ds(pl.multiple_of(dst_row * ROW_CHUNKS, ROW_CHUNKS), ROW_CHUNKS), :],
                                 sem)


def _dispatch_kernel(tbl_ref, slot_ref, h_ref, hs_ref, stage_ref, zero_ref, sem, zsem, *, tile):
    first = (pl.program_id(0) == 0) & (pl.program_id(1) == 0)
    tm = h_ref.shape[1]
    tile_rows = tile * ROW_CHUNKS

    @pl.when(first)
    def _():
        zero_ref[...] = jnp.zeros_like(zero_ref)
        n_tiles = hs_ref.shape[0] // tile_rows
        for wait in (False, True):
            for c in range(N_CLASSES):
                row = tbl_ref[TBL_LAST_TILE_ROW, c]

                @pl.when(row >= 0)
                def _():
                    at = pl.multiple_of(row * ROW_CHUNKS, ROW_CHUNKS)
                    cp = pltpu.make_async_copy(zero_ref, hs_ref.at[pl.ds(at, tile_rows), :], zsem)
                    cp.wait() if wait else cp.start()

            for k in range(n_tiles - N_CLASSES, n_tiles):
                @pl.when(tbl_ref[TBL_USED, k] == 0)
                def _():
                    cp = pltpu.make_async_copy(zero_ref, hs_ref.at[pl.ds(k * tile_rows, tile_rows), :], zsem)
                    cp.wait() if wait else cp.start()

    h = h_ref[0]
    for c in range(ROW_CHUNKS):
        stage_ref[pl.ds(c, tm, stride=ROW_CHUNKS), :] = h[:, c * LANES:(c + 1) * LANES].astype(_F32)

    def issue(g, carry):
        for k in range(SUBLANES):
            r = g * SUBLANES + k
            _row_copy(stage_ref, r, hs_ref, slot_ref[0, 0, r], sem).start()
        return carry

    lax.fori_loop(0, tm // SUBLANES, issue, 0)
    pltpu.make_async_copy(stage_ref, hs_ref.at[pl.ds(0, tm * ROW_CHUNKS), :], sem).wait()


def _moe_dispatch(tbl, slot, h2, n_tiles, tile):
    B, S, _ = h2.shape
    tm = TOKEN_TILE
    return pl.pallas_call(
        functools.partial(_dispatch_kernel, tile=tile),
        grid_spec=pltpu.PrefetchScalarGridSpec(
            num_scalar_prefetch=1,
            grid=(B, S // tm),
            in_specs=[pl.BlockSpec((1, 1, tm), lambda b, i, tbl: (b, 0, i), memory_space=pltpu.SMEM),
                      pl.BlockSpec((1, tm, D_MODEL), lambda b, i, tbl: (b, i, 0))],
            out_specs=pl.BlockSpec(memory_space=pl.ANY),
            scratch_shapes=[pltpu.VMEM((tm * ROW_CHUNKS, LANES), _F32),
                            pltpu.VMEM((tile * ROW_CHUNKS, LANES), _F32),
                            pltpu.SemaphoreType.DMA(()), pltpu.SemaphoreType.DMA(())],
        ),
        out_shape=jax.ShapeDtypeStruct((n_tiles * tile * ROW_CHUNKS, LANES), _F32),
        compiler_params=_params(("arbitrary", "arbitrary")),
        name="moe_dispatch",
    )(tbl, slot, h2)


def _experts_kernel(tbl_ref, hs_ref, wra_hi_ref, wra_lo_ref, wrb_hi_ref, wrb_lo_ref,
                    wga_ref, wua_ref, wda_ref, wgb_ref, wub_ref, wdb_ref, ys_ref):
    k = pl.program_id(0)
    tile = hs_ref.shape[0] // ROW_CHUNKS

    @pl.when(tbl_ref[TBL_USED, k] == 0)
    def _():
        ys_ref[...] = jnp.zeros_like(ys_ref)

    @pl.when(tbl_ref[TBL_USED, k] == 1)
    def _():
        x = jnp.concatenate([hs_ref[pl.ds(c, tile, stride=ROW_CHUNKS), :].astype(_BF16)
                             for c in range(ROW_CHUNKS)], axis=1)
        row_id = lax.broadcasted_iota(_I32, (2 * SUBLANES, D_MODEL), 0)
        rows = jnp.zeros((2 * SUBLANES, D_MODEL), _F32)
        for n, ref in enumerate((wra_hi_ref, wra_lo_ref, wrb_hi_ref, wrb_lo_ref)):
            rows = jnp.where(row_id == n, ref[0], rows)
        lg = lax.dot_general(x, rows.astype(_BF16), (((1,), (1,)), ((), ())),
                             preferred_element_type=_F32)
        aff_a = 1.0 / (1.0 + jnp.exp(-(lg[:, 0:1] + lg[:, 1:2])))
        aff_b = 1.0 / (1.0 + jnp.exp(-(lg[:, 2:3] + lg[:, 3:4])))
        den = aff_a + aff_b

        def hidden(wg_ref, wu_ref, gate):
            a = jnp.dot(x, wg_ref[0], preferred_element_type=_F32)
            b = jnp.dot(x, wu_ref[0], preferred_element_type=_F32)
            return ((a / (1.0 + jnp.exp(-a))) * b * gate).astype(_BF16)

        y = (jnp.dot(hidden(wga_ref, wua_ref, aff_a / den), wda_ref[0], preferred_element_type=_F32)
             + jnp.dot(hidden(wgb_ref, wub_ref, aff_b / den), wdb_ref[0], preferred_element_type=_F32))
        for c in range(ROW_CHUNKS):
            ys_ref[pl.ds(c, tile, stride=ROW_CHUNKS), :] = y[:, c * LANES:(c + 1) * LANES]


def _moe_experts(tbl, hs, wr_hi, wr_lo, wg, wu, wd, n_tiles, tile):
    rows = tile * ROW_CHUNKS
    tile_spec = pl.BlockSpec((rows, LANES), lambda k, tbl: (tbl[TBL_TILE, k], 0))
    router = lambda row: pl.BlockSpec((1, 1, D_MODEL), lambda k, tbl: (tbl[row, k], 0, 0))
    up = lambda row: pl.BlockSpec((1, D_MODEL, D_EXPERT), lambda k, tbl: (tbl[row, k], 0, 0))
    down = lambda row: pl.BlockSpec((1, D_EXPERT, D_MODEL), lambda k, tbl: (tbl[row, k], 0, 0))
    a, b = TBL_EXPERT_A, TBL_EXPERT_B
    return pl.pallas_call(
        _experts_kernel,
        grid_spec=pltpu.PrefetchScalarGridSpec(
            num_scalar_prefetch=1,
            grid=(n_tiles,),
            in_specs=[tile_spec, router(a), router(a), router(b), router(b),
                      up(a), up(a), down(a), up(b), up(b), down(b)],
            out_specs=pl.BlockSpec((rows, LANES), lambda k, tbl: (k, 0)),
        ),
        out_shape=jax.ShapeDtypeStruct(hs.shape, _F32),
        compiler_params=_params(("arbitrary",)),
        name="moe_experts",
    )(tbl, hs, wr_hi, wr_lo, wr_hi, wr_lo, wg, wu, wd, wg, wu, wd)


def _combine_kernel(slot_ref, x1_ref, mod_ref, gfin_ref, ys_ref, o_ref, buf_ref, sem, *, final_norm):
    tm = x1_ref.shape[1]

    def issue(g, carry):
        for k in range(SUBLANES):
            r = g * SUBLANES + k
            _row_copy(ys_ref, slot_ref[0, 0, r], buf_ref, r, sem).start()
        return carry

    lax.fori_loop(0, tm // SUBLANES, issue, 0)
    pltpu.make_async_copy(ys_ref.at[pl.ds(0, tm * ROW_CHUNKS), :], buf_ref, sem).wait()

    gate = mod_ref[0][5:6]
    x1 = x1_ref[0]
    x2 = jnp.concatenate(
        [x1[:, c * LANES:(c + 1) * LANES]
         + gate[:, c * LANES:(c + 1) * LANES] * buf_ref[pl.ds(c, tm, stride=ROW_CHUNKS), :]
         for c in range(ROW_CHUNKS)], axis=1)
    if final_norm:
        x2 = x2 * lax.rsqrt(jnp.mean(x2 * x2, axis=-1, keepdims=True) + RMS_EPS) * gfin_ref[...]
    o_ref[0] = x2


def _moe_combine(slot, x1, mod_l, gain_final, ys, batch_offset, final_norm):
    B, S, _ = x1.shape
    tm = TOKEN_TILE
    return pl.pallas_call(
        functools.partial(_combine_kernel, final_norm=final_norm),
        grid=(B, S // tm),
        in_specs=[pl.BlockSpec((1, 1, tm), lambda b, i: (b, 0, i), memory_space=pltpu.SMEM),
                  pl.BlockSpec((1, tm, D_MODEL), lambda b, i: (b, i, 0)),
                  pl.BlockSpec((1, 6, D_MODEL), lambda b, i: (b + batch_offset, 0, 0)),
                  pl.BlockSpec((1, D_MODEL), lambda b, i: (0, 0)),
                  pl.BlockSpec(memory_space=pl.ANY)],
        out_specs=pl.BlockSpec((1, tm, D_MODEL), lambda b, i: (b, i, 0)),
        out_shape=jax.ShapeDtypeStruct((B, S, D_MODEL), _F32),
        scratch_shapes=[pltpu.VMEM((tm * ROW_CHUNKS, LANES), _F32), pltpu.SemaphoreType.DMA(())],
        compiler_params=_params(("arbitrary", "arbitrary")),
        name="moe_combine",
    )(slot, x1, mod_l, gain_final, ys)


def _rope_tables(S):
    inv = 1.0 / (ROPE_THETA ** (jnp.arange(0, HEAD_DIM, 2, dtype=_F32) / HEAD_DIM))
    ang = jnp.arange(S, dtype=_F32)[:, None] * inv[None, :]
    cos, sin = jnp.cos(ang), jnp.sin(ang)
    reps = LANES // HEAD_DIM
    return (jnp.tile(jnp.concatenate([cos, cos], axis=1), (1, reps)),
            jnp.tile(jnp.concatenate([-sin, sin], axis=1), (1, reps)))


def _trunk(x, batch_offset, mod, w):
    cos, sin_signed = _rope_tables(x.shape[1])
    for l in range(DEPTH):
        mod_l = mod[l]
        outs = _in_proj(x, mod_l, w["norm_mix"][l], w["w_in"][l], cos, sin_signed, batch_offset)
        q, k, v, u = outs[0:3], outs[3:6], outs[6:9], outs[9]
        attn, lses = [], []
        for g, (window, dilation) in enumerate(DILATED_GROUPS):
            o, lse = _attention(q[g], k[g], v[g], window, dilation)
            attn.append(o)
            lses.append(lse)
        x1, h2, cls = _mix_out(x, attn, lses, u, mod_l, w["wpool_bd"][l], w["pool_scale"][l], w["w_out"][l],
                               w["norm_ffn"][l], w["wr_hi"], w["wr_lo"], w["rbias"], batch_offset)
        slot, tbl, n_tiles = _moe_plan(cls, MOE_TILE)
        hs = _moe_dispatch(tbl, slot, h2, n_tiles, MOE_TILE)
        ys = _moe_experts(tbl, hs, w["wr_hi3"], w["wr_lo3"], w["w_gate"][l], w["w_up"][l], w["w_down"][l],
                          n_tiles, MOE_TILE)
        x = _moe_combine(slot, x1, mod_l, w["norm_final"], ys, batch_offset, final_norm=(l == DEPTH - 1))
    return x


def kernel(x_prompt, x_sample, c_prompt, c_sample, norm_mix, w_mod, b_mod, w_in, w_pool, pool_scale, w_out,
           norm_ffn, w_router, router_bias, w_gate, w_up, w_down, norm_final):
    n_prompt = x_prompt.shape[0]
    c = jnp.concatenate([c_prompt, c_sample], axis=0)
    mod = _modulation(c, w_mod, b_mod).reshape(DEPTH, c.shape[0], 6, D_MODEL)
    n_pool = len(POOL_WINDOWS)
    eye = jnp.eye(n_pool, dtype=w_pool.dtype)
    wpool_bd = (w_pool[:, :, :, None, :] * eye[None, :, None, :, None]).reshape(DEPTH, POOL_W, POOL_W)
    wr_t = w_router.T.astype(_F32)
    wr_hi = wr_t.astype(_BF16)
    wr_lo = (wr_t - wr_hi.astype(_F32)).astype(_BF16)
    w = {
        "norm_mix": norm_mix.reshape(DEPTH, 1, D_MODEL),
        "norm_ffn": norm_ffn.reshape(DEPTH, 1, D_MODEL),
        "norm_final": norm_final.reshape(1, D_MODEL),
        "w_in": w_in.astype(_BF16),
        "w_out": w_out.astype(_BF16),
        "wpool_bd": wpool_bd.astype(_BF16),
        "pool_scale": pool_scale.reshape(DEPTH, 1, POOL_W),
        "wr_hi": wr_hi,
        "wr_lo": wr_lo,
        "wr_hi3": wr_hi.astype(_F32).reshape(N_EXPERTS, 1, D_MODEL),
        "wr_lo3": wr_lo.astype(_F32).reshape(N_EXPERTS, 1, D_MODEL),
        "rbias": router_bias.astype(_F32).reshape(N_EXPERTS, 1),
        "w_gate": w_gate.astype(_BF16),
        "w_up": w_up.astype(_BF16),
        "w_down": w_down.astype(_BF16),
    }
    y_prompt = _trunk(x_prompt, 0, mod, w)
    y_sample = _trunk(x_sample, n_prompt, mod, w)
    return (y_prompt, y_sample)
```

```python
import functools

import jax
import jax.numpy as jnp
from jax import lax
from jax.experimental import pallas as pl
from jax.experimental.pallas import tpu as pltpu

D_MODEL = 1024
DEPTH = 4
HEAD_DIM = 64
HALF_HEAD = HEAD_DIM // 2
DILATED_GROUPS = ((128, 1), (512, 4), (2048, 16))
N_GROUPS = len(DILATED_GROUPS)
HEADS_PER_GROUP = 4
GROUP_W = HEADS_PER_GROUP * HEAD_DIM
ATT_W = N_GROUPS * GROUP_W
POOL_WINDOWS = (2, 4, 8, 16)
POOL_GROUP_W = 64
POOL_W = POOL_GROUP_W * len(POOL_WINDOWS)
IN_W = 3 * ATT_W + POOL_W
ROPE_THETA = 10000.0
RMS_EPS = 1e-6
N_EXPERTS = 16
N_EXPERT_GROUPS = 4
EXPERTS_PER_GROUP = N_EXPERTS // N_EXPERT_GROUPS
PAIRS_PER_GROUP = EXPERTS_PER_GROUP * (EXPERTS_PER_GROUP - 1) // 2
N_CLASSES = N_EXPERT_GROUPS * PAIRS_PER_GROUP
D_EXPERT = 512

LANES = 128
SUBLANES = 8
Q_BLOCK = 128
VMEM_LIMIT = 48 * 1024 * 1024

TOKEN_TILE = 512
MOE_TILE = 256
PLAN_CHUNK = 512
CLASS_ROWS = 32
ROW_CHUNKS = D_MODEL // LANES

_BF16 = jnp.bfloat16
_F32 = jnp.float32
_I32 = jnp.int32


def _params(semantics):
    return pltpu.CompilerParams(dimension_semantics=semantics, vmem_limit_bytes=VMEM_LIMIT)


def _mod_kernel(c_ref, w_ref, b_ref, o_ref):
    c = c_ref[...]
    sc = c / (1.0 + jnp.exp(-c))
    o_ref[0] = jnp.dot(sc, w_ref[0], preferred_element_type=_F32,
                       precision=lax.Precision.HIGHEST) + b_ref[0]


def _modulation(c, w_mod, b_mod):
    nb = c.shape[0]
    col = D_MODEL
    n_col = w_mod.shape[2] // col
    return pl.pallas_call(
        _mod_kernel,
        grid=(DEPTH, n_col),
        in_specs=[
            pl.BlockSpec((nb, D_MODEL), lambda l, j: (0, 0)),
            pl.BlockSpec((1, D_MODEL, col), lambda l, j: (l, 0, j)),
            pl.BlockSpec((1, 1, col), lambda l, j: (l, 0, j)),
        ],
        out_specs=pl.BlockSpec((1, nb, col), lambda l, j: (l, 0, j)),
        out_shape=jax.ShapeDtypeStruct((DEPTH, nb, w_mod.shape[2]), _F32),
        compiler_params=_params(("arbitrary", "arbitrary")),
        name="modulation",
    )(c, w_mod, b_mod.reshape(DEPTH, 1, -1))


def _rmsnorm_mod(x, gain, scale, shift):
    y = x * lax.rsqrt(jnp.mean(x * x, axis=-1, keepdims=True) + RMS_EPS)
    return (y * gain) * (1.0 + scale) + shift


def _rope_chunk(t, cos, sin_signed, first_half):
    fwd = pltpu.roll(t, HALF_HEAD, axis=1)
    bwd = pltpu.roll(t, LANES - HALF_HEAD, axis=1)
    return t * cos + jnp.where(first_half, bwd, fwd) * sin_signed


def _store_by_residue(out_ref, chunk, value, dilation, stage_ref, row0):
    rows = value.shape[0] // dilation
    at = slice(row0 // dilation, row0 // dilation + rows)
    if dilation == 1:
        out_ref[0, at, chunk * LANES:(chunk + 1) * LANES] = value.astype(_BF16)
        return
    stage_ref[...] = value
    for r in range(dilation):
        lo = r * GROUP_W + chunk * LANES
        out_ref[0, at, lo:lo + LANES] = stage_ref[pl.ds(r, rows, stride=dilation), :].astype(_BF16)


def _in_proj_kernel(x_ref, mod_ref, gain_ref, w_ref, cos_ref, sin_ref, *refs):
    _project(x_ref[0], 0, mod_ref[0], gain_ref, w_ref, cos_ref, sin_ref, refs[0:3], refs[3:6], refs[6:9], refs[9],
             refs[10])


def _project(x, row0, mod, gain_ref, w_ref, cos_ref, sin_ref, q_refs, k_refs, v_refs, u_ref, stage_ref):
    n = x.shape[0]
    h = _rmsnorm_mod(x, gain_ref[...], mod[1:2], mod[0:1])
    proj = jnp.dot(h.astype(_BF16), w_ref[...], preferred_element_type=_F32)
    cos = cos_ref[row0:row0 + n, :]
    sin_signed = sin_ref[row0:row0 + n, :]
    lane = lax.broadcasted_iota(_I32, (1, LANES), 1)
    first_half = (lane % HEAD_DIM) < HALF_HEAD
    q_scale = HEAD_DIM ** -0.5
    for g, (_, dilation) in enumerate(DILATED_GROUPS):
        for c in range(GROUP_W // LANES):
            lo = g * GROUP_W + c * LANES
            q = _rope_chunk(proj[:, lo:lo + LANES], cos, sin_signed, first_half)
            _store_by_residue(q_refs[g], c, q * q_scale, dilation, stage_ref, row0)
            k = _rope_chunk(proj[:, ATT_W + lo:ATT_W + lo + LANES], cos, sin_signed, first_half)
            _store_by_residue(k_refs[g], c, k, dilation, stage_ref, row0)
            _store_by_residue(v_refs[g], c, proj[:, 2 * ATT_W + lo:2 * ATT_W + lo + LANES], dilation, stage_ref,
                              row0)
    u_ref[0, row0:row0 + n, :] = proj[:, 3 * ATT_W:]


def _in_proj(x, mod_l, gain, w_in_bf16, cos, sin_signed, batch_offset):
    B, S, _ = x.shape
    tm = TOKEN_TILE
    grp = [jax.ShapeDtypeStruct((B, S // d, d * GROUP_W), _BF16) for _, d in DILATED_GROUPS]
    grp_spec = [pl.BlockSpec((1, tm // d, d * GROUP_W), lambda b, i: (b, i, 0)) for _, d in DILATED_GROUPS]
    return pl.pallas_call(
        _in_proj_kernel,
        grid=(B, S // tm),
        in_specs=[
            pl.BlockSpec((1, tm, D_MODEL), lambda b, i: (b, i, 0)),
            pl.BlockSpec((1, 6, D_MODEL), lambda b, i: (b + batch_offset, 0, 0)),
            pl.BlockSpec((1, D_MODEL), lambda b, i: (0, 0)),
            pl.BlockSpec((D_MODEL, IN_W), lambda b, i: (0, 0)),
            pl.BlockSpec((tm, LANES), lambda b, i: (i, 0)),
            pl.BlockSpec((tm, LANES), lambda b, i: (i, 0)),
        ],
        out_specs=grp_spec * 3 + [pl.BlockSpec((1, tm, POOL_W), lambda b, i: (b, i, 0))],
        out_shape=grp * 3 + [jax.ShapeDtypeStruct((B, S, POOL_W), _F32)],
        scratch_shapes=[pltpu.VMEM((tm, LANES), _F32)],
        compiler_params=_params(("arbitrary", "arbitrary")),
        name="in_proj",
    )(x, mod_l, gain, w_in_bf16, cos, sin_signed)


def _attn_kernel(q_ref, k_ref, v_ref, o_ref, lse_ref, bias_ref, *, length, n_side, n_res, q_unroll):
    rb = pl.program_id(1)
    key_w = min(length, Q_BLOCK + 2 * n_side)
    n_qb = length // Q_BLOCK

    @pl.when((pl.program_id(0) == 0) & (rb == 0))
    def _():
        i = lax.broadcasted_iota(_I32, (2 * Q_BLOCK, key_w), 0) % Q_BLOCK
        j = lax.broadcasted_iota(_I32, (2 * Q_BLOCK, key_w), 1)
        for n in range(3):
            bias_ref[n] = jnp.where(jnp.abs(i - j + n * n_side) <= n_side, 0.0, -jnp.inf)

    @pl.when(rb == 0)
    def _():
        lse_ref[...] = jnp.zeros_like(lse_ref)

    lane = lax.broadcasted_iota(_I32, (1, LANES), 1)
    head_mask = [(lane < HEAD_DIM).astype(_BF16), (lane >= HEAD_DIM).astype(_BF16)]
    lse_lane = lax.broadcasted_iota(_I32, (1, lse_ref.shape[2]), 1)

    def trip(it, carry):
        for u in range(q_unroll):
            q0 = pl.multiple_of((it * q_unroll + u) * Q_BLOCK, Q_BLOCK)
            k0 = pl.multiple_of(jnp.clip(q0 - n_side, 0, length - key_w), n_side)
            bias = bias_ref[(q0 - k0) // n_side]
            lse_rows = lse_ref[0, pl.ds(q0, Q_BLOCK), :]
            for rr in range(n_res):
                for pair in range(GROUP_W // LANES):
                    lo = rr * GROUP_W + pair * LANES
                    cols = slice(lo, lo + LANES)
                    q2 = q_ref[0, pl.ds(q0, Q_BLOCK), cols]
                    k2 = k_ref[0, pl.ds(k0, key_w), cols]
                    v2 = v_ref[0, pl.ds(k0, key_w), cols]
                    qs = jnp.concatenate([q2 * head_mask[0], q2 * head_mask[1]], axis=0)
                    s = lax.dot_general(qs, k2, (((1,), (1,)), ((), ())), preferred_element_type=_F32) + bias
                    m = jnp.max(s, axis=1, keepdims=True)
                    p = jnp.exp(s - m)
                    l = jnp.sum(p, axis=1, keepdims=True)
                    o = jnp.dot(p.astype(_BF16), v2, preferred_element_type=_F32) * (1.0 / l)
                    lse = m + jnp.log(l)
                    for hh in range(2):
                        col = (rb * n_res + rr) * HEADS_PER_GROUP + pair * 2 + hh
                        lse_rows = jnp.where(lse_lane == col, lse[hh * Q_BLOCK:(hh + 1) * Q_BLOCK], lse_rows)
                    o_ref[0, pl.ds(q0, Q_BLOCK), cols] = jnp.where(lane < HEAD_DIM, o[:Q_BLOCK], o[Q_BLOCK:]).astype(_BF16)
            lse_ref[0, pl.ds(q0, Q_BLOCK), :] = lse_rows
        return carry

    lax.fori_loop(0, n_qb // q_unroll, trip, 0)


ATTN_BLOCKS_PER_TRIP = 4


def _attention(q, k, v, window, dilation):
    B, L, _ = q.shape
    d = dilation
    n_side = window // (2 * d)
    n_res = min(d, ATTN_BLOCKS_PER_TRIP)
    q_unroll = min(ATTN_BLOCKS_PER_TRIP // n_res, L // Q_BLOCK)
    key_w = min(L, Q_BLOCK + 2 * n_side)
    spec = pl.BlockSpec((1, L, n_res * GROUP_W), lambda b, r: (b, 0, r))
    return pl.pallas_call(
        functools.partial(_attn_kernel, length=L, n_side=n_side, n_res=n_res, q_unroll=q_unroll),
        grid=(B, d // n_res),
        in_specs=[spec, spec, spec],
        out_specs=[spec, pl.BlockSpec((1, L, d * HEADS_PER_GROUP), lambda b, r: (b, 0, 0))],
        out_shape=[jax.ShapeDtypeStruct((B, L, d * GROUP_W), _BF16),
                   jax.ShapeDtypeStruct((B, L, d * HEADS_PER_GROUP), _F32)],
        scratch_shapes=[pltpu.VMEM((3, 2 * Q_BLOCK, key_w), _F32)],
        compiler_params=_params(("arbitrary", "arbitrary")),
        name=f"attention_d{d}",
    )(q, k, v)


def _route(sel):
    s = [sel[e:e + 1, :] for e in range(N_EXPERTS)]
    group_score = []
    for g in range(N_EXPERT_GROUPS):
        c0, c1, c2, c3 = s[EXPERTS_PER_GROUP * g:EXPERTS_PER_GROUP * (g + 1)]
        m1, n1 = jnp.maximum(c0, c1), jnp.minimum(c0, c1)
        m2, n2 = jnp.maximum(c2, c3), jnp.minimum(c2, c3)
        group_score.append(jnp.maximum(m1, m2) + jnp.maximum(jnp.minimum(m1, m2), jnp.maximum(n1, n2)))
    best, top_g = group_score[0], jnp.zeros(group_score[0].shape, _I32)
    for g in range(1, N_EXPERT_GROUPS):
        better = group_score[g] > best
        best = jnp.where(better, group_score[g], best)
        top_g = jnp.where(better, g, top_g)
    cs = []
    for j in range(EXPERTS_PER_GROUP):
        c = s[j]
        for g in range(1, N_EXPERT_GROUPS):
            c = jnp.where(top_g == g, s[EXPERTS_PER_GROUP * g + j], c)
        cs.append(c)
    s1, i1 = cs[0], jnp.zeros_like(top_g)
    for j in range(1, EXPERTS_PER_GROUP):
        better = cs[j] > s1
        s1 = jnp.where(better, cs[j], s1)
        i1 = jnp.where(better, j, i1)
    i2 = jnp.where(i1 == 0, 1, 0)
    s2 = jnp.where(i1 == 0, cs[1], cs[0])
    for j in range(1, EXPERTS_PER_GROUP):
        better = (cs[j] > s2) & (i1 != j)
        s2 = jnp.where(better, cs[j], s2)
        i2 = jnp.where(better, j, i2)
    lo, hi = jnp.minimum(i1, i2), jnp.maximum(i1, i2)
    pair = jnp.where(lo == 0, 0, jnp.where(lo == 1, 3, 5)) + (hi - lo - 1)
    return top_g * PAIRS_PER_GROUP + pair


def _mix_out_kernel(x_ref, o0_ref, o1_ref, o2_ref, l0_ref, l1_ref, l2_ref, u_ref, up_ref, un_ref,
                    mod_ref, wpool_ref, pscale_ref, wout_ref, gain_ref, wr_hi_ref, wr_lo_ref, rbias_ref,
                    x1_ref, h2_ref, cls_ref, ext_ref, lvl_a_ref, lvl_b_ref, nat_o1a_ref, nat_o1b_ref,
                    nat_o2a_ref, nat_o2b_ref, nat_l1_ref, nat_l2_ref, *, seq_len):
    nat_o_refs = ((nat_o1a_ref, nat_o1b_ref), (nat_o2a_ref, nat_o2b_ref))
    nat_l_refs = (nat_l1_ref, nat_l2_ref)
    i = pl.program_id(1)
    n_i = pl.num_programs(1)
    tm = x_ref.shape[1]
    mod = mod_ref[0]

    outs, lses = [], []
    for g, (o_ref, l_ref) in enumerate(((o0_ref, l0_ref), (o1_ref, l1_ref), (o2_ref, l2_ref))):
        dilation = DILATED_GROUPS[g][1]
        if dilation == 1:
            outs.append(o_ref[0].astype(_F32))
            lses.append(l_ref[0])
            continue
        rows = tm // dilation
        lv = l_ref[0]
        lv = jnp.concatenate([lv, jnp.zeros((rows, LANES - lv.shape[1]), _F32)], axis=1)
        halves = nat_o_refs[g - 1]
        for r in range(dilation):
            for c, half_ref in enumerate(halves):
                lo = r * GROUP_W + c * LANES
                half_ref[pl.ds(r, rows, stride=dilation), :] = o_ref[0, :, lo:lo + LANES].astype(_F32)
            shifted = lv if r == 0 else pltpu.roll(lv, LANES - r * HEADS_PER_GROUP, axis=1)
            nat_l_refs[g - 1][pl.ds(r, rows, stride=dilation), :] = shifted
        outs.append(jnp.concatenate([half_ref[...] for half_ref in halves], axis=1))
        lses.append(nat_l_refs[g - 1][:, 0:HEADS_PER_GROUP])

    top = jnp.maximum(jnp.maximum(lses[0], lses[1]), lses[2])
    es = [jnp.exp(t - top) for t in lses]
    den = es[0] + es[1] + es[2]
    pieces = []
    for g in range(N_GROUPS):
        alpha = es[g] / den
        wide = jnp.concatenate([jnp.broadcast_to(alpha[:, h:h + 1], (tm, HEAD_DIM))
                                for h in range(HEADS_PER_GROUP)], axis=1)
        pieces.append((outs[g] * wide).astype(_BF16))

    pad = 2 * SUBLANES
    zeros_pad = jnp.zeros((SUBLANES, POOL_W), _F32)
    for ref in (ext_ref, lvl_a_ref, lvl_b_ref):
        ref[0:SUBLANES, :] = zeros_pad
        ref[tm + pad + SUBLANES:tm + 2 * pad, :] = zeros_pad
    u = u_ref[0]
    ext_ref[SUBLANES:pad, :] = jnp.where(i > 0, up_ref[0], 0.0)
    ext_ref[pad:pad + tm, :] = u
    ext_ref[pad + tm:pad + tm + SUBLANES, :] = jnp.where(i < n_i - 1, un_ref[0], 0.0)
    span = tm + 2 * SUBLANES
    win = lambda ref, off: ref[SUBLANES + off:SUBLANES + off + span, :]
    lvl_a_ref[SUBLANES:SUBLANES + span, :] = win(ext_ref, -1) + win(ext_ref, 0)
    s2 = lvl_a_ref[pad:pad + tm, :]
    lvl_b_ref[SUBLANES:SUBLANES + span, :] = win(lvl_a_ref, -1) + win(lvl_a_ref, 1)
    s4 = lvl_b_ref[pad:pad + tm, :]
    lvl_a_ref[SUBLANES:SUBLANES + span, :] = win(lvl_b_ref, -2) + win(lvl_b_ref, 2)
    s8 = lvl_a_ref[pad:pad + tm, :]
    s16 = lvl_a_ref[pad - 4:pad - 4 + tm, :] + lvl_a_ref[pad + 4:pad + 4 + tm, :]
    lane = lax.broadcasted_iota(_I32, (1, POOL_W), 1)
    wgroup = lane // POOL_GROUP_W
    wsum = jnp.where(wgroup == 0, s2, jnp.where(wgroup == 1, s4, jnp.where(wgroup == 2, s8, s16)))
    half_w = jnp.where(wgroup == 0, 1, jnp.where(wgroup == 1, 2, jnp.where(wgroup == 2, 4, 8)))
    pos = i * tm + lax.broadcasted_iota(_I32, (tm, 1), 0)
    cnt = jnp.minimum(pos + half_w, seq_len) - jnp.maximum(pos - half_w, 0)
    z = wsum / cnt.astype(_F32) - u
    pool = jnp.dot(z.astype(_BF16), wpool_ref[...], preferred_element_type=_F32) * pscale_ref[...]
    pieces.append(pool.astype(_BF16))

    mix = jnp.dot(jnp.concatenate(pieces, axis=1), wout_ref[...], preferred_element_type=_F32)
    x1 = x_ref[0] + mod[2:3] * mix
    x1_ref[0] = x1

    h2 = _rmsnorm_mod(x1, gain_ref[...], mod[4:5], mod[3:4])
    h_hi = h2.astype(_BF16)
    h2_ref[0] = h_hi
    h_lo = (h2 - h_hi.astype(_F32)).astype(_BF16)
    nt = (((1,), (1,)), ((), ()))
    logits = (lax.dot_general(wr_hi_ref[...], h_hi, nt, preferred_element_type=_F32)
              + lax.dot_general(wr_lo_ref[...], h_hi, nt, preferred_element_type=_F32)
              + lax.dot_general(wr_hi_ref[...], h_lo, nt, preferred_element_type=_F32))
    aff = 1.0 / (1.0 + jnp.exp(-logits))
    cls_ref[0] = _route(aff + rbias_ref[...])


def _mix_out(x, attn, lses, u, mod_l, wpool_bd, pool_scale, w_out_bf16, gain, wr_hi, wr_lo, rbias, batch_offset):
    B, S, _ = x.shape
    tm = TOKEN_TILE
    n_i = S // tm
    halo_blocks = tm // SUBLANES
    tok = lambda w: pl.BlockSpec((1, tm, w), lambda b, i: (b, i, 0))
    const = lambda shape: pl.BlockSpec(shape, lambda b, i: tuple(0 for _ in shape))
    rows = tm + 4 * SUBLANES
    return pl.pallas_call(
        functools.partial(_mix_out_kernel, seq_len=S),
        grid=(B, n_i),
        in_specs=[
            tok(D_MODEL),
            *[pl.BlockSpec((1, tm // d, d * GROUP_W), lambda b, i: (b, i, 0)) for _, d in DILATED_GROUPS],
            *[pl.BlockSpec((1, tm // d, d * HEADS_PER_GROUP), lambda b, i: (b, i, 0)) for _, d in DILATED_GROUPS],
            tok(POOL_W),
            pl.BlockSpec((1, SUBLANES, POOL_W), lambda b, i: (b, jnp.maximum(i * halo_blocks - 1, 0), 0)),
            pl.BlockSpec((1, SUBLANES, POOL_W),
                         lambda b, i: (b, jnp.minimum((i + 1) * halo_blocks, S // SUBLANES - 1), 0)),
            pl.BlockSpec((1, 6, D_MODEL), lambda b, i: (b + batch_offset, 0, 0)),
            const((POOL_W, POOL_W)), const((1, POOL_W)), const((D_MODEL, D_MODEL)), const((1, D_MODEL)),
            const((N_EXPERTS, D_MODEL)), const((N_EXPERTS, D_MODEL)), const((N_EXPERTS, 1)),
        ],
        out_specs=[tok(D_MODEL), tok(D_MODEL), pl.BlockSpec((1, 1, tm), lambda b, i: (b, 0, i))],
        out_shape=[jax.ShapeDtypeStruct((B, S, D_MODEL), _F32),
                   jax.ShapeDtypeStruct((B, S, D_MODEL), _BF16),
                   jax.ShapeDtypeStruct((B, 1, S), _I32)],
        scratch_shapes=[pltpu.VMEM((rows, POOL_W), _F32)] * 3 + [pltpu.VMEM((tm, LANES), _F32)] * 6,
        compiler_params=_params(("arbitrary", "arbitrary")),
        name="mix_out",
    )(x, *attn, *lses, u, u, u, mod_l, wpool_bd, pool_scale, w_out_bf16, gain, wr_hi, wr_lo, rbias)


TBL_EXPERT_A, TBL_EXPERT_B, TBL_TILE, TBL_USED, TBL_LAST_TILE_ROW = range(5)


def _plan_kernel(cls_ref, slot_ref, tbl_ref, cnt_ref, run_ref, base_ref, *, tile):
    phase, b, j = pl.program_id(0), pl.program_id(1), pl.program_id(2)
    first = (b == 0) & (j == 0)
    chunk = cls_ref.shape[2]
    n_tbl = tbl_ref.shape[1]
    onehot = lax.broadcasted_iota(_I32, (CLASS_ROWS, chunk), 0) == cls_ref[0]
    hits = jnp.sum(onehot.astype(_F32), axis=1, keepdims=True)

    @pl.when((phase == 0) & first)
    def _():
        cnt_ref[...] = jnp.zeros_like(cnt_ref)

    @pl.when(phase == 0)
    def _():
        cnt_ref[...] += hits

    @pl.when((phase == 1) & first)
    def _():
        padded = jnp.ceil(cnt_ref[...] * (1.0 / tile)) * tile
        r_i = lax.broadcasted_iota(_I32, (CLASS_ROWS, CLASS_ROWS), 0)
        c_i = lax.broadcasted_iota(_I32, (CLASS_ROWS, CLASS_ROWS), 1)
        padded_row = jnp.sum(jnp.where(r_i == c_i, padded, 0.0), axis=0, keepdims=True)
        base_col = jnp.sum(jnp.where(c_i < r_i, padded_row, 0.0), axis=1, keepdims=True)
        base_row = jnp.sum(jnp.where(r_i < c_i, padded, 0.0), axis=0, keepdims=True)
        total = jnp.sum(padded_row, axis=1, keepdims=True)
        base_ref[...] = base_col
        run_ref[...] = jnp.zeros_like(run_ref)

        k_row = lax.broadcasted_iota(_I32, (1, n_tbl), 1).astype(_F32)
        start = k_row * tile
        inside = (base_col <= start) & (start < base_col + padded)
        cls_id = lax.broadcasted_iota(_I32, (CLASS_ROWS, n_tbl), 0).astype(_F32)
        tile_cls = jnp.sum(jnp.where(inside, cls_id, 0.0), axis=0, keepdims=True)
        n_used = total * (1.0 / tile)
        last = jnp.maximum(n_used - 1.0, 0.0)
        last_cls = jnp.sum(jnp.where(k_row == last, tile_cls, 0.0), axis=1, keepdims=True)
        used = k_row < n_used
        tile_cls = jnp.where(used, tile_cls, last_cls)
        grp = sum(jnp.where(tile_cls >= PAIRS_PER_GROUP * g, 1.0, 0.0) for g in range(1, N_EXPERT_GROUPS))
        pair = tile_cls - grp * PAIRS_PER_GROUP
        a = jnp.where(pair < 3, 0.0, jnp.where(pair < 5, 1.0, 2.0))
        bb = jnp.where(pair < 3, pair + 1.0, jnp.where(pair < 5, pair - 1.0, 3.0))
        last_tile_row = jnp.where(padded_row > 0, base_row + padded_row - tile, -1.0)
        last_tile_row = jnp.concatenate(
            [last_tile_row, jnp.full((1, n_tbl - CLASS_ROWS), -1.0, _F32)], axis=1)
        zero = jnp.zeros((1, n_tbl), _F32)
        tbl_ref[...] = jnp.concatenate(
            [grp * EXPERTS_PER_GROUP + a, grp * EXPERTS_PER_GROUP + bb, jnp.where(used, k_row, last),
             jnp.where(used, 1.0, 0.0), last_tile_row, zero, zero, zero], axis=0).astype(_I32)

    @pl.when(phase == 1)
    def _():
        before = lax.broadcasted_iota(_I32, (chunk, chunk), 0) < lax.broadcasted_iota(_I32, (chunk, chunk), 1)
        earlier = jnp.dot(jnp.where(onehot, 1.0, 0.0).astype(_BF16), jnp.where(before, 1.0, 0.0).astype(_BF16),
                          preferred_element_type=_F32)
        dest = base_ref[...] + run_ref[...] + earlier
        slot_ref[0] = jnp.sum(jnp.where(onehot, dest, 0.0), axis=0, keepdims=True).astype(_I32)
        run_ref[...] += hits


def _moe_plan(cls, tile):
    B, _, S = cls.shape
    chunk = PLAN_CHUNK
    n_tiles = (B * S) // tile + N_CLASSES
    n_tbl = -(-n_tiles // LANES) * LANES
    slot, tbl = pl.pallas_call(
        functools.partial(_plan_kernel, tile=tile),
        grid=(2, B, S // chunk),
        in_specs=[pl.BlockSpec((1, 1, chunk), lambda p, b, j: (b, 0, j))],
        out_specs=[pl.BlockSpec((1, 1, chunk), lambda p, b, j: (b * p, 0, j * p)),
                   pl.BlockSpec((SUBLANES, n_tbl), lambda p, b, j: (0, 0))],
        out_shape=[jax.ShapeDtypeStruct((B, 1, S), _I32), jax.ShapeDtypeStruct((SUBLANES, n_tbl), _I32)],
        scratch_shapes=[pltpu.VMEM((CLASS_ROWS, 1), _F32)] * 3,
        compiler_params=_params(("arbitrary", "arbitrary", "arbitrary")),
        name="moe_plan",
    )(cls)
    return slot, tbl, n_tiles


def _row_copy(src, src_row, dst, dst_row, sem):
    return pltpu.make_async_copy(src.at[pl.ds(pl.multiple_of(src_row * ROW_CHUNKS, ROW_CHUNKS), ROW_CHUNKS), :],
                                 dst.at[pl.ds(pl.multiple_of(dst_row * ROW_CHUNKS, ROW_CHUNKS), ROW_CHUNKS), :],
                                 sem)


def _dispatch_kernel(tbl_ref, slot_ref, h_ref, hs_ref, stage_ref, zero_ref, sem, zsem, *, tile):
    first = (pl.program_id(0) == 0) & (pl.program_id(1) == 0)
    tm = h_ref.shape[1]
    tile_rows = tile * ROW_CHUNKS

    @pl.when(first)
    def _():
        zero_ref[...] = jnp.zeros_like(zero_ref)
        n_tiles = hs_ref.shape[0] // tile_rows
        for wait in (False, True):
            for c in range(N_CLASSES):
                row = tbl_ref[TBL_LAST_TILE_ROW, c]

                @pl.when(row >= 0)
                def _():
                    at = pl.multiple_of(row * ROW_CHUNKS, ROW_CHUNKS)
                    cp = pltpu.make_async_copy(zero_ref, hs_ref.at[pl.ds(at, tile_rows), :], zsem)
                    cp.wait() if wait else cp.start()

            for k in range(n_tiles - N_CLASSES, n_tiles):
                @pl.when(tbl_ref[TBL_USED, k] == 0)
                def _():
                    cp = pltpu.make_async_copy(zero_ref, hs_ref.at[pl.ds(k * tile_rows, tile_rows), :], zsem)
                    cp.wait() if wait else cp.start()

    h = h_ref[0]
    for c in range(ROW_CHUNKS):
        stage_ref[pl.ds(c, tm, stride=ROW_CHUNKS), :] = h[:, c * LANES:(c + 1) * LANES].astype(_F32)

    def issue(g, carry):
        for k in range(SUBLANES):
            r = g * SUBLANES + k
            _row_copy(stage_ref, r, hs_ref, slot_ref[0, 0, r], sem).start(priority=k % 2)
        return carry

    lax.fori_loop(0, tm // SUBLANES, issue, 0)
    pltpu.make_async_copy(stage_ref, hs_ref.at[pl.ds(0, tm * ROW_CHUNKS), :], sem).wait()


def _moe_dispatch(tbl, slot, h2, n_tiles, tile):
    B, S, _ = h2.shape
    tm = TOKEN_TILE
    return pl.pallas_call(
        functools.partial(_dispatch_kernel, tile=tile),
        grid_spec=pltpu.PrefetchScalarGridSpec(
            num_scalar_prefetch=1,
            grid=(B, S // tm),
            in_specs=[pl.BlockSpec((1, 1, tm), lambda b, i, tbl: (b, 0, i), memory_space=pltpu.SMEM),
                      pl.BlockSpec((1, tm, D_MODEL), lambda b, i, tbl: (b, i, 0))],
            out_specs=pl.BlockSpec(memory_space=pl.ANY),
            scratch_shapes=[pltpu.VMEM((tm * ROW_CHUNKS, LANES), _F32),
                            pltpu.VMEM((tile * ROW_CHUNKS, LANES), _F32),
                            pltpu.SemaphoreType.DMA(()), pltpu.SemaphoreType.DMA(())],
        ),
        out_shape=jax.ShapeDtypeStruct((n_tiles * tile * ROW_CHUNKS, LANES), _F32),
        compiler_params=_params(("arbitrary", "arbitrary")),
        name="moe_dispatch",
    )(tbl, slot, h2)


def _experts_kernel(tbl_ref, hs_ref, wra_hi_ref, wra_lo_ref, wrb_hi_ref, wrb_lo_ref,
                    wga_ref, wua_ref, wda_ref, wgb_ref, wub_ref, wdb_ref, ys_ref):
    k = pl.program_id(0)
    tile = hs_ref.shape[0] // ROW_CHUNKS

    @pl.when(tbl_ref[TBL_USED, k] == 0)
    def _():
        ys_ref[...] = jnp.zeros_like(ys_ref)

    @pl.when(tbl_ref[TBL_USED, k] == 1)
    def _():
        x = jnp.concatenate([hs_ref[pl.ds(c, tile, stride=ROW_CHUNKS), :].astype(_BF16)
                             for c in range(ROW_CHUNKS)], axis=1)
        row_id = lax.broadcasted_iota(_I32, (2 * SUBLANES, D_MODEL), 0)
        rows = jnp.zeros((2 * SUBLANES, D_MODEL), _F32)
        for n, ref in enumerate((wra_hi_ref, wra_lo_ref, wrb_hi_ref, wrb_lo_ref)):
            rows = jnp.where(row_id == n, ref[0], rows)
        lg = lax.dot_general(x, rows.astype(_BF16), (((1,), (1,)), ((), ())),
                             preferred_element_type=_F32)
        aff_a = 1.0 / (1.0 + jnp.exp(-(lg[:, 0:1] + lg[:, 1:2])))
        aff_b = 1.0 / (1.0 + jnp.exp(-(lg[:, 2:3] + lg[:, 3:4])))
        den = aff_a + aff_b

        def hidden(wg_ref, wu_ref, gate):
            a = jnp.dot(x, wg_ref[0], preferred_element_type=_F32)
            b = jnp.dot(x, wu_ref[0], preferred_element_type=_F32)
            return ((a / (1.0 + jnp.exp(-a))) * b * gate).astype(_BF16)

        y = (jnp.dot(hidden(wga_ref, wua_ref, aff_a / den), wda_ref[0], preferred_element_type=_F32)
             + jnp.dot(hidden(wgb_ref, wub_ref, aff_b / den), wdb_ref[0], preferred_element_type=_F32))
        for c in range(ROW_CHUNKS):
            ys_ref[pl.ds(c, tile, stride=ROW_CHUNKS), :] = y[:, c * LANES:(c + 1) * LANES]


def _moe_experts(tbl, hs, wr_hi, wr_lo, wg, wu, wd, n_tiles, tile):
    rows = tile * ROW_CHUNKS
    tile_spec = pl.BlockSpec((rows, LANES), lambda k, tbl: (tbl[TBL_TILE, k], 0))
    router = lambda row: pl.BlockSpec((1, 1, D_MODEL), lambda k, tbl: (tbl[row, k], 0, 0))
    up = lambda row: pl.BlockSpec((1, D_MODEL, D_EXPERT), lambda k, tbl: (tbl[row, k], 0, 0))
    down = lambda row: pl.BlockSpec((1, D_EXPERT, D_MODEL), lambda k, tbl: (tbl[row, k], 0, 0))
    a, b = TBL_EXPERT_A, TBL_EXPERT_B
    return pl.pallas_call(
        _experts_kernel,
        grid_spec=pltpu.PrefetchScalarGridSpec(
            num_scalar_prefetch=1,
            grid=(n_tiles,),
            in_specs=[tile_spec, router(a), router(a), router(b), router(b),
                      up(a), up(a), down(a), up(b), up(b), down(b)],
            out_specs=pl.BlockSpec((rows, LANES), lambda k, tbl: (k, 0)),
        ),
        out_shape=jax.ShapeDtypeStruct(hs.shape, _F32),
        compiler_params=_params(("arbitrary",)),
        name="moe_experts",
    )(tbl, hs, wr_hi, wr_lo, wr_hi, wr_lo, wg, wu, wd, wg, wu, wd)


def _combine_kernel(slot_ref, slot_next_ref, x1_ref, mod_ref, ys_ref, *refs, project):
    if project:
        (modn_ref, gain_ref, w_ref, cos_ref, sin_ref, x2_ref, *out_refs) = refs[:16]
        buf_refs, sems, stage_ref = refs[16:18], refs[18:20], refs[20]
    else:
        gfin_ref, x2_ref = refs[:2]
        buf_refs, sems = refs[2:4], refs[4:6]
    tm = x1_ref.shape[1]
    half = tm // 2
    step = pl.program_id(0) * pl.num_programs(1) + pl.program_id(1)
    last = pl.num_programs(0) * pl.num_programs(1) - 1
    gate = mod_ref[0][5:6]

    def gather(sref, row0, n):
        def issue(g, carry):
            for k in range(SUBLANES):
                r = g * SUBLANES + k
                _row_copy(ys_ref, sref[0, 0, row0 + r], buf_refs[n], r, sems[n]).start(priority=k % 2)
            return carry
        lax.fori_loop(0, half // SUBLANES, issue, 0)

    def residual(n):
        pltpu.make_async_copy(ys_ref.at[pl.ds(0, half * ROW_CHUNKS), :], buf_refs[n], sems[n]).wait()
        rows = slice(n * half, (n + 1) * half)
        for c in range(ROW_CHUNKS):
            cols = slice(c * LANES, (c + 1) * LANES)
            x2_ref[0, rows, cols] = x1_ref[0, rows, cols] + gate[:, cols] * buf_refs[n][pl.ds(c, half, stride=ROW_CHUNKS), :]

    def tail(n):
        rows = slice(n * half, (n + 1) * half)
        x2 = x2_ref[0, rows, :]
        if project:
            _project(x2, n * half, modn_ref[0], gain_ref, w_ref, cos_ref, sin_ref, out_refs[0:3], out_refs[3:6],
                     out_refs[6:9], out_refs[9], stage_ref)
        else:
            x2_ref[0, rows, :] = x2 * lax.rsqrt(jnp.mean(x2 * x2, axis=-1, keepdims=True) + RMS_EPS) * gfin_ref[...]

    @pl.when(step == 0)
    def _():
        gather(slot_ref, 0, 0)

    gather(slot_ref, half, 1)
    residual(0)

    @pl.when(step < last)
    def _():
        gather(slot_next_ref, 0, 0)

    tail(0)
    residual(1)
    tail(1)


def _moe_combine(slot, x1, mod_prev, ys, batch_offset, tail_args):
    B, S, _ = x1.shape
    tm = TOKEN_TILE
    half = tm // 2
    n_i = S // tm
    project = len(tail_args) > 1

    def next_step(b, i):
        n = jnp.minimum(b * n_i + i + 1, B * n_i - 1)
        return n // n_i, 0, n % n_i

    tok = pl.BlockSpec((1, tm, D_MODEL), lambda b, i: (b, i, 0))
    mod_spec = pl.BlockSpec((1, 6, D_MODEL), lambda b, i: (b + batch_offset, 0, 0))
    in_specs = [pl.BlockSpec((1, 1, tm), lambda b, i: (b, 0, i), memory_space=pltpu.SMEM),
                pl.BlockSpec((1, 1, tm), next_step, memory_space=pltpu.SMEM),
                tok, mod_spec, pl.BlockSpec(memory_space=pl.ANY)]
    out_specs, out_shape = [tok], [jax.ShapeDtypeStruct((B, S, D_MODEL), _F32)]
    scratch = [pltpu.VMEM((half * ROW_CHUNKS, LANES), _F32)] * 2 + [pltpu.SemaphoreType.DMA(())] * 2
    if project:
        in_specs += [mod_spec,
                     pl.BlockSpec((1, D_MODEL), lambda b, i: (0, 0)),
                     pl.BlockSpec((D_MODEL, IN_W), lambda b, i: (0, 0)),
                     pl.BlockSpec((tm, LANES), lambda b, i: (i, 0)),
                     pl.BlockSpec((tm, LANES), lambda b, i: (i, 0))]
        out_specs += [pl.BlockSpec((1, tm // d, d * GROUP_W), lambda b, i: (b, i, 0)) for _, d in DILATED_GROUPS] * 3
        out_specs += [pl.BlockSpec((1, tm, POOL_W), lambda b, i: (b, i, 0))]
        out_shape += [jax.ShapeDtypeStruct((B, S // d, d * GROUP_W), _BF16) for _, d in DILATED_GROUPS] * 3
        out_shape += [jax.ShapeDtypeStruct((B, S, POOL_W), _F32)]
        scratch += [pltpu.VMEM((half, LANES), _F32)]
    else:
        in_specs += [pl.BlockSpec((1, D_MODEL), lambda b, i: (0, 0))]
    return pl.pallas_call(
        functools.partial(_combine_kernel, project=project),
        grid=(B, n_i),
        in_specs=in_specs,
        out_specs=out_specs,
        out_shape=out_shape,
        scratch_shapes=scratch,
        compiler_params=_params(("arbitrary", "arbitrary")),
        name="moe_combine_in_proj" if project else "moe_combine_norm",
    )(slot, slot, x1, mod_prev, ys, *tail_args)


def _rope_tables(S):
    inv = 1.0 / (ROPE_THETA ** (jnp.arange(0, HEAD_DIM, 2, dtype=_F32) / HEAD_DIM))
    ang = jnp.arange(S, dtype=_F32)[:, None] * inv[None, :]
    cos, sin = jnp.cos(ang), jnp.sin(ang)
    reps = LANES // HEAD_DIM
    return (jnp.tile(jnp.concatenate([cos, cos], axis=1), (1, reps)),
            jnp.tile(jnp.concatenate([-sin, sin], axis=1), (1, reps)))


def _trunk(x, batch_offset, mod, w):
    cos, sin_signed = _rope_tables(x.shape[1])
    outs = _in_proj(x, mod[0], w["norm_mix"][0], w["w_in"][0], cos, sin_signed, batch_offset)
    for l in range(DEPTH):
        mod_l = mod[l]
        q, k, v, u = outs[0:3], outs[3:6], outs[6:9], outs[9]
        attn, lses = [], []
        for g, (window, dilation) in enumerate(DILATED_GROUPS):
            o, lse = _attention(q[g], k[g], v[g], window, dilation)
            attn.append(o)
            lses.append(lse)
        x1, h2, cls = _mix_out(x, attn, lses, u, mod_l, w["wpool_bd"][l], w["pool_scale"][l], w["w_out"][l],
                               w["norm_ffn"][l], w["wr_hi"], w["wr_lo"], w["rbias"], batch_offset)
        slot, tbl, n_tiles = _moe_plan(cls, MOE_TILE)
        hs = _moe_dispatch(tbl, slot, h2, n_tiles, MOE_TILE)
        ys = _moe_experts(tbl, hs, w["wr_hi3"], w["wr_lo3"], w["w_gate"][l], w["w_up"][l], w["w_down"][l],
                          n_tiles, MOE_TILE)
        if l + 1 < DEPTH:
            tail = (mod[l + 1], w["norm_mix"][l + 1], w["w_in"][l + 1], cos, sin_signed)
        else:
            tail = (w["norm_final"],)
        x, *outs = _moe_combine(slot, x1, mod_l, ys, batch_offset, tail)
    return x


def kernel(x_prompt, x_sample, c_prompt, c_sample, norm_mix, w_mod, b_mod, w_in, w_pool, pool_scale, w_out,
           norm_ffn, w_router, router_bias, w_gate, w_up, w_down, norm_final):
    n_prompt = x_prompt.shape[0]
    c = jnp.concatenate([c_prompt, c_sample], axis=0)
    mod = _modulation(c, w_mod, b_mod).reshape(DEPTH, c.shape[0], 6, D_MODEL)
    n_pool = len(POOL_WINDOWS)
    eye = jnp.eye(n_pool, dtype=w_pool.dtype)
    wpool_bd = (w_pool[:, :, :, None, :] * eye[None, :, None, :, None]).reshape(DEPTH, POOL_W, POOL_W)
    wr_t = w_router.T.astype(_F32)
    wr_hi = wr_t.astype(_BF16)
    wr_lo = (wr_t - wr_hi.astype(_F32)).astype(_BF16)
    w = {
        "norm_mix": norm_mix.reshape(DEPTH, 1, D_MODEL),
        "norm_ffn": norm_ffn.reshape(DEPTH, 1, D_MODEL),
        "norm_final": norm_final.reshape(1, D_MODEL),
        "w_in": w_in.astype(_BF16),
        "w_out": w_out.astype(_BF16),
        "wpool_bd": wpool_bd.astype(_BF16),
        "pool_scale": pool_scale.reshape(DEPTH, 1, POOL_W),
        "wr_hi": wr_hi,
        "wr_lo": wr_lo,
        "wr_hi3": wr_hi.astype(_F32).reshape(N_EXPERTS, 1, D_MODEL),
        "wr_lo3": wr_lo.astype(_F32).reshape(N_EXPERTS, 1, D_MODEL),
        "rbias": router_bias.astype(_F32).reshape(N_EXPERTS, 1),
        "w_gate": w_gate.astype(_BF16),
        "w_up": w_up.astype(_BF16),
        "w_down": w_down.astype(_BF16),
    }
    y_prompt = _trunk(x_prompt, 0, mod, w)
    y_sample = _trunk(x_sample, n_prompt, mod, w)
    return (y_prompt, y_sample)
```

```python
import functools

import jax
import jax.numpy as jnp
from jax import lax
from jax.experimental import pallas as pl
from jax.experimental.pallas import tpu as pltpu

D_MODEL = 1024
DEPTH = 4
HEAD_DIM = 64
HALF_HEAD = HEAD_DIM // 2
DILATED_GROUPS = ((128, 1), (512, 4), (2048, 16))
N_GROUPS = len(DILATED_GROUPS)
HEADS_PER_GROUP = 4
GROUP_W = HEADS_PER_GROUP * HEAD_DIM
ATT_W = N_GROUPS * GROUP_W
POOL_WINDOWS = (2, 4, 8, 16)
POOL_GROUP_W = 64
POOL_W = POOL_GROUP_W * len(POOL_WINDOWS)
IN_W = 3 * ATT_W + POOL_W
ROPE_THETA = 10000.0
RMS_EPS = 1e-6
N_EXPERTS = 16
N_EXPERT_GROUPS = 4
EXPERTS_PER_GROUP = N_EXPERTS // N_EXPERT_GROUPS
PAIRS_PER_GROUP = EXPERTS_PER_GROUP * (EXPERTS_PER_GROUP - 1) // 2
N_CLASSES = N_EXPERT_GROUPS * PAIRS_PER_GROUP
D_EXPERT = 512

LANES = 128
SUBLANES = 8
Q_BLOCK = 128
VMEM_LIMIT = 48 * 1024 * 1024

TOKEN_TILE = 512
MOE_TILE = 256
MIX_CHUNKS = 1
PLAN_CHUNK = 512
CLASS_ROWS = 32
ROW_CHUNKS = D_MODEL // LANES

_BF16 = jnp.bfloat16
_F32 = jnp.float32
_I32 = jnp.int32


def _params(semantics):
    return pltpu.CompilerParams(dimension_semantics=semantics, vmem_limit_bytes=VMEM_LIMIT)


def _mod_kernel(c_ref, w_ref, b_ref, o_ref):
    c = c_ref[...]
    sc = c / (1.0 + jnp.exp(-c))
    o_ref[0] = jnp.dot(sc, w_ref[0], preferred_element_type=_F32,
                       precision=lax.Precision.HIGHEST) + b_ref[0]


def _modulation(c, w_mod, b_mod):
    nb = c.shape[0]
    col = D_MODEL
    n_col = w_mod.shape[2] // col
    return pl.pallas_call(
        _mod_kernel,
        grid=(DEPTH, n_col),
        in_specs=[
            pl.BlockSpec((nb, D_MODEL), lambda l, j: (0, 0)),
            pl.BlockSpec((1, D_MODEL, col), lambda l, j: (l, 0, j)),
            pl.BlockSpec((1, 1, col), lambda l, j: (l, 0, j)),
        ],
        out_specs=pl.BlockSpec((1, nb, col), lambda l, j: (l, 0, j)),
        out_shape=jax.ShapeDtypeStruct((DEPTH, nb, w_mod.shape[2]), _F32),
        compiler_params=_params(("arbitrary", "arbitrary")),
        name="modulation",
    )(c, w_mod, b_mod.reshape(DEPTH, 1, -1))


def _rmsnorm_mod(x, gain, scale, shift):
    y = x * lax.rsqrt(jnp.mean(x * x, axis=-1, keepdims=True) + RMS_EPS)
    return (y * gain) * (1.0 + scale) + shift


def _rope_chunk(t, cos, sin_signed, first_half):
    fwd = pltpu.roll(t, HALF_HEAD, axis=1)
    bwd = pltpu.roll(t, LANES - HALF_HEAD, axis=1)
    return t * cos + jnp.where(first_half, bwd, fwd) * sin_signed


def _store_by_residue(out_ref, chunk, value, dilation, stage_ref, row0):
    rows = value.shape[0] // dilation
    at = slice(row0 // dilation, row0 // dilation + rows)
    if dilation == 1:
        out_ref[0, at, chunk * LANES:(chunk + 1) * LANES] = value.astype(_BF16)
        return
    stage_ref[...] = value
    for r in range(dilation):
        lo = r * GROUP_W + chunk * LANES
        out_ref[0, at, lo:lo + LANES] = stage_ref[pl.ds(r, rows, stride=dilation), :].astype(_BF16)


def _in_proj_kernel(x_ref, mod_ref, gain_ref, w_ref, cos_ref, sin_ref, *refs):
    _project(x_ref[0], 0, mod_ref[0], gain_ref, w_ref, cos_ref, sin_ref, refs[0:3], refs[3:6], refs[6:9], refs[9],
             refs[10])


def _project(x, row0, mod, gain_ref, w_ref, cos_ref, sin_ref, q_refs, k_refs, v_refs, u_ref, stage_ref):
    n = x.shape[0]
    h = _rmsnorm_mod(x, gain_ref[...], mod[1:2], mod[0:1])
    proj = jnp.dot(h.astype(_BF16), w_ref[...], preferred_element_type=_F32)
    cos = cos_ref[row0:row0 + n, :]
    sin_signed = sin_ref[row0:row0 + n, :]
    lane = lax.broadcasted_iota(_I32, (1, LANES), 1)
    first_half = (lane % HEAD_DIM) < HALF_HEAD
    q_scale = HEAD_DIM ** -0.5
    for g, (_, dilation) in enumerate(DILATED_GROUPS):
        for c in range(GROUP_W // LANES):
            lo = g * GROUP_W + c * LANES
            q = _rope_chunk(proj[:, lo:lo + LANES], cos, sin_signed, first_half)
            _store_by_residue(q_refs[g], c, q * q_scale, dilation, stage_ref, row0)
            k = _rope_chunk(proj[:, ATT_W + lo:ATT_W + lo + LANES], cos, sin_signed, first_half)
            _store_by_residue(k_refs[g], c, k, dilation, stage_ref, row0)
            _store_by_residue(v_refs[g], c, proj[:, 2 * ATT_W + lo:2 * ATT_W + lo + LANES], dilation, stage_ref,
                              row0)
    u_ref[0, row0:row0 + n, :] = proj[:, 3 * ATT_W:]


def _in_proj(x, mod_l, gain, w_in_bf16, cos, sin_signed, batch_offset):
    B, S, _ = x.shape
    tm = TOKEN_TILE
    grp = [jax.ShapeDtypeStruct((B, S // d, d * GROUP_W), _BF16) for _, d in DILATED_GROUPS]
    grp_spec = [pl.BlockSpec((1, tm // d, d * GROUP_W), lambda b, i: (b, i, 0)) for _, d in DILATED_GROUPS]
    return pl.pallas_call(
        _in_proj_kernel,
        grid=(B, S // tm),
        in_specs=[
            pl.BlockSpec((1, tm, D_MODEL), lambda b, i: (b, i, 0)),
            pl.BlockSpec((1, 6, D_MODEL), lambda b, i: (b + batch_offset, 0, 0)),
            pl.BlockSpec((1, D_MODEL), lambda b, i: (0, 0)),
            pl.BlockSpec((D_MODEL, IN_W), lambda b, i: (0, 0)),
            pl.BlockSpec((tm, LANES), lambda b, i: (i, 0)),
            pl.BlockSpec((tm, LANES), lambda b, i: (i, 0)),
        ],
        out_specs=grp_spec * 3 + [pl.BlockSpec((1, tm, POOL_W), lambda b, i: (b, i, 0))],
        out_shape=grp * 3 + [jax.ShapeDtypeStruct((B, S, POOL_W), _F32)],
        scratch_shapes=[pltpu.VMEM((tm, LANES), _F32)],
        compiler_params=_params(("arbitrary", "arbitrary")),
        name="in_proj",
    )(x, mod_l, gain, w_in_bf16, cos, sin_signed)


def _attn_kernel(q_ref, k_ref, v_ref, o_ref, lse_ref, bias_ref, *, length, n_side, n_res, q_unroll):
    rb = pl.program_id(1)
    key_w = min(length, Q_BLOCK + 2 * n_side)
    n_qb = length // Q_BLOCK

    @pl.when((pl.program_id(0) == 0) & (rb == 0))
    def _():
        i = lax.broadcasted_iota(_I32, (2 * Q_BLOCK, key_w), 0) % Q_BLOCK
        j = lax.broadcasted_iota(_I32, (2 * Q_BLOCK, key_w), 1)
        for n in range(3):
            bias_ref[n] = jnp.where(jnp.abs(i - j + n * n_side) <= n_side, 0.0, -jnp.inf)

    @pl.when(rb == 0)
    def _():
        lse_ref[...] = jnp.zeros_like(lse_ref)

    lane = lax.broadcasted_iota(_I32, (1, LANES), 1)
    head_mask = [(lane < HEAD_DIM).astype(_BF16), (lane >= HEAD_DIM).astype(_BF16)]
    lse_lane = lax.broadcasted_iota(_I32, (1, lse_ref.shape[2]), 1)

    def trip(it, carry):
        for u in range(q_unroll):
            q0 = pl.multiple_of((it * q_unroll + u) * Q_BLOCK, Q_BLOCK)
            k0 = pl.multiple_of(jnp.clip(q0 - n_side, 0, length - key_w), n_side)
            bias = bias_ref[(q0 - k0) // n_side]
            lse_rows = lse_ref[0, pl.ds(q0, Q_BLOCK), :]
            for rr in range(n_res):
                for pair in range(GROUP_W // LANES):
                    lo = rr * GROUP_W + pair * LANES
                    cols = slice(lo, lo + LANES)
                    q2 = q_ref[0, pl.ds(q0, Q_BLOCK), cols]
                    k2 = k_ref[0, pl.ds(k0, key_w), cols]
                    v2 = v_ref[0, pl.ds(k0, key_w), cols]
                    qs = jnp.concatenate([q2 * head_mask[0], q2 * head_mask[1]], axis=0)
                    s = lax.dot_general(qs, k2, (((1,), (1,)), ((), ())), preferred_element_type=_F32) + bias
                    m = jnp.max(s, axis=1, keepdims=True)
                    p = jnp.exp(s - m)
                    l = jnp.sum(p, axis=1, keepdims=True)
                    o = jnp.dot(p.astype(_BF16), v2, preferred_element_type=_F32) * (1.0 / l)
                    lse = m + jnp.log(l)
                    for hh in range(2):
                        col = (rb * n_res + rr) * HEADS_PER_GROUP + pair * 2 + hh
                        lse_rows = jnp.where(lse_lane == col, lse[hh * Q_BLOCK:(hh + 1) * Q_BLOCK], lse_rows)
                    o_ref[0, pl.ds(q0, Q_BLOCK), cols] = jnp.where(lane < HEAD_DIM, o[:Q_BLOCK], o[Q_BLOCK:]).astype(_BF16)
            lse_ref[0, pl.ds(q0, Q_BLOCK), :] = lse_rows
        return carry

    lax.fori_loop(0, n_qb // q_unroll, trip, 0)


ATTN_BLOCKS_PER_TRIP = 4


def _attention(q, k, v, window, dilation):
    B, L, _ = q.shape
    d = dilation
    n_side = window // (2 * d)
    n_res = min(d, ATTN_BLOCKS_PER_TRIP)
    q_unroll = min(ATTN_BLOCKS_PER_TRIP // n_res, L // Q_BLOCK)
    key_w = min(L, Q_BLOCK + 2 * n_side)
    spec = pl.BlockSpec((1, L, n_res * GROUP_W), lambda b, r: (b, 0, r))
    return pl.pallas_call(
        functools.partial(_attn_kernel, length=L, n_side=n_side, n_res=n_res, q_unroll=q_unroll),
        grid=(B, d // n_res),
        in_specs=[spec, spec, spec],
        out_specs=[spec, pl.BlockSpec((1, L, d * HEADS_PER_GROUP), lambda b, r: (b, 0, 0))],
        out_shape=[jax.ShapeDtypeStruct((B, L, d * GROUP_W), _BF16),
                   jax.ShapeDtypeStruct((B, L, d * HEADS_PER_GROUP), _F32)],
        scratch_shapes=[pltpu.VMEM((3, 2 * Q_BLOCK, key_w), _F32)],
        compiler_params=_params(("arbitrary", "arbitrary")),
        name=f"attention_d{d}",
    )(q, k, v)


def _route(sel):
    s = [sel[e:e + 1, :] for e in range(N_EXPERTS)]
    group_score = []
    for g in range(N_EXPERT_GROUPS):
        c0, c1, c2, c3 = s[EXPERTS_PER_GROUP * g:EXPERTS_PER_GROUP * (g + 1)]
        m1, n1 = jnp.maximum(c0, c1), jnp.minimum(c0, c1)
        m2, n2 = jnp.maximum(c2, c3), jnp.minimum(c2, c3)
        group_score.append(jnp.maximum(m1, m2) + jnp.maximum(jnp.minimum(m1, m2), jnp.maximum(n1, n2)))
    best, top_g = group_score[0], jnp.zeros(group_score[0].shape, _I32)
    for g in range(1, N_EXPERT_GROUPS):
        better = group_score[g] > best
        best = jnp.where(better, group_score[g], best)
        top_g = jnp.where(better, g, top_g)
    cs = []
    for j in range(EXPERTS_PER_GROUP):
        c = s[j]
        for g in range(1, N_EXPERT_GROUPS):
            c = jnp.where(top_g == g, s[EXPERTS_PER_GROUP * g + j], c)
        cs.append(c)
    s1, i1 = cs[0], jnp.zeros_like(top_g)
    for j in range(1, EXPERTS_PER_GROUP):
        better = cs[j] > s1
        s1 = jnp.where(better, cs[j], s1)
        i1 = jnp.where(better, j, i1)
    i2 = jnp.where(i1 == 0, 1, 0)
    s2 = jnp.where(i1 == 0, cs[1], cs[0])
    for j in range(1, EXPERTS_PER_GROUP):
        better = (cs[j] > s2) & (i1 != j)
        s2 = jnp.where(better, cs[j], s2)
        i2 = jnp.where(better, j, i2)
    lo, hi = jnp.minimum(i1, i2), jnp.maximum(i1, i2)
    pair = jnp.where(lo == 0, 0, jnp.where(lo == 1, 3, 5)) + (hi - lo - 1)
    return top_g * PAIRS_PER_GROUP + pair


def _mix_out_kernel(x_ref, o0_ref, o1_ref, o2_ref, l0_ref, l1_ref, l2_ref, u_ref, up_ref, un_ref,
                    mod_ref, wpool_ref, pscale_ref, wout_ref, gain_ref, wr_hi_ref, wr_lo_ref, rbias_ref,
                    x1_ref, h2_ref, cls_ref, ext_ref, lvl_a_ref, lvl_b_ref, lvl_c_ref, nat_o1a_ref, nat_o1b_ref,
                    nat_o2a_ref, nat_o2b_ref, nat_l1_ref, nat_l2_ref, *, seq_len):
    nat_o_refs = ((nat_o1a_ref, nat_o1b_ref), (nat_o2a_ref, nat_o2b_ref))
    nat_l_refs = (nat_l1_ref, nat_l2_ref)
    i = pl.program_id(1)
    n_i = pl.num_programs(1)
    tm = x_ref.shape[1]
    mod = mod_ref[0]

    for g, (o_ref, l_ref) in enumerate(((o0_ref, l0_ref), (o1_ref, l1_ref), (o2_ref, l2_ref))):
        dilation = DILATED_GROUPS[g][1]
        if dilation == 1:
            continue
        rows = tm // dilation
        lv = l_ref[0]
        lv = jnp.concatenate([lv, jnp.zeros((rows, LANES - lv.shape[1]), _F32)], axis=1)
        halves = nat_o_refs[g - 1]
        for r in range(dilation):
            for c, half_ref in enumerate(halves):
                lo = r * GROUP_W + c * LANES
                half_ref[pl.ds(r, rows, stride=dilation), :] = o_ref[0, :, lo:lo + LANES].astype(_F32)
            shifted = lv if r == 0 else pltpu.roll(lv, LANES - r * HEADS_PER_GROUP, axis=1)
            nat_l_refs[g - 1][pl.ds(r, rows, stride=dilation), :] = shifted

    pad = 2 * SUBLANES
    zeros_pad = jnp.zeros((SUBLANES, POOL_W), _F32)
    for ref in (ext_ref, lvl_a_ref, lvl_b_ref, lvl_c_ref):
        ref[0:SUBLANES, :] = zeros_pad
        ref[tm + pad + SUBLANES:tm + 2 * pad, :] = zeros_pad
    ext_ref[SUBLANES:pad, :] = jnp.where(i > 0, up_ref[0], 0.0)
    ext_ref[pad:pad + tm, :] = u_ref[0]
    ext_ref[pad + tm:pad + tm + SUBLANES, :] = jnp.where(i < n_i - 1, un_ref[0], 0.0)
    span = tm + 2 * SUBLANES
    win = lambda ref, off: ref[SUBLANES + off:SUBLANES + off + span, :]
    lvl_a_ref[SUBLANES:SUBLANES + span, :] = win(ext_ref, -1) + win(ext_ref, 0)
    lvl_b_ref[SUBLANES:SUBLANES + span, :] = win(lvl_a_ref, -1) + win(lvl_a_ref, 1)
    lvl_c_ref[SUBLANES:SUBLANES + span, :] = win(lvl_b_ref, -2) + win(lvl_b_ref, 2)
    lane = lax.broadcasted_iota(_I32, (1, POOL_W), 1)
    wgroup = lane // POOL_GROUP_W
    half_w = jnp.where(wgroup == 0, 1, jnp.where(wgroup == 1, 2, jnp.where(wgroup == 2, 4, 8)))

    half = tm // MIX_CHUNKS
    for hf in range(MIX_CHUNKS):
        r0 = hf * half
        rows = slice(r0, r0 + half)
        outs = [o0_ref[0, rows, :].astype(_F32)]
        lses = [l0_ref[0, rows, :]]
        for g in range(1, N_GROUPS):
            outs.append(jnp.concatenate([ref[rows, :] for ref in nat_o_refs[g - 1]], axis=1))
            lses.append(nat_l_refs[g - 1][rows, 0:HEADS_PER_GROUP])

        top = jnp.maximum(jnp.maximum(lses[0], lses[1]), lses[2])
        es = [jnp.exp(t - top) for t in lses]
        den = es[0] + es[1] + es[2]
        pieces = []
        for g in range(N_GROUPS):
            alpha = es[g] / den
            wide = jnp.concatenate([jnp.broadcast_to(alpha[:, h:h + 1], (half, HEAD_DIM))
                                    for h in range(HEADS_PER_GROUP)], axis=1)
            pieces.append((outs[g] * wide).astype(_BF16))

        s2 = lvl_a_ref[pad + r0:pad + r0 + half, :]
        s4 = lvl_b_ref[pad + r0:pad + r0 + half, :]
        s8 = lvl_c_ref[pad + r0:pad + r0 + half, :]
        s16 = lvl_c_ref[pad + r0 - 4:pad + r0 - 4 + half, :] + lvl_c_ref[pad + r0 + 4:pad + r0 + 4 + half, :]
        wsum = jnp.where(wgroup == 0, s2, jnp.where(wgroup == 1, s4, jnp.where(wgroup == 2, s8, s16)))
        pos = i * tm + r0 + lax.broadcasted_iota(_I32, (half, 1), 0)
        cnt = jnp.minimum(pos + half_w, seq_len) - jnp.maximum(pos - half_w, 0)
        z = wsum / cnt.astype(_F32) - u_ref[0, rows, :]
        pool = jnp.dot(z.astype(_BF16), wpool_ref[...], preferred_element_type=_F32) * pscale_ref[...]
        pieces.append(pool.astype(_BF16))

        mix = jnp.dot(jnp.concatenate(pieces, axis=1), wout_ref[...], preferred_element_type=_F32)
        x1 = x_ref[0, rows, :] + mod[2:3] * mix
        x1_ref[0, rows, :] = x1

        h2 = _rmsnorm_mod(x1, gain_ref[...], mod[4:5], mod[3:4])
        h_hi = h2.astype(_BF16)
        h2_ref[0, rows, :] = h_hi
        h_lo = (h2 - h_hi.astype(_F32)).astype(_BF16)
        nt = (((1,), (1,)), ((), ()))
        logits = (lax.dot_general(wr_hi_ref[...], h_hi, nt, preferred_element_type=_F32)
                  + lax.dot_general(wr_lo_ref[...], h_hi, nt, preferred_element_type=_F32)
                  + lax.dot_general(wr_hi_ref[...], h_lo, nt, preferred_element_type=_F32))
        aff = 1.0 / (1.0 + jnp.exp(-logits))
        cls_ref[0, :, rows] = _route(aff + rbias_ref[...])


def _mix_out(x, attn, lses, u, mod_l, wpool_bd, pool_scale, w_out_bf16, gain, wr_hi, wr_lo, rbias, batch_offset,
             rider=None):
    B, S, _ = x.shape
    tm = TOKEN_TILE
    n_i = S // tm
    halo_blocks = tm // SUBLANES
    tok = lambda w: pl.BlockSpec((1, tm, w), lambda b, i: (b, i, 0))
    const = lambda shape: pl.BlockSpec(shape, lambda b, i: tuple(0 for _ in shape))
    rows = tm + 4 * SUBLANES
    return _launch(
        functools.partial(_mix_out_kernel, seq_len=S),
        grid=(B, n_i),
        in_specs=[
            tok(D_MODEL),
            *[pl.BlockSpec((1, tm // d, d * GROUP_W), lambda b, i: (b, i, 0)) for _, d in DILATED_GROUPS],
            *[pl.BlockSpec((1, tm // d, d * HEADS_PER_GROUP), lambda b, i: (b, i, 0)) for _, d in DILATED_GROUPS],
            tok(POOL_W),
            pl.BlockSpec((1, SUBLANES, POOL_W), lambda b, i: (b, jnp.maximum(i * halo_blocks - 1, 0), 0)),
            pl.BlockSpec((1, SUBLANES, POOL_W),
                         lambda b, i: (b, jnp.minimum((i + 1) * halo_blocks, S // SUBLANES - 1), 0)),
            pl.BlockSpec((1, 6, D_MODEL), lambda b, i: (b + batch_offset, 0, 0)),
            const((POOL_W, POOL_W)), const((1, POOL_W)), const((D_MODEL, D_MODEL)), const((1, D_MODEL)),
            const((N_EXPERTS, D_MODEL)), const((N_EXPERTS, D_MODEL)), const((N_EXPERTS, 1)),
        ],
        out_specs=[tok(D_MODEL), tok(D_MODEL), pl.BlockSpec((1, 1, tm), lambda b, i: (b, 0, i))],
        out_shape=[jax.ShapeDtypeStruct((B, S, D_MODEL), _F32),
                   jax.ShapeDtypeStruct((B, S, D_MODEL), _BF16),
                   jax.ShapeDtypeStruct((B, 1, S), _I32)],
        scratch=[pltpu.VMEM((rows, POOL_W), _F32)] * 4 + [pltpu.VMEM((tm, LANES), _F32)] * 6,
        args=(x, *attn, *lses, u, u, u, mod_l, wpool_bd, pool_scale, w_out_bf16, gain, wr_hi, wr_lo, rbias),
        name="mix_out",
        rider=rider,
    )


TBL_EXPERT_A, TBL_EXPERT_B, TBL_TILE, TBL_USED, TBL_LAST_TILE_ROW = range(5)


def _plan_kernel(cls_ref, slot_ref, tbl_ref, cnt_ref, run_ref, base_ref, *, tile):
    phase, b, j = pl.program_id(0), pl.program_id(1), pl.program_id(2)
    first = (b == 0) & (j == 0)
    chunk = cls_ref.shape[2]
    n_tbl = tbl_ref.shape[1]
    onehot = lax.broadcasted_iota(_I32, (CLASS_ROWS, chunk), 0) == cls_ref[0]
    hits = jnp.sum(onehot.astype(_F32), axis=1, keepdims=True)

    @pl.when((phase == 0) & first)
    def _():
        cnt_ref[...] = jnp.zeros_like(cnt_ref)

    @pl.when(phase == 0)
    def _():
        cnt_ref[...] += hits

    @pl.when((phase == 1) & first)
    def _():
        padded = jnp.ceil(cnt_ref[...] * (1.0 / tile)) * tile
        r_i = lax.broadcasted_iota(_I32, (CLASS_ROWS, CLASS_ROWS), 0)
        c_i = lax.broadcasted_iota(_I32, (CLASS_ROWS, CLASS_ROWS), 1)
        padded_row = jnp.sum(jnp.where(r_i == c_i, padded, 0.0), axis=0, keepdims=True)
        base_col = jnp.sum(jnp.where(c_i < r_i, padded_row, 0.0), axis=1, keepdims=True)
        base_row = jnp.sum(jnp.where(r_i < c_i, padded, 0.0), axis=0, keepdims=True)
        total = jnp.sum(padded_row, axis=1, keepdims=True)
        base_ref[...] = base_col
        run_ref[...] = jnp.zeros_like(run_ref)

        k_row = lax.broadcasted_iota(_I32, (1, n_tbl), 1).astype(_F32)
        start = k_row * tile
        inside = (base_col <= start) & (start < base_col + padded)
        cls_id = lax.broadcasted_iota(_I32, (CLASS_ROWS, n_tbl), 0).astype(_F32)
        tile_cls = jnp.sum(jnp.where(inside, cls_id, 0.0), axis=0, keepdims=True)
        n_used = total * (1.0 / tile)
        last = jnp.maximum(n_used - 1.0, 0.0)
        last_cls = jnp.sum(jnp.where(k_row == last, tile_cls, 0.0), axis=1, keepdims=True)
        used = k_row < n_used
        tile_cls = jnp.where(used, tile_cls, last_cls)
        grp = sum(jnp.where(tile_cls >= PAIRS_PER_GROUP * g, 1.0, 0.0) for g in range(1, N_EXPERT_GROUPS))
        pair = tile_cls - grp * PAIRS_PER_GROUP
        a = jnp.where(pair < 3, 0.0, jnp.where(pair < 5, 1.0, 2.0))
        bb = jnp.where(pair < 3, pair + 1.0, jnp.where(pair < 5, pair - 1.0, 3.0))
        last_tile_row = jnp.where(padded_row > 0, base_row + padded_row - tile, -1.0)
        last_tile_row = jnp.concatenate(
            [last_tile_row, jnp.full((1, n_tbl - CLASS_ROWS), -1.0, _F32)], axis=1)
        zero = jnp.zeros((1, n_tbl), _F32)
        tbl_ref[...] = jnp.concatenate(
            [grp * EXPERTS_PER_GROUP + a, grp * EXPERTS_PER_GROUP + bb, jnp.where(used, k_row, last),
             jnp.where(used, 1.0, 0.0), last_tile_row, zero, zero, zero], axis=0).astype(_I32)

    @pl.when(phase == 1)
    def _():
        before = lax.broadcasted_iota(_I32, (chunk, chunk), 0) < lax.broadcasted_iota(_I32, (chunk, chunk), 1)
        earlier = jnp.dot(jnp.where(onehot, 1.0, 0.0).astype(_BF16), jnp.where(before, 1.0, 0.0).astype(_BF16),
                          preferred_element_type=_F32)
        dest = base_ref[...] + run_ref[...] + earlier
        slot_ref[0] = jnp.sum(jnp.where(onehot, dest, 0.0), axis=0, keepdims=True).astype(_I32)
        run_ref[...] += hits


def _moe_plan(cls, tile):
    B, _, S = cls.shape
    chunk = PLAN_CHUNK
    n_tiles = (B * S) // tile + N_CLASSES
    n_tbl = -(-n_tiles // LANES) * LANES
    slot, tbl = pl.pallas_call(
        functools.partial(_plan_kernel, tile=tile),
        grid=(2, B, S // chunk),
        in_specs=[pl.BlockSpec((1, 1, chunk), lambda p, b, j: (b, 0, j))],
        out_specs=[pl.BlockSpec((1, 1, chunk), lambda p, b, j: (b * p, 0, j * p)),
                   pl.BlockSpec((SUBLANES, n_tbl), lambda p, b, j: (0, 0))],
        out_shape=[jax.ShapeDtypeStruct((B, 1, S), _I32), jax.ShapeDtypeStruct((SUBLANES, n_tbl), _I32)],
        scratch_shapes=[pltpu.VMEM((CLASS_ROWS, 1), _F32)] * 3,
        compiler_params=_params(("arbitrary", "arbitrary", "arbitrary")),
        name="moe_plan",
    )(cls)
    return slot, tbl, n_tiles


def _row_copy(src, src_row, dst, dst_row, sem):
    return pltpu.make_async_copy(src.at[pl.ds(pl.multiple_of(src_row * ROW_CHUNKS, ROW_CHUNKS), ROW_CHUNKS), :],
                                 dst.at[pl.ds(pl.multiple_of(dst_row * ROW_CHUNKS, ROW_CHUNKS), ROW_CHUNKS), :],
                                 sem)


def _dispatch_start(tbl_ref, slot_ref, h_ref, hs_ref, stage_ref, zero_ref, sem, zsem, *, tile):
    first = (pl.program_id(0) == 0) & (pl.program_id(1) == 0)
    tm = h_ref.shape[1]
    tile_rows = tile * ROW_CHUNKS

    @pl.when(first)
    def _():
        zero_ref[...] = jnp.zeros_like(zero_ref)
        n_tiles = hs_ref.shape[0] // tile_rows
        for wait in (False, True):
            for c in range(N_CLASSES):
                row = tbl_ref[TBL_LAST_TILE_ROW, c]

                @pl.when(row >= 0)
                def _():
                    at = pl.multiple_of(row * ROW_CHUNKS, ROW_CHUNKS)
                    cp = pltpu.make_async_copy(zero_ref, hs_ref.at[pl.ds(at, tile_rows), :], zsem)
                    cp.wait() if wait else cp.start()

            for k in range(n_tiles - N_CLASSES, n_tiles):
                @pl.when(tbl_ref[TBL_USED, k] == 0)
                def _():
                    cp = pltpu.make_async_copy(zero_ref, hs_ref.at[pl.ds(k * tile_rows, tile_rows), :], zsem)
                    cp.wait() if wait else cp.start()

    h = h_ref[0]
    for c in range(ROW_CHUNKS):
        stage_ref[pl.ds(c, tm, stride=ROW_CHUNKS), :] = h[:, c * LANES:(c + 1) * LANES].astype(_F32)

    def issue(g, carry):
        for k in range(SUBLANES):
            r = g * SUBLANES + k
            _row_copy(stage_ref, r, hs_ref, slot_ref[0, 0, r], sem).start(priority=k % 2)
        return carry

    lax.fori_loop(0, tm // SUBLANES, issue, 0)


def _dispatch_wait(stage_ref, hs_ref, sem):
    pltpu.make_async_copy(stage_ref, hs_ref.at[pl.ds(0, stage_ref.shape[0]), :], sem).wait()


def _launch(kernel_fn, grid, in_specs, out_specs, out_shape, scratch, args, name, rider=None):
    params = _params(("arbitrary",) * len(grid))
    if rider is None:
        return pl.pallas_call(kernel_fn, grid=grid, in_specs=in_specs, out_specs=out_specs, out_shape=out_shape,
                              scratch_shapes=scratch, compiler_params=params, name=name)(*args)
    tbl, slot, h2, n_tiles = rider
    tile = MOE_TILE
    steps = grid[0] * grid[1]
    Bo, So, _ = h2.shape
    share = (Bo * So) // steps
    per_seq = So // share
    n_in, n_out, n_scr = len(in_specs), len(out_specs), len(scratch)

    def drop_tbl(spec):
        if spec.index_map is None:
            return spec
        return pl.BlockSpec(spec.block_shape, lambda *a, f=spec.index_map: f(*a[:-1]), memory_space=spec.memory_space)

    def kern(tbl_ref, *refs):
        host_in, (slot_ref, h_ref) = refs[:n_in], refs[n_in:n_in + 2]
        host_out, hs_ref = refs[n_in + 2:n_in + 2 + n_out], refs[n_in + 2 + n_out]
        host_scr = refs[n_in + 3 + n_out:n_in + 3 + n_out + n_scr]
        stage_ref, zero_ref, sem, zsem = refs[n_in + 3 + n_out + n_scr:]
        _dispatch_start(tbl_ref, slot_ref, h_ref, hs_ref, stage_ref, zero_ref, sem, zsem, tile=tile)
        kernel_fn(*host_in, *host_out, *host_scr)
        _dispatch_wait(stage_ref, hs_ref, sem)

    step_of = lambda b, i: b * grid[1] + i
    rider_in = [pl.BlockSpec((1, 1, share), lambda b, i, t: (step_of(b, i) // per_seq, 0, step_of(b, i) % per_seq),
                             memory_space=pltpu.SMEM),
                pl.BlockSpec((1, share, D_MODEL), lambda b, i, t: (step_of(b, i) // per_seq, step_of(b, i) % per_seq, 0))]
    return pl.pallas_call(
        kern,
        grid_spec=pltpu.PrefetchScalarGridSpec(
            num_scalar_prefetch=1,
            grid=grid,
            in_specs=[drop_tbl(s) for s in in_specs] + rider_in,
            out_specs=[drop_tbl(s) for s in out_specs] + [pl.BlockSpec(memory_space=pl.ANY)],
            scratch_shapes=list(scratch) + [pltpu.VMEM((share * ROW_CHUNKS, LANES), _F32),
                                            pltpu.VMEM((tile * ROW_CHUNKS, LANES), _F32),
                                            pltpu.SemaphoreType.DMA(()), pltpu.SemaphoreType.DMA(())],
        ),
        out_shape=list(out_shape) + [jax.ShapeDtypeStruct((n_tiles * tile * ROW_CHUNKS, LANES), _F32)],
        compiler_params=params,
        name=name + "_dispatch",
    )(tbl, *args, slot, h2)


def _experts_kernel(tbl_ref, hs_ref, wra_hi_ref, wra_lo_ref, wrb_hi_ref, wrb_lo_ref,
                    wga_ref, wua_ref, wda_ref, wgb_ref, wub_ref, wdb_ref, ys_ref):
    k = pl.program_id(0)
    tile = hs_ref.shape[0] // ROW_CHUNKS

    @pl.when(tbl_ref[TBL_USED, k] == 0)
    def _():
        ys_ref[...] = jnp.zeros_like(ys_ref)

    @pl.when(tbl_ref[TBL_USED, k] == 1)
    def _():
        x = jnp.concatenate([hs_ref[pl.ds(c, tile, stride=ROW_CHUNKS), :].astype(_BF16)
                             for c in range(ROW_CHUNKS)], axis=1)
        row_id = lax.broadcasted_iota(_I32, (2 * SUBLANES, D_MODEL), 0)
        rows = jnp.zeros((2 * SUBLANES, D_MODEL), _F32)
        for n, ref in enumerate((wra_hi_ref, wra_lo_ref, wrb_hi_ref, wrb_lo_ref)):
            rows = jnp.where(row_id == n, ref[0], rows)
        lg = lax.dot_general(x, rows.astype(_BF16), (((1,), (1,)), ((), ())),
                             preferred_element_type=_F32)
        aff_a = 1.0 / (1.0 + jnp.exp(-(lg[:, 0:1] + lg[:, 1:2])))
        aff_b = 1.0 / (1.0 + jnp.exp(-(lg[:, 2:3] + lg[:, 3:4])))
        den = aff_a + aff_b

        def hidden(wg_ref, wu_ref, gate):
            a = jnp.dot(x, wg_ref[0], preferred_element_type=_F32)
            b = jnp.dot(x, wu_ref[0], preferred_element_type=_F32)
            return ((a / (1.0 + jnp.exp(-a))) * b * gate).astype(_BF16)

        y = (jnp.dot(hidden(wga_ref, wua_ref, aff_a / den), wda_ref[0], preferred_element_type=_F32)
             + jnp.dot(hidden(wgb_ref, wub_ref, aff_b / den), wdb_ref[0], preferred_element_type=_F32))
        for c in range(ROW_CHUNKS):
            ys_ref[pl.ds(c, tile, stride=ROW_CHUNKS), :] = y[:, c * LANES:(c + 1) * LANES]


def _moe_experts(tbl, hs, wr_hi, wr_lo, wg, wu, wd, layer, n_tiles, tile):
    rows = tile * ROW_CHUNKS
    first = layer * N_EXPERTS
    tile_spec = pl.BlockSpec((rows, LANES), lambda k, tbl: (tbl[TBL_TILE, k], 0))
    router = lambda row: pl.BlockSpec((1, 1, D_MODEL), lambda k, tbl: (tbl[row, k], 0, 0))
    up = lambda row: pl.BlockSpec((1, D_MODEL, D_EXPERT), lambda k, tbl: (first + tbl[row, k], 0, 0))
    down = lambda row: pl.BlockSpec((1, D_EXPERT, D_MODEL), lambda k, tbl: (first + tbl[row, k], 0, 0))
    a, b = TBL_EXPERT_A, TBL_EXPERT_B
    return pl.pallas_call(
        _experts_kernel,
        grid_spec=pltpu.PrefetchScalarGridSpec(
            num_scalar_prefetch=1,
            grid=(n_tiles,),
            in_specs=[tile_spec, router(a), router(a), router(b), router(b),
                      up(a), up(a), down(a), up(b), up(b), down(b)],
            out_specs=pl.BlockSpec((rows, LANES), lambda k, tbl: (k, 0)),
        ),
        out_shape=jax.ShapeDtypeStruct(hs.shape, _F32),
        compiler_params=_params(("arbitrary",)),
        name="moe_experts",
    )(tbl, hs, wr_hi, wr_lo, wr_hi, wr_lo, wg, wu, wd, wg, wu, wd)


def _combine_kernel(slot_ref, slot_next_ref, x1_ref, mod_ref, ys_ref, *refs, project):
    if project:
        (modn_ref, gain_ref, w_ref, cos_ref, sin_ref, x2_ref, *out_refs) = refs[:16]
        buf_refs, sems, stage_ref = refs[16:18], refs[18:20], refs[20]
    else:
        gfin_ref, x2_ref = refs[:2]
        buf_refs, sems = refs[2:4], refs[4:6]
    tm = x1_ref.shape[1]
    half = tm // 2
    step = pl.program_id(0) * pl.num_programs(1) + pl.program_id(1)
    last = pl.num_programs(0) * pl.num_programs(1) - 1
    gate = mod_ref[0][5:6]

    def gather(sref, row0, n):
        def issue(g, carry):
            for k in range(SUBLANES):
                r = g * SUBLANES + k
                _row_copy(ys_ref, sref[0, 0, row0 + r], buf_refs[n], r, sems[n]).start(priority=k % 2)
            return carry
        lax.fori_loop(0, half // SUBLANES, issue, 0)

    def residual(n):
        pltpu.make_async_copy(ys_ref.at[pl.ds(0, half * ROW_CHUNKS), :], buf_refs[n], sems[n]).wait()
        rows = slice(n * half, (n + 1) * half)
        for c in range(ROW_CHUNKS):
            cols = slice(c * LANES, (c + 1) * LANES)
            x2_ref[0, rows, cols] = x1_ref[0, rows, cols] + gate[:, cols] * buf_refs[n][pl.ds(c, half, stride=ROW_CHUNKS), :]

    def tail(n):
        rows = slice(n * half, (n + 1) * half)
        x2 = x2_ref[0, rows, :]
        if project:
            _project(x2, n * half, modn_ref[0], gain_ref, w_ref, cos_ref, sin_ref, out_refs[0:3], out_refs[3:6],
                     out_refs[6:9], out_refs[9], stage_ref)
        else:
            x2_ref[0, rows, :] = x2 * lax.rsqrt(jnp.mean(x2 * x2, axis=-1, keepdims=True) + RMS_EPS) * gfin_ref[...]

    @pl.when(step == 0)
    def _():
        gather(slot_ref, 0, 0)

    gather(slot_ref, half, 1)
    residual(0)

    @pl.when(step < last)
    def _():
        gather(slot_next_ref, 0, 0)

    tail(0)
    residual(1)
    tail(1)


def _moe_combine(slot, x1, mod_prev, ys, batch_offset, tail_args, rider=None):
    B, S, _ = x1.shape
    tm = TOKEN_TILE
    half = tm // 2
    n_i = S // tm
    project = len(tail_args) > 1

    def next_step(b, i):
        n = jnp.minimum(b * n_i + i + 1, B * n_i - 1)
        return n // n_i, 0, n % n_i

    tok = pl.BlockSpec((1, tm, D_MODEL), lambda b, i: (b, i, 0))
    mod_spec = pl.BlockSpec((1, 6, D_MODEL), lambda b, i: (b + batch_offset, 0, 0))
    in_specs = [pl.BlockSpec((1, 1, tm), lambda b, i: (b, 0, i), memory_space=pltpu.SMEM),
                pl.BlockSpec((1, 1, tm), next_step, memory_space=pltpu.SMEM),
                tok, mod_spec, pl.BlockSpec(memory_space=pl.ANY)]
    out_specs, out_shape = [tok], [jax.ShapeDtypeStruct((B, S, D_MODEL), _F32)]
    scratch = [pltpu.VMEM((half * ROW_CHUNKS, LANES), _F32)] * 2 + [pltpu.SemaphoreType.DMA(())] * 2
    if project:
        in_specs += [mod_spec,
                     pl.BlockSpec((1, D_MODEL), lambda b, i: (0, 0)),
                     pl.BlockSpec((D_MODEL, IN_W), lambda b, i: (0, 0)),
                     pl.BlockSpec((tm, LANES), lambda b, i: (i, 0)),
                     pl.BlockSpec((tm, LANES), lambda b, i: (i, 0))]
        out_specs += [pl.BlockSpec((1, tm // d, d * GROUP_W), lambda b, i: (b, i, 0)) for _, d in DILATED_GROUPS] * 3
        out_specs += [pl.BlockSpec((1, tm, POOL_W), lambda b, i: (b, i, 0))]
        out_shape += [jax.ShapeDtypeStruct((B, S // d, d * GROUP_W), _BF16) for _, d in DILATED_GROUPS] * 3
        out_shape += [jax.ShapeDtypeStruct((B, S, POOL_W), _F32)]
        scratch += [pltpu.VMEM((half, LANES), _F32)]
    else:
        in_specs += [pl.BlockSpec((1, D_MODEL), lambda b, i: (0, 0))]
    return _launch(
        functools.partial(_combine_kernel, project=project),
        grid=(B, n_i),
        in_specs=in_specs,
        out_specs=out_specs,
        out_shape=out_shape,
        scratch=scratch,
        args=(slot, slot, x1, mod_prev, ys, *tail_args),
        name="moe_combine_in_proj" if project else "moe_combine_norm",
        rider=rider,
    )


def _rope_tables(S):
    inv = 1.0 / (ROPE_THETA ** (jnp.arange(0, HEAD_DIM, 2, dtype=_F32) / HEAD_DIM))
    ang = jnp.arange(S, dtype=_F32)[:, None] * inv[None, :]
    cos, sin = jnp.cos(ang), jnp.sin(ang)
    reps = LANES // HEAD_DIM
    return (jnp.tile(jnp.concatenate([cos, cos], axis=1), (1, reps)),
            jnp.tile(jnp.concatenate([-sin, sin], axis=1), (1, reps)))


def _trunks(xs, batch_offsets, mod, w):
    tables = [_rope_tables(x.shape[1]) for x in xs]
    outs = [_in_proj(x, mod[0], w["norm_mix"][0], w["w_in"][0], *tables[t], batch_offsets[t])
            for t, x in enumerate(xs)]
    xs = list(xs)
    for l in range(DEPTH):
        mod_l = mod[l]

        def mix(t, rider):
            q, k, v, u = outs[t][0:3], outs[t][3:6], outs[t][6:9], outs[t][9]
            attn, lses = [], []
            for g, (window, dilation) in enumerate(DILATED_GROUPS):
                o, lse = _attention(q[g], k[g], v[g], window, dilation)
                attn.append(o)
                lses.append(lse)
            return _mix_out(xs[t], attn, lses, u, mod_l, w["wpool_bd"][l], w["pool_scale"][l], w["w_out"][l],
                            w["norm_ffn"][l], w["wr_hi"], w["wr_lo"], w["rbias"], batch_offsets[t], rider)

        def experts(tbl, hs, n_tiles):
            return _moe_experts(tbl, hs, w["wr_hi3"], w["wr_lo3"], w["w_gate"], w["w_up"], w["w_down"], l,
                                n_tiles, MOE_TILE)

        def combine(t, slot, x1, ys, rider):
            if l + 1 < DEPTH:
                tail = (mod[l + 1], w["norm_mix"][l + 1], w["w_in"][l + 1], *tables[t])
            else:
                tail = (w["norm_final"],)
            return _moe_combine(slot, x1, mod_l, ys, batch_offsets[t], tail, rider)

        x1_0, h2_0, cls_0 = mix(0, None)
        slot_0, tbl_0, n_0 = _moe_plan(cls_0, MOE_TILE)
        x1_1, h2_1, cls_1, hs_0 = mix(1, (tbl_0, slot_0, h2_0, n_0))
        slot_1, tbl_1, n_1 = _moe_plan(cls_1, MOE_TILE)
        ys_0 = experts(tbl_0, hs_0, n_0)
        xs[0], *rest = combine(0, slot_0, x1_0, ys_0, (tbl_1, slot_1, h2_1, n_1))
        outs[0], hs_1 = rest[:-1], rest[-1]
        ys_1 = experts(tbl_1, hs_1, n_1)
        xs[1], *outs[1] = combine(1, slot_1, x1_1, ys_1, None)
    return xs


def kernel(x_prompt, x_sample, c_prompt, c_sample, norm_mix, w_mod, b_mod, w_in, w_pool, pool_scale, w_out,
           norm_ffn, w_router, router_bias, w_gate, w_up, w_down, norm_final):
    n_prompt = x_prompt.shape[0]
    c = jnp.concatenate([c_prompt, c_sample], axis=0)
    mod = _modulation(c, w_mod, b_mod).reshape(DEPTH, c.shape[0], 6, D_MODEL)
    n_pool = len(POOL_WINDOWS)
    eye = jnp.eye(n_pool, dtype=w_pool.dtype)
    wpool_bd = (w_pool[:, :, :, None, :] * eye[None, :, None, :, None]).reshape(DEPTH, POOL_W, POOL_W)
    wr_t = w_router.T.astype(_F32)
    wr_hi = wr_t.astype(_BF16)
    wr_lo = (wr_t - wr_hi.astype(_F32)).astype(_BF16)
    w = {
        "norm_mix": norm_mix.reshape(DEPTH, 1, D_MODEL),
        "norm_ffn": norm_ffn.reshape(DEPTH, 1, D_MODEL),
        "norm_final": norm_final.reshape(1, D_MODEL),
        "w_in": w_in.astype(_BF16),
        "w_out": w_out.astype(_BF16),
        "wpool_bd": wpool_bd.astype(_BF16),
        "pool_scale": pool_scale.reshape(DEPTH, 1, POOL_W),
        "wr_hi": wr_hi,
        "wr_lo": wr_lo,
        "wr_hi3": wr_hi.astype(_F32).reshape(N_EXPERTS, 1, D_MODEL),
        "wr_lo3": wr_lo.astype(_F32).reshape(N_EXPERTS, 1, D_MODEL),
        "rbias": router_bias.astype(_F32).reshape(N_EXPERTS, 1),
        "w_gate": w_gate.astype(_BF16).reshape(DEPTH * N_EXPERTS, D_MODEL, D_EXPERT),
        "w_up": w_up.astype(_BF16).reshape(DEPTH * N_EXPERTS, D_MODEL, D_EXPERT),
        "w_down": w_down.astype(_BF16).reshape(DEPTH * N_EXPERTS, D_EXPERT, D_MODEL),
    }
    y_prompt, y_sample = _trunks((x_prompt, x_sample), (0, n_prompt), mod, w)
    return (y_prompt, y_sample)
```

```python
import functools

import jax
import jax.numpy as jnp
from jax import lax
from jax.experimental import pallas as pl
from jax.experimental.pallas import tpu as pltpu

D_MODEL = 1024
DEPTH = 4
HEAD_DIM = 64
HALF_HEAD = HEAD_DIM // 2
DILATED_GROUPS = ((128, 1), (512, 4), (2048, 16))
N_GROUPS = len(DILATED_GROUPS)
HEADS_PER_GROUP = 4
GROUP_W = HEADS_PER_GROUP * HEAD_DIM
ATT_W = N_GROUPS * GROUP_W
POOL_WINDOWS = (2, 4, 8, 16)
POOL_GROUP_W = 64
POOL_W = POOL_GROUP_W * len(POOL_WINDOWS)
IN_W = 3 * ATT_W + POOL_W
ROPE_THETA = 10000.0
RMS_EPS = 1e-6
N_EXPERTS = 16
N_EXPERT_GROUPS = 4
EXPERTS_PER_GROUP = N_EXPERTS // N_EXPERT_GROUPS
PAIRS_PER_GROUP = EXPERTS_PER_GROUP * (EXPERTS_PER_GROUP - 1) // 2
N_CLASSES = N_EXPERT_GROUPS * PAIRS_PER_GROUP
D_EXPERT = 512

LANES = 128
SUBLANES = 8
Q_BLOCK = 128
VMEM_LIMIT = 48 * 1024 * 1024

TOKEN_TILE = 512
MOE_TILE = 256
MIX_CHUNKS = 1
PLAN_CHUNK = 512
CLASS_ROWS = 32
ROW_CHUNKS = D_MODEL // LANES

_BF16 = jnp.bfloat16
_F32 = jnp.float32
_I32 = jnp.int32


def _params(semantics):
    return pltpu.CompilerParams(dimension_semantics=semantics, vmem_limit_bytes=VMEM_LIMIT)


def _mod_kernel(c_ref, w_ref, b_ref, o_ref):
    c = c_ref[...]
    sc = c / (1.0 + jnp.exp(-c))
    o_ref[0] = jnp.dot(sc, w_ref[0], preferred_element_type=_F32,
                       precision=lax.Precision.HIGHEST) + b_ref[0]


def _modulation(c, w_mod, b_mod):
    nb = c.shape[0]
    col = D_MODEL
    n_col = w_mod.shape[2] // col
    return pl.pallas_call(
        _mod_kernel,
        grid=(DEPTH, n_col),
        in_specs=[
            pl.BlockSpec((nb, D_MODEL), lambda l, j: (0, 0)),
            pl.BlockSpec((1, D_MODEL, col), lambda l, j: (l, 0, j)),
            pl.BlockSpec((1, 1, col), lambda l, j: (l, 0, j)),
        ],
        out_specs=pl.BlockSpec((1, nb, col), lambda l, j: (l, 0, j)),
        out_shape=jax.ShapeDtypeStruct((DEPTH, nb, w_mod.shape[2]), _F32),
        compiler_params=_params(("arbitrary", "arbitrary")),
        name="modulation",
    )(c, w_mod, b_mod.reshape(DEPTH, 1, -1))


def _rmsnorm_mod(x, gain, scale, shift):
    y = x * lax.rsqrt(jnp.mean(x * x, axis=-1, keepdims=True) + RMS_EPS)
    return (y * gain) * (1.0 + scale) + shift


def _rope_chunk(t, cos, sin_signed, first_half):
    fwd = pltpu.roll(t, HALF_HEAD, axis=1)
    bwd = pltpu.roll(t, LANES - HALF_HEAD, axis=1)
    return t * cos + jnp.where(first_half, bwd, fwd) * sin_signed


def _store_by_residue(out_ref, chunk, value, dilation, stage_ref, row0):
    rows = value.shape[0] // dilation
    at = slice(row0 // dilation, row0 // dilation + rows)
    if dilation == 1:
        out_ref[0, at, chunk * LANES:(chunk + 1) * LANES] = value.astype(_BF16)
        return
    stage_ref[...] = value
    for r in range(dilation):
        lo = r * GROUP_W + chunk * LANES
        out_ref[0, at, lo:lo + LANES] = stage_ref[pl.ds(r, rows, stride=dilation), :].astype(_BF16)


def _in_proj_kernel(x_ref, mod_ref, gain_ref, w_ref, cos_ref, sin_ref, *refs):
    _project(x_ref[0], 0, mod_ref[0], gain_ref, w_ref, cos_ref, sin_ref, refs[0:3], refs[3:6], refs[6:9], refs[9],
             refs[10])


def _project(x, row0, mod, gain_ref, w_ref, cos_ref, sin_ref, q_refs, k_refs, v_refs, u_ref, stage_ref):
    n = x.shape[0]
    h = _rmsnorm_mod(x, gain_ref[...], mod[1:2], mod[0:1])
    proj = jnp.dot(h.astype(_BF16), w_ref[...], preferred_element_type=_F32)
    cos = cos_ref[row0:row0 + n, :]
    sin_signed = sin_ref[row0:row0 + n, :]
    lane = lax.broadcasted_iota(_I32, (1, LANES), 1)
    first_half = (lane % HEAD_DIM) < HALF_HEAD
    q_scale = HEAD_DIM ** -0.5
    for g, (_, dilation) in enumerate(DILATED_GROUPS):
        for c in range(GROUP_W // LANES):
            lo = g * GROUP_W + c * LANES
            q = _rope_chunk(proj[:, lo:lo + LANES], cos, sin_signed, first_half)
            _store_by_residue(q_refs[g], c, q * q_scale, dilation, stage_ref, row0)
            k = _rope_chunk(proj[:, ATT_W + lo:ATT_W + lo + LANES], cos, sin_signed, first_half)
            _store_by_residue(k_refs[g], c, k, dilation, stage_ref, row0)
            _store_by_residue(v_refs[g], c, proj[:, 2 * ATT_W + lo:2 * ATT_W + lo + LANES], dilation, stage_ref,
                              row0)
    u_ref[0, row0:row0 + n, :] = proj[:, 3 * ATT_W:]


def _in_proj(x, mod_l, gain, w_in_bf16, cos, sin_signed, batch_offset):
    B, S, _ = x.shape
    tm = TOKEN_TILE
    grp = [jax.ShapeDtypeStruct((B, S // d, d * GROUP_W), _BF16) for _, d in DILATED_GROUPS]
    grp_spec = [pl.BlockSpec((1, tm // d, d * GROUP_W), lambda b, i: (b, i, 0)) for _, d in DILATED_GROUPS]
    return pl.pallas_call(
        _in_proj_kernel,
        grid=(B, S // tm),
        in_specs=[
            pl.BlockSpec((1, tm, D_MODEL), lambda b, i: (b, i, 0)),
            pl.BlockSpec((1, 6, D_MODEL), lambda b, i: (b + batch_offset, 0, 0)),
            pl.BlockSpec((1, D_MODEL), lambda b, i: (0, 0)),
            pl.BlockSpec((D_MODEL, IN_W), lambda b, i: (0, 0)),
            pl.BlockSpec((tm, LANES), lambda b, i: (i, 0)),
            pl.BlockSpec((tm, LANES), lambda b, i: (i, 0)),
        ],
        out_specs=grp_spec * 3 + [pl.BlockSpec((1, tm, POOL_W), lambda b, i: (b, i, 0))],
        out_shape=grp * 3 + [jax.ShapeDtypeStruct((B, S, POOL_W), _F32)],
        scratch_shapes=[pltpu.VMEM((tm, LANES), _F32)],
        compiler_params=_params(("arbitrary", "arbitrary")),
        name="in_proj",
    )(x, mod_l, gain, w_in_bf16, cos, sin_signed)


def _attn_kernel(q_ref, k_ref, v_ref, o_ref, lse_ref, bias_ref, *, length, n_side, n_res, q_unroll):
    rb = pl.program_id(1)
    key_w = min(length, Q_BLOCK + 2 * n_side)
    n_qb = length // Q_BLOCK

    @pl.when((pl.program_id(0) == 0) & (rb == 0))
    def _():
        i = lax.broadcasted_iota(_I32, (2 * Q_BLOCK, key_w), 0) % Q_BLOCK
        j = lax.broadcasted_iota(_I32, (2 * Q_BLOCK, key_w), 1)
        for n in range(3):
            bias_ref[n] = jnp.where(jnp.abs(i - j + n * n_side) <= n_side, 0.0, -jnp.inf)

    @pl.when(rb == 0)
    def _():
        lse_ref[...] = jnp.zeros_like(lse_ref)

    lane = lax.broadcasted_iota(_I32, (1, LANES), 1)
    head_mask = [(lane < HEAD_DIM).astype(_BF16), (lane >= HEAD_DIM).astype(_BF16)]
    lse_lane = lax.broadcasted_iota(_I32, (1, lse_ref.shape[2]), 1)

    def trip(it, carry):
        for u in range(q_unroll):
            q0 = pl.multiple_of((it * q_unroll + u) * Q_BLOCK, Q_BLOCK)
            k0 = pl.multiple_of(jnp.clip(q0 - n_side, 0, length - key_w), n_side)
            bias = bias_ref[(q0 - k0) // n_side]
            lse_rows = lse_ref[0, pl.ds(q0, Q_BLOCK), :]
            for rr in range(n_res):
                for pair in range(GROUP_W // LANES):
                    lo = rr * GROUP_W + pair * LANES
                    cols = slice(lo, lo + LANES)
                    q2 = q_ref[0, pl.ds(q0, Q_BLOCK), cols]
                    k2 = k_ref[0, pl.ds(k0, key_w), cols]
                    v2 = v_ref[0, pl.ds(k0, key_w), cols]
                    qs = jnp.concatenate([q2 * head_mask[0], q2 * head_mask[1]], axis=0)
                    s = lax.dot_general(qs, k2, (((1,), (1,)), ((), ())), preferred_element_type=_F32) + bias
                    m = jnp.max(s, axis=1, keepdims=True)
                    p = jnp.exp(s - m)
                    l = jnp.sum(p, axis=1, keepdims=True)
                    o = jnp.dot(p.astype(_BF16), v2, preferred_element_type=_F32) * (1.0 / l)
                    lse = m + jnp.log(l)
                    for hh in range(2):
                        col = (rb * n_res + rr) * HEADS_PER_GROUP + pair * 2 + hh
                        lse_rows = jnp.where(lse_lane == col, lse[hh * Q_BLOCK:(hh + 1) * Q_BLOCK], lse_rows)
                    o_ref[0, pl.ds(q0, Q_BLOCK), cols] = jnp.where(lane < HEAD_DIM, o[:Q_BLOCK], o[Q_BLOCK:]).astype(_BF16)
            lse_ref[0, pl.ds(q0, Q_BLOCK), :] = lse_rows
        return carry

    lax.fori_loop(0, n_qb // q_unroll, trip, 0)


ATTN_BLOCKS_PER_TRIP = 4


def _attention(q, k, v, window, dilation):
    B, L, _ = q.shape
    d = dilation
    n_side = window // (2 * d)
    n_res = min(d, ATTN_BLOCKS_PER_TRIP)
    q_unroll = min(ATTN_BLOCKS_PER_TRIP // n_res, L // Q_BLOCK)
    key_w = min(L, Q_BLOCK + 2 * n_side)
    spec = pl.BlockSpec((1, L, n_res * GROUP_W), lambda b, r: (b, 0, r))
    return pl.pallas_call(
        functools.partial(_attn_kernel, length=L, n_side=n_side, n_res=n_res, q_unroll=q_unroll),
        grid=(B, d // n_res),
        in_specs=[spec, spec, spec],
        out_specs=[spec, pl.BlockSpec((1, L, d * HEADS_PER_GROUP), lambda b, r: (b, 0, 0))],
        out_shape=[jax.ShapeDtypeStruct((B, L, d * GROUP_W), _BF16),
                   jax.ShapeDtypeStruct((B, L, d * HEADS_PER_GROUP), _F32)],
        scratch_shapes=[pltpu.VMEM((3, 2 * Q_BLOCK, key_w), _F32)],
        compiler_params=_params(("arbitrary", "arbitrary")),
        name=f"attention_d{d}",
    )(q, k, v)


def _route(sel):
    s = [sel[e:e + 1, :] for e in range(N_EXPERTS)]
    group_score = []
    for g in range(N_EXPERT_GROUPS):
        c0, c1, c2, c3 = s[EXPERTS_PER_GROUP * g:EXPERTS_PER_GROUP * (g + 1)]
        m1, n1 = jnp.maximum(c0, c1), jnp.minimum(c0, c1)
        m2, n2 = jnp.maximum(c2, c3), jnp.minimum(c2, c3)
        group_score.append(jnp.maximum(m1, m2) + jnp.maximum(jnp.minimum(m1, m2), jnp.maximum(n1, n2)))
    best, top_g = group_score[0], jnp.zeros(group_score[0].shape, _I32)
    for g in range(1, N_EXPERT_GROUPS):
        better = group_score[g] > best
        best = jnp.where(better, group_score[g], best)
        top_g = jnp.where(better, g, top_g)
    cs = []
    for j in range(EXPERTS_PER_GROUP):
        c = s[j]
        for g in range(1, N_EXPERT_GROUPS):
            c = jnp.where(top_g == g, s[EXPERTS_PER_GROUP * g + j], c)
        cs.append(c)
    s1, i1 = cs[0], jnp.zeros_like(top_g)
    for j in range(1, EXPERTS_PER_GROUP):
        better = cs[j] > s1
        s1 = jnp.where(better, cs[j], s1)
        i1 = jnp.where(better, j, i1)
    i2 = jnp.where(i1 == 0, 1, 0)
    s2 = jnp.where(i1 == 0, cs[1], cs[0])
    for j in range(1, EXPERTS_PER_GROUP):
        better = (cs[j] > s2) & (i1 != j)
        s2 = jnp.where(better, cs[j], s2)
        i2 = jnp.where(better, j, i2)
    lo, hi = jnp.minimum(i1, i2), jnp.maximum(i1, i2)
    pair = jnp.where(lo == 0, 0, jnp.where(lo == 1, 3, 5)) + (hi - lo - 1)
    return top_g * PAIRS_PER_GROUP + pair


def _mix_out_kernel(x_ref, o0_ref, o1_ref, o2_ref, l0_ref, l1_ref, l2_ref, u_ref, up_ref, un_ref,
                    mod_ref, wpool_ref, pscale_ref, wout_ref, gain_ref, wr_hi_ref, wr_lo_ref, rbias_ref,
                    x1_ref, h2_ref, cls_ref, ext_ref, lvl_a_ref, lvl_b_ref, lvl_c_ref, nat_o1a_ref, nat_o1b_ref,
                    nat_o2a_ref, nat_o2b_ref, nat_l1_ref, nat_l2_ref, *, seq_len):
    nat_o_refs = ((nat_o1a_ref, nat_o1b_ref), (nat_o2a_ref, nat_o2b_ref))
    nat_l_refs = (nat_l1_ref, nat_l2_ref)
    i = pl.program_id(1)
    n_i = pl.num_programs(1)
    tm = x_ref.shape[1]
    mod = mod_ref[0]

    for g, (o_ref, l_ref) in enumerate(((o0_ref, l0_ref), (o1_ref, l1_ref), (o2_ref, l2_ref))):
        dilation = DILATED_GROUPS[g][1]
        if dilation == 1:
            continue
        rows = tm // dilation
        lv = l_ref[0]
        lv = jnp.concatenate([lv, jnp.zeros((rows, LANES - lv.shape[1]), _F32)], axis=1)
        halves = nat_o_refs[g - 1]
        for r in range(dilation):
            for c, half_ref in enumerate(halves):
                lo = r * GROUP_W + c * LANES
                half_ref[pl.ds(r, rows, stride=dilation), :] = o_ref[0, :, lo:lo + LANES].astype(_F32)
            shifted = lv if r == 0 else pltpu.roll(lv, LANES - r * HEADS_PER_GROUP, axis=1)
            nat_l_refs[g - 1][pl.ds(r, rows, stride=dilation), :] = shifted

    pad = 2 * SUBLANES
    zeros_pad = jnp.zeros((SUBLANES, POOL_W), _F32)
    for ref in (ext_ref, lvl_a_ref, lvl_b_ref, lvl_c_ref):
        ref[0:SUBLANES, :] = zeros_pad
        ref[tm + pad + SUBLANES:tm + 2 * pad, :] = zeros_pad
    ext_ref[SUBLANES:pad, :] = jnp.where(i > 0, up_ref[0], 0.0)
    ext_ref[pad:pad + tm, :] = u_ref[0]
    ext_ref[pad + tm:pad + tm + SUBLANES, :] = jnp.where(i < n_i - 1, un_ref[0], 0.0)
    span = tm + 2 * SUBLANES
    win = lambda ref, off: ref[SUBLANES + off:SUBLANES + off + span, :]
    lvl_a_ref[SUBLANES:SUBLANES + span, :] = win(ext_ref, -1) + win(ext_ref, 0)
    lvl_b_ref[SUBLANES:SUBLANES + span, :] = win(lvl_a_ref, -1) + win(lvl_a_ref, 1)
    lvl_c_ref[SUBLANES:SUBLANES + span, :] = win(lvl_b_ref, -2) + win(lvl_b_ref, 2)
    lane = lax.broadcasted_iota(_I32, (1, POOL_W), 1)
    wgroup = lane // POOL_GROUP_W
    half_w = jnp.where(wgroup == 0, 1, jnp.where(wgroup == 1, 2, jnp.where(wgroup == 2, 4, 8)))

    half = tm // MIX_CHUNKS
    for hf in range(MIX_CHUNKS):
        r0 = hf * half
        rows = slice(r0, r0 + half)
        outs = [o0_ref[0, rows, :].astype(_F32)]
        lses = [l0_ref[0, rows, :]]
        for g in range(1, N_GROUPS):
            outs.append(jnp.concatenate([ref[rows, :] for ref in nat_o_refs[g - 1]], axis=1))
            lses.append(nat_l_refs[g - 1][rows, 0:HEADS_PER_GROUP])

        top = jnp.maximum(jnp.maximum(lses[0], lses[1]), lses[2])
        es = [jnp.exp(t - top) for t in lses]
        den = es[0] + es[1] + es[2]
        pieces = []
        for g in range(N_GROUPS):
            alpha = es[g] / den
            wide = jnp.concatenate([jnp.broadcast_to(alpha[:, h:h + 1], (half, HEAD_DIM))
                                    for h in range(HEADS_PER_GROUP)], axis=1)
            pieces.append((outs[g] * wide).astype(_BF16))

        s2 = lvl_a_ref[pad + r0:pad + r0 + half, :]
        s4 = lvl_b_ref[pad + r0:pad + r0 + half, :]
        s8 = lvl_c_ref[pad + r0:pad + r0 + half, :]
        s16 = lvl_c_ref[pad + r0 - 4:pad + r0 - 4 + half, :] + lvl_c_ref[pad + r0 + 4:pad + r0 + 4 + half, :]
        wsum = jnp.where(wgroup == 0, s2, jnp.where(wgroup == 1, s4, jnp.where(wgroup == 2, s8, s16)))
        pos = i * tm + r0 + lax.broadcasted_iota(_I32, (half, 1), 0)
        cnt = jnp.minimum(pos + half_w, seq_len) - jnp.maximum(pos - half_w, 0)
        z = wsum / cnt.astype(_F32) - u_ref[0, rows, :]
        pool = jnp.dot(z.astype(_BF16), wpool_ref[...], preferred_element_type=_F32) * pscale_ref[...]
        pieces.append(pool.astype(_BF16))

        mix = jnp.dot(jnp.concatenate(pieces, axis=1), wout_ref[...], preferred_element_type=_F32)
        x1 = x_ref[0, rows, :] + mod[2:3] * mix
        x1_ref[0, rows, :] = x1

        h2 = _rmsnorm_mod(x1, gain_ref[...], mod[4:5], mod[3:4])
        h_hi = h2.astype(_BF16)
        h2_ref[0, rows, :] = h_hi
        h_lo = (h2 - h_hi.astype(_F32)).astype(_BF16)
        nt = (((1,), (1,)), ((), ()))
        logits = (lax.dot_general(wr_hi_ref[...], h_hi, nt, preferred_element_type=_F32)
                  + lax.dot_general(wr_lo_ref[...], h_hi, nt, preferred_element_type=_F32)
                  + lax.dot_general(wr_hi_ref[...], h_lo, nt, preferred_element_type=_F32))
        aff = 1.0 / (1.0 + jnp.exp(-logits))
        cls_ref[0, :, rows] = _route(aff + rbias_ref[...])


def _mix_out(x, attn, lses, u, mod_l, wpool_bd, pool_scale, w_out_bf16, gain, wr_hi, wr_lo, rbias, batch_offset,
             rider=None):
    B, S, _ = x.shape
    tm = TOKEN_TILE
    n_i = S // tm
    halo_blocks = tm // SUBLANES
    tok = lambda w: pl.BlockSpec((1, tm, w), lambda b, i: (b, i, 0))
    const = lambda shape: pl.BlockSpec(shape, lambda b, i: tuple(0 for _ in shape))
    rows = tm + 4 * SUBLANES
    return _launch(
        functools.partial(_mix_out_kernel, seq_len=S),
        grid=(B, n_i),
        in_specs=[
            tok(D_MODEL),
            *[pl.BlockSpec((1, tm // d, d * GROUP_W), lambda b, i: (b, i, 0)) for _, d in DILATED_GROUPS],
            *[pl.BlockSpec((1, tm // d, d * HEADS_PER_GROUP), lambda b, i: (b, i, 0)) for _, d in DILATED_GROUPS],
            tok(POOL_W),
            pl.BlockSpec((1, SUBLANES, POOL_W), lambda b, i: (b, jnp.maximum(i * halo_blocks - 1, 0), 0)),
            pl.BlockSpec((1, SUBLANES, POOL_W),
                         lambda b, i: (b, jnp.minimum((i + 1) * halo_blocks, S // SUBLANES - 1), 0)),
            pl.BlockSpec((1, 6, D_MODEL), lambda b, i: (b + batch_offset, 0, 0)),
            const((POOL_W, POOL_W)), const((1, POOL_W)), const((D_MODEL, D_MODEL)), const((1, D_MODEL)),
            const((N_EXPERTS, D_MODEL)), const((N_EXPERTS, D_MODEL)), const((N_EXPERTS, 1)),
        ],
        out_specs=[tok(D_MODEL), tok(D_MODEL), pl.BlockSpec((1, 1, tm), lambda b, i: (b, 0, i))],
        out_shape=[jax.ShapeDtypeStruct((B, S, D_MODEL), _F32),
                   jax.ShapeDtypeStruct((B, S, D_MODEL), _BF16),
                   jax.ShapeDtypeStruct((B, 1, S), _I32)],
        scratch=[pltpu.VMEM((rows, POOL_W), _F32)] * 4 + [pltpu.VMEM((tm, LANES), _F32)] * 6,
        args=(x, *attn, *lses, u, u, u, mod_l, wpool_bd, pool_scale, w_out_bf16, gain, wr_hi, wr_lo, rbias),
        name="mix_out",
        rider=rider,
    )


TBL_EXPERT_A, TBL_EXPERT_B, TBL_TILE, TBL_USED, TBL_LAST_TILE_ROW = range(5)


def _plan_kernel(cls_ref, slot_ref, tbl_ref, cnt_ref, run_ref, base_ref, *, tile):
    phase, b, j = pl.program_id(0), pl.program_id(1), pl.program_id(2)
    first = (b == 0) & (j == 0)
    chunk = cls_ref.shape[2]
    n_tbl = tbl_ref.shape[1]
    onehot = lax.broadcasted_iota(_I32, (CLASS_ROWS, chunk), 0) == cls_ref[0]
    hits = jnp.sum(onehot.astype(_F32), axis=1, keepdims=True)

    @pl.when((phase == 0) & first)
    def _():
        cnt_ref[...] = jnp.zeros_like(cnt_ref)

    @pl.when(phase == 0)
    def _():
        cnt_ref[...] += hits

    @pl.when((phase == 1) & first)
    def _():
        padded = jnp.ceil(cnt_ref[...] * (1.0 / tile)) * tile
        r_i = lax.broadcasted_iota(_I32, (CLASS_ROWS, CLASS_ROWS), 0)
        c_i = lax.broadcasted_iota(_I32, (CLASS_ROWS, CLASS_ROWS), 1)
        padded_row = jnp.sum(jnp.where(r_i == c_i, padded, 0.0), axis=0, keepdims=True)
        base_col = jnp.sum(jnp.where(c_i < r_i, padded_row, 0.0), axis=1, keepdims=True)
        base_row = jnp.sum(jnp.where(r_i < c_i, padded, 0.0), axis=0, keepdims=True)
        total = jnp.sum(padded_row, axis=1, keepdims=True)
        base_ref[...] = base_col
        run_ref[...] = jnp.zeros_like(run_ref)

        k_row = lax.broadcasted_iota(_I32, (1, n_tbl), 1).astype(_F32)
        start = k_row * tile
        inside = (base_col <= start) & (start < base_col + padded)
        cls_id = lax.broadcasted_iota(_I32, (CLASS_ROWS, n_tbl), 0).astype(_F32)
        tile_cls = jnp.sum(jnp.where(inside, cls_id, 0.0), axis=0, keepdims=True)
        n_used = total * (1.0 / tile)
        last = jnp.maximum(n_used - 1.0, 0.0)
        last_cls = jnp.sum(jnp.where(k_row == last, tile_cls, 0.0), axis=1, keepdims=True)
        used = k_row < n_used
        tile_cls = jnp.where(used, tile_cls, last_cls)
        grp = sum(jnp.where(tile_cls >= PAIRS_PER_GROUP * g, 1.0, 0.0) for g in range(1, N_EXPERT_GROUPS))
        pair = tile_cls - grp * PAIRS_PER_GROUP
        a = jnp.where(pair < 3, 0.0, jnp.where(pair < 5, 1.0, 2.0))
        bb = jnp.where(pair < 3, pair + 1.0, jnp.where(pair < 5, pair - 1.0, 3.0))
        last_tile_row = jnp.where(padded_row > 0, base_row + padded_row - tile, -1.0)
        last_tile_row = jnp.concatenate(
            [last_tile_row, jnp.full((1, n_tbl - CLASS_ROWS), -1.0, _F32)], axis=1)
        zero = jnp.zeros((1, n_tbl), _F32)
        tbl_ref[...] = jnp.concatenate(
            [grp * EXPERTS_PER_GROUP + a, grp * EXPERTS_PER_GROUP + bb, jnp.where(used, k_row, last),
             jnp.where(used, 1.0, 0.0), last_tile_row, zero, zero, zero], axis=0).astype(_I32)

    @pl.when(phase == 1)
    def _():
        before = lax.broadcasted_iota(_I32, (chunk, chunk), 0) < lax.broadcasted_iota(_I32, (chunk, chunk), 1)
        earlier = jnp.dot(jnp.where(onehot, 1.0, 0.0).astype(_BF16), jnp.where(before, 1.0, 0.0).astype(_BF16),
                          preferred_element_type=_F32)
        dest = base_ref[...] + run_ref[...] + earlier
        slot_ref[0] = jnp.sum(jnp.where(onehot, dest, 0.0), axis=0, keepdims=True).astype(_I32)
        run_ref[...] += hits


def _moe_plan(cls, tile):
    B, _, S = cls.shape
    chunk = PLAN_CHUNK
    n_tiles = (B * S) // tile + N_CLASSES
    n_tbl = -(-n_tiles // LANES) * LANES
    slot, tbl = pl.pallas_call(
        functools.partial(_plan_kernel, tile=tile),
        grid=(2, B, S // chunk),
        in_specs=[pl.BlockSpec((1, 1, chunk), lambda p, b, j: (b, 0, j))],
        out_specs=[pl.BlockSpec((1, 1, chunk), lambda p, b, j: (b * p, 0, j * p)),
                   pl.BlockSpec((SUBLANES, n_tbl), lambda p, b, j: (0, 0))],
        out_shape=[jax.ShapeDtypeStruct((B, 1, S), _I32), jax.ShapeDtypeStruct((SUBLANES, n_tbl), _I32)],
        scratch_shapes=[pltpu.VMEM((CLASS_ROWS, 1), _F32)] * 3,
        compiler_params=_params(("arbitrary", "arbitrary", "arbitrary")),
        name="moe_plan",
    )(cls)
    return slot, tbl, n_tiles


def _row_copy(src, src_row, dst, dst_row, sem):
    return pltpu.make_async_copy(src.at[pl.ds(pl.multiple_of(src_row * ROW_CHUNKS, ROW_CHUNKS), ROW_CHUNKS), :],
                                 dst.at[pl.ds(pl.multiple_of(dst_row * ROW_CHUNKS, ROW_CHUNKS), ROW_CHUNKS), :],
                                 sem)


def _dispatch_start(tbl_ref, slot_ref, h_ref, hs_ref, stage_ref, zero_ref, sem, zsem, *, tile):
    first = (pl.program_id(0) == 0) & (pl.program_id(1) == 0)
    tm = h_ref.shape[1]
    tile_rows = tile * ROW_CHUNKS

    @pl.when(first)
    def _():
        zero_ref[...] = jnp.zeros_like(zero_ref)
        n_tiles = hs_ref.shape[0] // tile_rows
        for wait in (False, True):
            for c in range(N_CLASSES):
                row = tbl_ref[TBL_LAST_TILE_ROW, c]

                @pl.when(row >= 0)
                def _():
                    at = pl.multiple_of(row * ROW_CHUNKS, ROW_CHUNKS)
                    cp = pltpu.make_async_copy(zero_ref, hs_ref.at[pl.ds(at, tile_rows), :], zsem)
                    cp.wait() if wait else cp.start()

            for k in range(n_tiles - N_CLASSES, n_tiles):
                @pl.when(tbl_ref[TBL_USED, k] == 0)
                def _():
                    cp = pltpu.make_async_copy(zero_ref, hs_ref.at[pl.ds(k * tile_rows, tile_rows), :], zsem)
                    cp.wait() if wait else cp.start()

    h = h_ref[0]
    for c in range(ROW_CHUNKS):
        stage_ref[pl.ds(c, tm, stride=ROW_CHUNKS), :] = h[:, c * LANES:(c + 1) * LANES].astype(_F32)

    def issue(g, carry):
        for k in range(SUBLANES):
            r = g * SUBLANES + k
            _row_copy(stage_ref, r, hs_ref, slot_ref[0, 0, r], sem).start(priority=k % 2)
        return carry

    lax.fori_loop(0, tm // SUBLANES, issue, 0)


def _dispatch_wait(stage_ref, hs_ref, sem):
    pltpu.make_async_copy(stage_ref, hs_ref.at[pl.ds(0, stage_ref.shape[0]), :], sem).wait()


def _launch(kernel_fn, grid, in_specs, out_specs, out_shape, scratch, args, name, rider=None):
    params = _params(("arbitrary",) * len(grid))
    if rider is None:
        return pl.pallas_call(kernel_fn, grid=grid, in_specs=in_specs, out_specs=out_specs, out_shape=out_shape,
                              scratch_shapes=scratch, compiler_params=params, name=name)(*args)
    tbl, slot, h2, n_tiles = rider
    tile = MOE_TILE
    steps = grid[0] * grid[1]
    Bo, So, _ = h2.shape
    share = (Bo * So) // steps
    per_seq = So // share
    n_in, n_out, n_scr = len(in_specs), len(out_specs), len(scratch)

    def drop_tbl(spec):
        if spec.index_map is None:
            return spec
        return pl.BlockSpec(spec.block_shape, lambda *a, f=spec.index_map: f(*a[:-1]), memory_space=spec.memory_space)

    def kern(tbl_ref, *refs):
        host_in, (slot_ref, h_ref) = refs[:n_in], refs[n_in:n_in + 2]
        host_out, hs_ref = refs[n_in + 2:n_in + 2 + n_out], refs[n_in + 2 + n_out]
        host_scr = refs[n_in + 3 + n_out:n_in + 3 + n_out + n_scr]
        stage_ref, zero_ref, sem, zsem = refs[n_in + 3 + n_out + n_scr:]
        _dispatch_start(tbl_ref, slot_ref, h_ref, hs_ref, stage_ref, zero_ref, sem, zsem, tile=tile)
        kernel_fn(*host_in, *host_out, *host_scr)
        _dispatch_wait(stage_ref, hs_ref, sem)

    step_of = lambda b, i: b * grid[1] + i
    rider_in = [pl.BlockSpec((1, 1, share), lambda b, i, t: (step_of(b, i) // per_seq, 0, step_of(b, i) % per_seq),
                             memory_space=pltpu.SMEM),
                pl.BlockSpec((1, share, D_MODEL), lambda b, i, t: (step_of(b, i) // per_seq, step_of(b, i) % per_seq, 0))]
    return pl.pallas_call(
        kern,
        grid_spec=pltpu.PrefetchScalarGridSpec(
            num_scalar_prefetch=1,
            grid=grid,
            in_specs=[drop_tbl(s) for s in in_specs] + rider_in,
            out_specs=[drop_tbl(s) for s in out_specs] + [pl.BlockSpec(memory_space=pl.ANY)],
            scratch_shapes=list(scratch) + [pltpu.VMEM((share * ROW_CHUNKS, LANES), _F32),
                                            pltpu.VMEM((tile * ROW_CHUNKS, LANES), _F32),
                                            pltpu.SemaphoreType.DMA(()), pltpu.SemaphoreType.DMA(())],
        ),
        out_shape=list(out_shape) + [jax.ShapeDtypeStruct((n_tiles * tile * ROW_CHUNKS, LANES), _F32)],
        compiler_params=params,
        name=name + "_dispatch",
    )(tbl, *args, slot, h2)


EXPERT_TILE_INPUTS = 11


def _experts_kernel(tbl_ref, *refs):
    n = EXPERT_TILE_INPUTS
    ins = (refs[:n], refs[n:2 * n])
    ys_ref = refs[2 * n]
    rows = ys_ref.shape[0] // 2
    k = pl.program_id(0)
    used = [tbl_ref[TBL_USED, 2 * k + j] for j in range(2)]

    def idle(j):
        ys_ref[j * rows:(j + 1) * rows, :] = jnp.zeros((rows, LANES), _F32)

    @pl.when(used[1] == 1)
    def _():
        for j in range(2):
            _expert_tile(*ins[j], ys_ref, j * rows)

    @pl.when((used[0] == 1) & (used[1] == 0))
    def _():
        _expert_tile(*ins[0], ys_ref, 0)
        idle(1)

    @pl.when(used[0] == 0)
    def _():
        idle(0)
        idle(1)


def _expert_tile(hs_ref, wra_hi_ref, wra_lo_ref, wrb_hi_ref, wrb_lo_ref,
                 wga_ref, wua_ref, wda_ref, wgb_ref, wub_ref, wdb_ref, ys_ref, row0):
    tile = hs_ref.shape[0] // ROW_CHUNKS
    x = jnp.concatenate([hs_ref[pl.ds(c, tile, stride=ROW_CHUNKS), :].astype(_BF16)
                         for c in range(ROW_CHUNKS)], axis=1)
    row_id = lax.broadcasted_iota(_I32, (2 * SUBLANES, D_MODEL), 0)
    rows = jnp.zeros((2 * SUBLANES, D_MODEL), _F32)
    for n, ref in enumerate((wra_hi_ref, wra_lo_ref, wrb_hi_ref, wrb_lo_ref)):
        rows = jnp.where(row_id == n, ref[0], rows)
    lg = lax.dot_general(x, rows.astype(_BF16), (((1,), (1,)), ((), ())),
                         preferred_element_type=_F32)
    aff_a = 1.0 / (1.0 + jnp.exp(-(lg[:, 0:1] + lg[:, 1:2])))
    aff_b = 1.0 / (1.0 + jnp.exp(-(lg[:, 2:3] + lg[:, 3:4])))
    den = aff_a + aff_b

    def hidden(wg_ref, wu_ref, gate):
        a = jnp.dot(x, wg_ref[0], preferred_element_type=_F32)
        b = jnp.dot(x, wu_ref[0], preferred_element_type=_F32)
        return ((a / (1.0 + jnp.exp(-a))) * b * gate).astype(_BF16)

    y = (jnp.dot(hidden(wga_ref, wua_ref, aff_a / den), wda_ref[0], preferred_element_type=_F32)
         + jnp.dot(hidden(wgb_ref, wub_ref, aff_b / den), wdb_ref[0], preferred_element_type=_F32))
    for c in range(ROW_CHUNKS):
        ys_ref[pl.ds(row0 + c, tile, stride=ROW_CHUNKS), :] = y[:, c * LANES:(c + 1) * LANES]


def _moe_experts(tbl, hs, wr_hi, wr_lo, wg, wu, wd, layer, n_tiles, tile):
    rows = tile * ROW_CHUNKS
    first = layer * N_EXPERTS
    a, b = TBL_EXPERT_A, TBL_EXPERT_B
    in_specs = []
    for j in range(2):
        router = lambda row, j=j: pl.BlockSpec((1, 1, D_MODEL), lambda k, tbl: (tbl[row, 2 * k + j], 0, 0))
        up = lambda row, j=j: pl.BlockSpec((1, D_MODEL, D_EXPERT),
                                           lambda k, tbl: (first + tbl[row, 2 * k + j], 0, 0))
        down = lambda row, j=j: pl.BlockSpec((1, D_EXPERT, D_MODEL),
                                             lambda k, tbl: (first + tbl[row, 2 * k + j], 0, 0))
        in_specs += [pl.BlockSpec((rows, LANES), lambda k, tbl, j=j: (tbl[TBL_TILE, 2 * k + j], 0)),
                     router(a), router(a), router(b), router(b), up(a), up(a), down(a), up(b), up(b), down(b)]
    tile_args = (hs, wr_hi, wr_lo, wr_hi, wr_lo, wg, wu, wd, wg, wu, wd)
    return pl.pallas_call(
        _experts_kernel,
        grid_spec=pltpu.PrefetchScalarGridSpec(
            num_scalar_prefetch=1,
            grid=(n_tiles // 2,),
            in_specs=in_specs,
            out_specs=pl.BlockSpec((2 * rows, LANES), lambda k, tbl: (k, 0)),
        ),
        out_shape=jax.ShapeDtypeStruct(hs.shape, _F32),
        compiler_params=_params(("arbitrary",)),
        name="moe_experts",
    )(tbl, *tile_args, *tile_args)


def _combine_kernel(slot_ref, slot_next_ref, x1_ref, mod_ref, ys_ref, *refs, project):
    if project:
        (modn_ref, gain_ref, w_ref, cos_ref, sin_ref, x2_ref, *out_refs) = refs[:16]
        buf_refs, sems, stage_ref = refs[16:18], refs[18:20], refs[20]
    else:
        gfin_ref, x2_ref = refs[:2]
        buf_refs, sems = refs[2:4], refs[4:6]
    tm = x1_ref.shape[1]
    half = tm // 2
    step = pl.program_id(0) * pl.num_programs(1) + pl.program_id(1)
    last = pl.num_programs(0) * pl.num_programs(1) - 1
    gate = mod_ref[0][5:6]

    def gather(sref, row0, n):
        def issue(g, carry):
            for k in range(SUBLANES):
                r = g * SUBLANES + k
                _row_copy(ys_ref, sref[0, 0, row0 + r], buf_refs[n], r, sems[n]).start(priority=k % 2)
            return carry
        lax.fori_loop(0, half // SUBLANES, issue, 0)

    def residual(n):
        pltpu.make_async_copy(ys_ref.at[pl.ds(0, half * ROW_CHUNKS), :], buf_refs[n], sems[n]).wait()
        rows = slice(n * half, (n + 1) * half)
        for c in range(ROW_CHUNKS):
            cols = slice(c * LANES, (c + 1) * LANES)
            x2_ref[0, rows, cols] = x1_ref[0, rows, cols] + gate[:, cols] * buf_refs[n][pl.ds(c, half, stride=ROW_CHUNKS), :]

    def tail(n):
        rows = slice(n * half, (n + 1) * half)
        x2 = x2_ref[0, rows, :]
        if project:
            _project(x2, n * half, modn_ref[0], gain_ref, w_ref, cos_ref, sin_ref, out_refs[0:3], out_refs[3:6],
                     out_refs[6:9], out_refs[9], stage_ref)
        else:
            x2_ref[0, rows, :] = x2 * lax.rsqrt(jnp.mean(x2 * x2, axis=-1, keepdims=True) + RMS_EPS) * gfin_ref[...]

    @pl.when(step == 0)
    def _():
        gather(slot_ref, 0, 0)

    gather(slot_ref, half, 1)
    residual(0)

    @pl.when(step < last)
    def _():
        gather(slot_next_ref, 0, 0)

    tail(0)
    residual(1)
    tail(1)


def _moe_combine(slot, x1, mod_prev, ys, batch_offset, tail_args, rider=None):
    B, S, _ = x1.shape
    tm = TOKEN_TILE
    half = tm // 2
    n_i = S // tm
    project = len(tail_args) > 1

    def next_step(b, i):
        n = jnp.minimum(b * n_i + i + 1, B * n_i - 1)
        return n // n_i, 0, n % n_i

    tok = pl.BlockSpec((1, tm, D_MODEL), lambda b, i: (b, i, 0))
    mod_spec = pl.BlockSpec((1, 6, D_MODEL), lambda b, i: (b + batch_offset, 0, 0))
    in_specs = [pl.BlockSpec((1, 1, tm), lambda b, i: (b, 0, i), memory_space=pltpu.SMEM),
                pl.BlockSpec((1, 1, tm), next_step, memory_space=pltpu.SMEM),
                tok, mod_spec, pl.BlockSpec(memory_space=pl.ANY)]
    out_specs, out_shape = [tok], [jax.ShapeDtypeStruct((B, S, D_MODEL), _F32)]
    scratch = [pltpu.VMEM((half * ROW_CHUNKS, LANES), _F32)] * 2 + [pltpu.SemaphoreType.DMA(())] * 2
    if project:
        in_specs += [mod_spec,
                     pl.BlockSpec((1, D_MODEL), lambda b, i: (0, 0)),
                     pl.BlockSpec((D_MODEL, IN_W), lambda b, i: (0, 0)),
                     pl.BlockSpec((tm, LANES), lambda b, i: (i, 0)),
                     pl.BlockSpec((tm, LANES), lambda b, i: (i, 0))]
        out_specs += [pl.BlockSpec((1, tm // d, d * GROUP_W), lambda b, i: (b, i, 0)) for _, d in DILATED_GROUPS] * 3
        out_specs += [pl.BlockSpec((1, tm, POOL_W), lambda b, i: (b, i, 0))]
        out_shape += [jax.ShapeDtypeStruct((B, S // d, d * GROUP_W), _BF16) for _, d in DILATED_GROUPS] * 3
        out_shape += [jax.ShapeDtypeStruct((B, S, POOL_W), _F32)]
        scratch += [pltpu.VMEM((half, LANES), _F32)]
    else:
        in_specs += [pl.BlockSpec((1, D_MODEL), lambda b, i: (0, 0))]
    return _launch(
        functools.partial(_combine_kernel, project=project),
        grid=(B, n_i),
        in_specs=in_specs,
        out_specs=out_specs,
        out_shape=out_shape,
        scratch=scratch,
        args=(slot, slot, x1, mod_prev, ys, *tail_args),
        name="moe_combine_in_proj" if project else "moe_combine_norm",
        rider=rider,
    )


def _rope_tables(S):
    inv = 1.0 / (ROPE_THETA ** (jnp.arange(0, HEAD_DIM, 2, dtype=_F32) / HEAD_DIM))
    ang = jnp.arange(S, dtype=_F32)[:, None] * inv[None, :]
    cos, sin = jnp.cos(ang), jnp.sin(ang)
    reps = LANES // HEAD_DIM
    return (jnp.tile(jnp.concatenate([cos, cos], axis=1), (1, reps)),
            jnp.tile(jnp.concatenate([-sin, sin], axis=1), (1, reps)))


def _trunks(xs, batch_offsets, mod, w):
    tables = [_rope_tables(x.shape[1]) for x in xs]
    outs = [_in_proj(x, mod[0], w["norm_mix"][0], w["w_in"][0], *tables[t], batch_offsets[t])
            for t, x in enumerate(xs)]
    xs = list(xs)
    for l in range(DEPTH):
        mod_l = mod[l]

        def mix(t, rider):
            q, k, v, u = outs[t][0:3], outs[t][3:6], outs[t][6:9], outs[t][9]
            attn, lses = [], []
            for g, (window, dilation) in enumerate(DILATED_GROUPS):
                o, lse = _attention(q[g], k[g], v[g], window, dilation)
                attn.append(o)
                lses.append(lse)
            return _mix_out(xs[t], attn, lses, u, mod_l, w["wpool_bd"][l], w["pool_scale"][l], w["w_out"][l],
                            w["norm_ffn"][l], w["wr_hi"], w["wr_lo"], w["rbias"], batch_offsets[t], rider)

        def experts(tbl, hs, n_tiles):
            return _moe_experts(tbl, hs, w["wr_hi3"], w["wr_lo3"], w["w_gate"], w["w_up"], w["w_down"], l,
                                n_tiles, MOE_TILE)

        def combine(t, slot, x1, ys, rider):
            if l + 1 < DEPTH:
                tail = (mod[l + 1], w["norm_mix"][l + 1], w["w_in"][l + 1], *tables[t])
            else:
                tail = (w["norm_final"],)
            return _moe_combine(slot, x1, mod_l, ys, batch_offsets[t], tail, rider)

        x1_0, h2_0, cls_0 = mix(0, None)
        slot_0, tbl_0, n_0 = _moe_plan(cls_0, MOE_TILE)
        x1_1, h2_1, cls_1, hs_0 = mix(1, (tbl_0, slot_0, h2_0, n_0))
        slot_1, tbl_1, n_1 = _moe_plan(cls_1, MOE_TILE)
        ys_0 = experts(tbl_0, hs_0, n_0)
        xs[0], *rest = combine(0, slot_0, x1_0, ys_0, (tbl_1, slot_1, h2_1, n_1))
        outs[0], hs_1 = rest[:-1], rest[-1]
        ys_1 = experts(tbl_1, hs_1, n_1)
        xs[1], *outs[1] = combine(1, slot_1, x1_1, ys_1, None)
    return xs


def kernel(x_prompt, x_sample, c_prompt, c_sample, norm_mix, w_mod, b_mod, w_in, w_pool, pool_scale, w_out,
           norm_ffn, w_router, router_bias, w_gate, w_up, w_down, norm_final):
    n_prompt = x_prompt.shape[0]
    c = jnp.concatenate([c_prompt, c_sample], axis=0)
    mod = _modulation(c, w_mod, b_mod).reshape(DEPTH, c.shape[0], 6, D_MODEL)
    n_pool = len(POOL_WINDOWS)
    eye = jnp.eye(n_pool, dtype=w_pool.dtype)
    wpool_bd = (w_pool[:, :, :, None, :] * eye[None, :, None, :, None]).reshape(DEPTH, POOL_W, POOL_W)
    wr_t = w_router.T.astype(_F32)
    wr_hi = wr_t.astype(_BF16)
    wr_lo = (wr_t - wr_hi.astype(_F32)).astype(_BF16)
    w = {
        "norm_mix": norm_mix.reshape(DEPTH, 1, D_MODEL),
        "norm_ffn": norm_ffn.reshape(DEPTH, 1, D_MODEL),
        "norm_final": norm_final.reshape(1, D_MODEL),
        "w_in": w_in.astype(_BF16),
        "w_out": w_out.astype(_BF16),
        "wpool_bd": wpool_bd.astype(_BF16),
        "pool_scale": pool_scale.reshape(DEPTH, 1, POOL_W),
        "wr_hi": wr_hi,
        "wr_lo": wr_lo,
        "wr_hi3": wr_hi.astype(_F32).reshape(N_EXPERTS, 1, D_MODEL),
        "wr_lo3": wr_lo.astype(_F32).reshape(N_EXPERTS, 1, D_MODEL),
        "rbias": router_bias.astype(_F32).reshape(N_EXPERTS, 1),
        "w_gate": w_gate.astype(_BF16).reshape(DEPTH * N_EXPERTS, D_MODEL, D_EXPERT),
        "w_up": w_up.astype(_BF16).reshape(DEPTH * N_EXPERTS, D_MODEL, D_EXPERT),
        "w_down": w_down.astype(_BF16).reshape(DEPTH * N_EXPERTS, D_EXPERT, D_MODEL),
    }
    y_prompt, y_sample = _trunks((x_prompt, x_sample), (0, n_prompt), mod, w)
    return (y_prompt, y_sample)
```

```python
import functools

import jax
import jax.numpy as jnp
from jax import lax
from jax.experimental import pallas as pl
from jax.experimental.pallas import tpu as pltpu

D_MODEL = 1024
DEPTH = 4
HEAD_DIM = 64
HALF_HEAD = HEAD_DIM // 2
DILATED_GROUPS = ((128, 1), (512, 4), (2048, 16))
N_GROUPS = len(DILATED_GROUPS)
HEADS_PER_GROUP = 4
GROUP_W = HEADS_PER_GROUP * HEAD_DIM
ATT_W = N_GROUPS * GROUP_W
POOL_WINDOWS = (2, 4, 8, 16)
POOL_GROUP_W = 64
POOL_W = POOL_GROUP_W * len(POOL_WINDOWS)
IN_W = 3 * ATT_W + POOL_W
ROPE_THETA = 10000.0
RMS_EPS = 1e-6
N_EXPERTS = 16
N_EXPERT_GROUPS = 4
EXPERTS_PER_GROUP = N_EXPERTS // N_EXPERT_GROUPS
PAIRS_PER_GROUP = EXPERTS_PER_GROUP * (EXPERTS_PER_GROUP - 1) // 2
N_CLASSES = N_EXPERT_GROUPS * PAIRS_PER_GROUP
D_EXPERT = 512

LANES = 128
SUBLANES = 8
Q_BLOCK = 128
VMEM_LIMIT = 48 * 1024 * 1024

TOKEN_TILE = 512
MOE_TILE = 256
MIX_TILE = 1024
MIX_CHUNKS = 2
PLAN_CHUNK = 512
CLASS_ROWS = 32
ROW_CHUNKS = D_MODEL // LANES
ROWS_PER_TRIP = 32

_BF16 = jnp.bfloat16
_F32 = jnp.float32
_I32 = jnp.int32


def _params(semantics):
    return pltpu.CompilerParams(dimension_semantics=semantics, vmem_limit_bytes=VMEM_LIMIT)


def _mod_kernel(c_ref, w_ref, b_ref, o_ref):
    c = c_ref[...]
    sc = c / (1.0 + jnp.exp(-c))
    o_ref[0] = jnp.dot(sc, w_ref[0], preferred_element_type=_F32,
                       precision=lax.Precision.HIGHEST) + b_ref[0]


def _modulation(c, w_mod, b_mod):
    nb = c.shape[0]
    col = D_MODEL
    n_col = w_mod.shape[2] // col
    return pl.pallas_call(
        _mod_kernel,
        grid=(DEPTH, n_col),
        in_specs=[
            pl.BlockSpec((nb, D_MODEL), lambda l, j: (0, 0)),
            pl.BlockSpec((1, D_MODEL, col), lambda l, j: (l, 0, j)),
            pl.BlockSpec((1, 1, col), lambda l, j: (l, 0, j)),
        ],
        out_specs=pl.BlockSpec((1, nb, col), lambda l, j: (l, 0, j)),
        out_shape=jax.ShapeDtypeStruct((DEPTH, nb, w_mod.shape[2]), _F32),
        compiler_params=_params(("arbitrary", "arbitrary")),
        name="modulation",
    )(c, w_mod, b_mod.reshape(DEPTH, 1, -1))


def _rmsnorm_mod(x, gain, scale, shift):
    y = x * lax.rsqrt(jnp.mean(x * x, axis=-1, keepdims=True) + RMS_EPS)
    return (y * gain) * (1.0 + scale) + shift


def _rope_chunk(t, cos, sin_signed, first_half):
    fwd = pltpu.roll(t, HALF_HEAD, axis=1)
    bwd = pltpu.roll(t, LANES - HALF_HEAD, axis=1)
    return t * cos + jnp.where(first_half, bwd, fwd) * sin_signed


def _store_by_residue(out_ref, chunk, value, dilation, stage_ref, row0):
    rows = value.shape[0] // dilation
    at = slice(row0 // dilation, row0 // dilation + rows)
    if dilation == 1:
        out_ref[0, at, chunk * LANES:(chunk + 1) * LANES] = value.astype(_BF16)
        return
    stage_ref[...] = value
    for r in range(dilation):
        lo = r * GROUP_W + chunk * LANES
        out_ref[0, at, lo:lo + LANES] = stage_ref[pl.ds(r, rows, stride=dilation), :].astype(_BF16)


def _in_proj_kernel(x_ref, mod_ref, gain_ref, w_ref, cos_ref, sin_ref, *refs):
    _project(x_ref[0], 0, mod_ref[0], gain_ref, w_ref, cos_ref, sin_ref, refs[0:3], refs[3:6], refs[6:9], refs[9],
             refs[10])


def _project(x, row0, mod, gain_ref, w_ref, cos_ref, sin_ref, q_refs, k_refs, v_refs, u_ref, stage_ref):
    n = x.shape[0]
    h = _rmsnorm_mod(x, gain_ref[...], mod[1:2], mod[0:1])
    proj = jnp.dot(h.astype(_BF16), w_ref[...], preferred_element_type=_F32)
    cos = cos_ref[row0:row0 + n, :]
    sin_signed = sin_ref[row0:row0 + n, :]
    lane = lax.broadcasted_iota(_I32, (1, LANES), 1)
    first_half = (lane % HEAD_DIM) < HALF_HEAD
    q_scale = HEAD_DIM ** -0.5
    for g, (_, dilation) in enumerate(DILATED_GROUPS):
        for c in range(GROUP_W // LANES):
            lo = g * GROUP_W + c * LANES
            q = _rope_chunk(proj[:, lo:lo + LANES], cos, sin_signed, first_half)
            _store_by_residue(q_refs[g], c, q * q_scale, dilation, stage_ref, row0)
            k = _rope_chunk(proj[:, ATT_W + lo:ATT_W + lo + LANES], cos, sin_signed, first_half)
            _store_by_residue(k_refs[g], c, k, dilation, stage_ref, row0)
            _store_by_residue(v_refs[g], c, proj[:, 2 * ATT_W + lo:2 * ATT_W + lo + LANES], dilation, stage_ref,
                              row0)
    u_ref[0, row0:row0 + n, :] = proj[:, 3 * ATT_W:]


def _in_proj(x, mod_l, gain, w_in_bf16, cos, sin_signed, batch_offset):
    B, S, _ = x.shape
    tm = TOKEN_TILE
    grp = [jax.ShapeDtypeStruct((B, S // d, d * GROUP_W), _BF16) for _, d in DILATED_GROUPS]
    grp_spec = [pl.BlockSpec((1, tm // d, d * GROUP_W), lambda b, i: (b, i, 0)) for _, d in DILATED_GROUPS]
    return pl.pallas_call(
        _in_proj_kernel,
        grid=(B, S // tm),
        in_specs=[
            pl.BlockSpec((1, tm, D_MODEL), lambda b, i: (b, i, 0)),
            pl.BlockSpec((1, 6, D_MODEL), lambda b, i: (b + batch_offset, 0, 0)),
            pl.BlockSpec((1, D_MODEL), lambda b, i: (0, 0)),
            pl.BlockSpec((D_MODEL, IN_W), lambda b, i: (0, 0)),
            pl.BlockSpec((tm, LANES), lambda b, i: (i, 0)),
            pl.BlockSpec((tm, LANES), lambda b, i: (i, 0)),
        ],
        out_specs=grp_spec * 3 + [pl.BlockSpec((1, tm, POOL_W), lambda b, i: (b, i, 0))],
        out_shape=grp * 3 + [jax.ShapeDtypeStruct((B, S, POOL_W), _F32)],
        scratch_shapes=[pltpu.VMEM((tm, LANES), _F32)],
        compiler_params=_params(("arbitrary", "arbitrary")),
        name="in_proj",
    )(x, mod_l, gain, w_in_bf16, cos, sin_signed)


def _attn_kernel(q_ref, k_ref, v_ref, o_ref, lse_ref, bias_ref, *, length, n_side, n_res, q_unroll):
    rb = pl.program_id(1)
    key_w = min(length, Q_BLOCK + 2 * n_side)
    n_qb = length // Q_BLOCK

    @pl.when((pl.program_id(0) == 0) & (rb == 0))
    def _():
        i = lax.broadcasted_iota(_I32, (2 * Q_BLOCK, key_w), 0) % Q_BLOCK
        j = lax.broadcasted_iota(_I32, (2 * Q_BLOCK, key_w), 1)
        for n in range(3):
            bias_ref[n] = jnp.where(jnp.abs(i - j + n * n_side) <= n_side, 0.0, -jnp.inf)

    @pl.when(rb == 0)
    def _():
        lse_ref[...] = jnp.zeros_like(lse_ref)

    lane = lax.broadcasted_iota(_I32, (1, LANES), 1)
    head_mask = [(lane < HEAD_DIM).astype(_BF16), (lane >= HEAD_DIM).astype(_BF16)]
    lse_lane = lax.broadcasted_iota(_I32, (1, lse_ref.shape[2]), 1)

    def trip(it, carry):
        for u in range(q_unroll):
            q0 = pl.multiple_of((it * q_unroll + u) * Q_BLOCK, Q_BLOCK)
            k0 = pl.multiple_of(jnp.clip(q0 - n_side, 0, length - key_w), n_side)
            bias = bias_ref[(q0 - k0) // n_side]
            lse_rows = lse_ref[0, pl.ds(q0, Q_BLOCK), :]
            for rr in range(n_res):
                for pair in range(GROUP_W // LANES):
                    lo = rr * GROUP_W + pair * LANES
                    cols = slice(lo, lo + LANES)
                    q2 = q_ref[0, pl.ds(q0, Q_BLOCK), cols]
                    k2 = k_ref[0, pl.ds(k0, key_w), cols]
                    v2 = v_ref[0, pl.ds(k0, key_w), cols]
                    qs = jnp.concatenate([q2 * head_mask[0], q2 * head_mask[1]], axis=0)
                    s = lax.dot_general(qs, k2, (((1,), (1,)), ((), ())), preferred_element_type=_F32) + bias
                    m = jnp.max(s, axis=1, keepdims=True)
                    p = jnp.exp(s - m)
                    l = jnp.sum(p, axis=1, keepdims=True)
                    o = jnp.dot(p.astype(_BF16), v2, preferred_element_type=_F32) * (1.0 / l)
                    lse = m + jnp.log(l)
                    for hh in range(2):
                        col = (rb * n_res + rr) * HEADS_PER_GROUP + pair * 2 + hh
                        lse_rows = jnp.where(lse_lane == col, lse[hh * Q_BLOCK:(hh + 1) * Q_BLOCK], lse_rows)
                    o_ref[0, pl.ds(q0, Q_BLOCK), cols] = jnp.where(lane < HEAD_DIM, o[:Q_BLOCK], o[Q_BLOCK:]).astype(_BF16)
            lse_ref[0, pl.ds(q0, Q_BLOCK), :] = lse_rows
        return carry

    lax.fori_loop(0, n_qb // q_unroll, trip, 0)


ATTN_BLOCKS_PER_TRIP = 4


def _attention(q, k, v, window, dilation):
    B, L, _ = q.shape
    d = dilation
    n_side = window // (2 * d)
    n_res = min(d, ATTN_BLOCKS_PER_TRIP)
    q_unroll = min(ATTN_BLOCKS_PER_TRIP // n_res, L // Q_BLOCK)
    key_w = min(L, Q_BLOCK + 2 * n_side)
    spec = pl.BlockSpec((1, L, n_res * GROUP_W), lambda b, r: (b, 0, r))
    return pl.pallas_call(
        functools.partial(_attn_kernel, length=L, n_side=n_side, n_res=n_res, q_unroll=q_unroll),
        grid=(B, d // n_res),
        in_specs=[spec, spec, spec],
        out_specs=[spec, pl.BlockSpec((1, L, d * HEADS_PER_GROUP), lambda b, r: (b, 0, 0))],
        out_shape=[jax.ShapeDtypeStruct((B, L, d * GROUP_W), _BF16),
                   jax.ShapeDtypeStruct((B, L, d * HEADS_PER_GROUP), _F32)],
        scratch_shapes=[pltpu.VMEM((3, 2 * Q_BLOCK, key_w), _F32)],
        compiler_params=_params(("arbitrary", "arbitrary")),
        name=f"attention_d{d}",
    )(q, k, v)


def _route(sel):
    s = [sel[e:e + 1, :] for e in range(N_EXPERTS)]
    group_score = []
    for g in range(N_EXPERT_GROUPS):
        c0, c1, c2, c3 = s[EXPERTS_PER_GROUP * g:EXPERTS_PER_GROUP * (g + 1)]
        m1, n1 = jnp.maximum(c0, c1), jnp.minimum(c0, c1)
        m2, n2 = jnp.maximum(c2, c3), jnp.minimum(c2, c3)
        group_score.append(jnp.maximum(m1, m2) + jnp.maximum(jnp.minimum(m1, m2), jnp.maximum(n1, n2)))
    best, top_g = group_score[0], jnp.zeros(group_score[0].shape, _I32)
    for g in range(1, N_EXPERT_GROUPS):
        better = group_score[g] > best
        best = jnp.where(better, group_score[g], best)
        top_g = jnp.where(better, g, top_g)
    cs = []
    for j in range(EXPERTS_PER_GROUP):
        c = s[j]
        for g in range(1, N_EXPERT_GROUPS):
            c = jnp.where(top_g == g, s[EXPERTS_PER_GROUP * g + j], c)
        cs.append(c)
    s1, i1 = cs[0], jnp.zeros_like(top_g)
    for j in range(1, EXPERTS_PER_GROUP):
        better = cs[j] > s1
        s1 = jnp.where(better, cs[j], s1)
        i1 = jnp.where(better, j, i1)
    i2 = jnp.where(i1 == 0, 1, 0)
    s2 = jnp.where(i1 == 0, cs[1], cs[0])
    for j in range(1, EXPERTS_PER_GROUP):
        better = (cs[j] > s2) & (i1 != j)
        s2 = jnp.where(better, cs[j], s2)
        i2 = jnp.where(better, j, i2)
    lo, hi = jnp.minimum(i1, i2), jnp.maximum(i1, i2)
    pair = jnp.where(lo == 0, 0, jnp.where(lo == 1, 3, 5)) + (hi - lo - 1)
    return top_g * PAIRS_PER_GROUP + pair


def _mix_out_kernel(x_ref, o0_ref, o1_ref, o2_ref, l0_ref, l1_ref, l2_ref, u_ref, up_ref, un_ref,
                    mod_ref, wpool_ref, pscale_ref, wout_ref, gain_ref, wr_hi_ref, wr_lo_ref, rbias_ref,
                    x1_ref, h2_ref, cls_ref, ext_ref, lvl_a_ref, lvl_b_ref, lvl_c_ref, nat_o1a_ref, nat_o1b_ref,
                    nat_o2a_ref, nat_o2b_ref, nat_l1_ref, nat_l2_ref, *, seq_len):
    nat_o_refs = ((nat_o1a_ref, nat_o1b_ref), (nat_o2a_ref, nat_o2b_ref))
    nat_l_refs = (nat_l1_ref, nat_l2_ref)
    i = pl.program_id(1)
    n_i = pl.num_programs(1)
    tm = x_ref.shape[1]
    mod = mod_ref[0]

    for g, (o_ref, l_ref) in enumerate(((o0_ref, l0_ref), (o1_ref, l1_ref), (o2_ref, l2_ref))):
        dilation = DILATED_GROUPS[g][1]
        if dilation == 1:
            continue
        rows = tm // dilation
        lv = l_ref[0]
        lv = jnp.concatenate([lv, jnp.zeros((rows, LANES - lv.shape[1]), _F32)], axis=1)
        halves = nat_o_refs[g - 1]
        for r in range(dilation):
            for c, half_ref in enumerate(halves):
                lo = r * GROUP_W + c * LANES
                half_ref[pl.ds(r, rows, stride=dilation), :] = o_ref[0, :, lo:lo + LANES].astype(_F32)
            shifted = lv if r == 0 else pltpu.roll(lv, LANES - r * HEADS_PER_GROUP, axis=1)
            nat_l_refs[g - 1][pl.ds(r, rows, stride=dilation), :] = shifted

    pad = 2 * SUBLANES
    zeros_pad = jnp.zeros((SUBLANES, POOL_W), _F32)
    for ref in (ext_ref, lvl_a_ref, lvl_b_ref, lvl_c_ref):
        ref[0:SUBLANES, :] = zeros_pad
        ref[tm + pad + SUBLANES:tm + 2 * pad, :] = zeros_pad
    ext_ref[SUBLANES:pad, :] = jnp.where(i > 0, up_ref[0], 0.0)
    ext_ref[pad:pad + tm, :] = u_ref[0]
    ext_ref[pad + tm:pad + tm + SUBLANES, :] = jnp.where(i < n_i - 1, un_ref[0], 0.0)
    span = tm + 2 * SUBLANES
    win = lambda ref, off: ref[SUBLANES + off:SUBLANES + off + span, :]
    lvl_a_ref[SUBLANES:SUBLANES + span, :] = win(ext_ref, -1) + win(ext_ref, 0)
    lvl_b_ref[SUBLANES:SUBLANES + span, :] = win(lvl_a_ref, -1) + win(lvl_a_ref, 1)
    lvl_c_ref[SUBLANES:SUBLANES + span, :] = win(lvl_b_ref, -2) + win(lvl_b_ref, 2)
    lane = lax.broadcasted_iota(_I32, (1, POOL_W), 1)
    wgroup = lane // POOL_GROUP_W
    half_w = jnp.where(wgroup == 0, 1, jnp.where(wgroup == 1, 2, jnp.where(wgroup == 2, 4, 8)))

    half = tm // MIX_CHUNKS
    for hf in range(MIX_CHUNKS):
        r0 = hf * half
        rows = slice(r0, r0 + half)
        outs = [o0_ref[0, rows, :].astype(_F32)]
        lses = [l0_ref[0, rows, :]]
        for g in range(1, N_GROUPS):
            outs.append(jnp.concatenate([ref[rows, :] for ref in nat_o_refs[g - 1]], axis=1))
            lses.append(nat_l_refs[g - 1][rows, 0:HEADS_PER_GROUP])

        top = jnp.maximum(jnp.maximum(lses[0], lses[1]), lses[2])
        es = [jnp.exp(t - top) for t in lses]
        den = es[0] + es[1] + es[2]
        pieces = []
        for g in range(N_GROUPS):
            alpha = es[g] / den
            wide = jnp.concatenate([jnp.broadcast_to(alpha[:, h:h + 1], (half, HEAD_DIM))
                                    for h in range(HEADS_PER_GROUP)], axis=1)
            pieces.append((outs[g] * wide).astype(_BF16))

        s2 = lvl_a_ref[pad + r0:pad + r0 + half, :]
        s4 = lvl_b_ref[pad + r0:pad + r0 + half, :]
        s8 = lvl_c_ref[pad + r0:pad + r0 + half, :]
        s16 = lvl_c_ref[pad + r0 - 4:pad + r0 - 4 + half, :] + lvl_c_ref[pad + r0 + 4:pad + r0 + 4 + half, :]
        wsum = jnp.where(wgroup == 0, s2, jnp.where(wgroup == 1, s4, jnp.where(wgroup == 2, s8, s16)))
        pos = i * tm + r0 + lax.broadcasted_iota(_I32, (half, 1), 0)
        cnt = jnp.minimum(pos + half_w, seq_len) - jnp.maximum(pos - half_w, 0)
        z = wsum / cnt.astype(_F32) - u_ref[0, rows, :]
        pool = jnp.dot(z.astype(_BF16), wpool_ref[...], preferred_element_type=_F32) * pscale_ref[...]
        pieces.append(pool.astype(_BF16))

        mix = jnp.dot(jnp.concatenate(pieces, axis=1), wout_ref[...], preferred_element_type=_F32)
        x1 = x_ref[0, rows, :] + mod[2:3] * mix
        x1_ref[0, rows, :] = x1

        h2 = _rmsnorm_mod(x1, gain_ref[...], mod[4:5], mod[3:4])
        h_hi = h2.astype(_BF16)
        h2_ref[0, rows, :] = h_hi
        h_lo = (h2 - h_hi.astype(_F32)).astype(_BF16)
        nt = (((1,), (1,)), ((), ()))
        logits = (lax.dot_general(wr_hi_ref[...], h_hi, nt, preferred_element_type=_F32)
                  + lax.dot_general(wr_lo_ref[...], h_hi, nt, preferred_element_type=_F32)
                  + lax.dot_general(wr_hi_ref[...], h_lo, nt, preferred_element_type=_F32))
        aff = 1.0 / (1.0 + jnp.exp(-logits))
        cls_ref[0, :, rows] = _route(aff + rbias_ref[...])


def _mix_out(x, attn, lses, u, mod_l, wpool_bd, pool_scale, w_out_bf16, gain, wr_hi, wr_lo, rbias, batch_offset,
             rider=None):
    B, S, _ = x.shape
    tm = MIX_TILE
    n_i = S // tm
    halo_blocks = tm // SUBLANES
    tok = lambda w: pl.BlockSpec((1, tm, w), lambda b, i: (b, i, 0))
    const = lambda shape: pl.BlockSpec(shape, lambda b, i: tuple(0 for _ in shape))
    rows = tm + 4 * SUBLANES
    return _launch(
        functools.partial(_mix_out_kernel, seq_len=S),
        grid=(B, n_i),
        in_specs=[
            tok(D_MODEL),
            *[pl.BlockSpec((1, tm // d, d * GROUP_W), lambda b, i: (b, i, 0)) for _, d in DILATED_GROUPS],
            *[pl.BlockSpec((1, tm // d, d * HEADS_PER_GROUP), lambda b, i: (b, i, 0)) for _, d in DILATED_GROUPS],
            tok(POOL_W),
            pl.BlockSpec((1, SUBLANES, POOL_W), lambda b, i: (b, jnp.maximum(i * halo_blocks - 1, 0), 0)),
            pl.BlockSpec((1, SUBLANES, POOL_W),
                         lambda b, i: (b, jnp.minimum((i + 1) * halo_blocks, S // SUBLANES - 1), 0)),
            pl.BlockSpec((1, 6, D_MODEL), lambda b, i: (b + batch_offset, 0, 0)),
            const((POOL_W, POOL_W)), const((1, POOL_W)), const((D_MODEL, D_MODEL)), const((1, D_MODEL)),
            const((N_EXPERTS, D_MODEL)), const((N_EXPERTS, D_MODEL)), const((N_EXPERTS, 1)),
        ],
        out_specs=[tok(D_MODEL), tok(D_MODEL), pl.BlockSpec((1, 1, tm), lambda b, i: (b, 0, i))],
        out_shape=[jax.ShapeDtypeStruct((B, S, D_MODEL), _F32),
                   jax.ShapeDtypeStruct((B, S, D_MODEL), _BF16),
                   jax.ShapeDtypeStruct((B, 1, S), _I32)],
        scratch=[pltpu.VMEM((rows, POOL_W), _F32)] * 4 + [pltpu.VMEM((tm, LANES), _F32)] * 6,
        args=(x, *attn, *lses, u, u, u, mod_l, wpool_bd, pool_scale, w_out_bf16, gain, wr_hi, wr_lo, rbias),
        name="mix_out",
        rider=rider,
    )


TBL_EXPERT_A, TBL_EXPERT_B, TBL_TILE, TBL_USED, TBL_LAST_TILE_ROW = range(5)


def _plan_kernel(cls_ref, slot_ref, tbl_ref, cnt_ref, run_ref, base_ref, *, tile):
    phase, b, j = pl.program_id(0), pl.program_id(1), pl.program_id(2)
    first = (b == 0) & (j == 0)
    chunk = cls_ref.shape[2]
    n_tbl = tbl_ref.shape[1]
    onehot = lax.broadcasted_iota(_I32, (CLASS_ROWS, chunk), 0) == cls_ref[0]
    hits = jnp.sum(onehot.astype(_F32), axis=1, keepdims=True)

    @pl.when((phase == 0) & first)
    def _():
        cnt_ref[...] = jnp.zeros_like(cnt_ref)

    @pl.when(phase == 0)
    def _():
        cnt_ref[...] += hits

    @pl.when((phase == 1) & first)
    def _():
        padded = jnp.ceil(cnt_ref[...] * (1.0 / tile)) * tile
        r_i = lax.broadcasted_iota(_I32, (CLASS_ROWS, CLASS_ROWS), 0)
        c_i = lax.broadcasted_iota(_I32, (CLASS_ROWS, CLASS_ROWS), 1)
        padded_row = jnp.sum(jnp.where(r_i == c_i, padded, 0.0), axis=0, keepdims=True)
        base_col = jnp.sum(jnp.where(c_i < r_i, padded_row, 0.0), axis=1, keepdims=True)
        base_row = jnp.sum(jnp.where(r_i < c_i, padded, 0.0), axis=0, keepdims=True)
        total = jnp.sum(padded_row, axis=1, keepdims=True)
        base_ref[...] = base_col
        run_ref[...] = jnp.zeros_like(run_ref)

        k_row = lax.broadcasted_iota(_I32, (1, n_tbl), 1).astype(_F32)
        start = k_row * tile
        inside = (base_col <= start) & (start < base_col + padded)
        cls_id = lax.broadcasted_iota(_I32, (CLASS_ROWS, n_tbl), 0).astype(_F32)
        tile_cls = jnp.sum(jnp.where(inside, cls_id, 0.0), axis=0, keepdims=True)
        n_used = total * (1.0 / tile)
        last = jnp.maximum(n_used - 1.0, 0.0)
        last_cls = jnp.sum(jnp.where(k_row == last, tile_cls, 0.0), axis=1, keepdims=True)
        used = k_row < n_used
        tile_cls = jnp.where(used, tile_cls, last_cls)
        grp = sum(jnp.where(tile_cls >= PAIRS_PER_GROUP * g, 1.0, 0.0) for g in range(1, N_EXPERT_GROUPS))
        pair = tile_cls - grp * PAIRS_PER_GROUP
        a = jnp.where(pair < 3, 0.0, jnp.where(pair < 5, 1.0, 2.0))
        bb = jnp.where(pair < 3, pair + 1.0, jnp.where(pair < 5, pair - 1.0, 3.0))
        last_tile_row = jnp.where(padded_row > 0, base_row + padded_row - tile, -1.0)
        last_tile_row = jnp.concatenate(
            [last_tile_row, jnp.full((1, n_tbl - CLASS_ROWS), -1.0, _F32)], axis=1)
        zero = jnp.zeros((1, n_tbl), _F32)
        tbl_ref[...] = jnp.concatenate(
            [grp * EXPERTS_PER_GROUP + a, grp * EXPERTS_PER_GROUP + bb, jnp.where(used, k_row, last),
             jnp.where(used, 1.0, 0.0), last_tile_row, zero, zero, zero], axis=0).astype(_I32)

    @pl.when(phase == 1)
    def _():
        before = lax.broadcasted_iota(_I32, (chunk, chunk), 0) < lax.broadcasted_iota(_I32, (chunk, chunk), 1)
        earlier = jnp.dot(jnp.where(onehot, 1.0, 0.0).astype(_BF16), jnp.where(before, 1.0, 0.0).astype(_BF16),
                          preferred_element_type=_F32)
        dest = base_ref[...] + run_ref[...] + earlier
        slot_ref[0] = jnp.sum(jnp.where(onehot, dest, 0.0), axis=0, keepdims=True).astype(_I32)
        run_ref[...] += hits


def _moe_plan(cls, tile):
    B, _, S = cls.shape
    chunk = PLAN_CHUNK
    n_tiles = (B * S) // tile + N_CLASSES
    n_tbl = -(-n_tiles // LANES) * LANES
    slot, tbl = pl.pallas_call(
        functools.partial(_plan_kernel, tile=tile),
        grid=(2, B, S // chunk),
        in_specs=[pl.BlockSpec((1, 1, chunk), lambda p, b, j: (b, 0, j))],
        out_specs=[pl.BlockSpec((1, 1, chunk), lambda p, b, j: (b * p, 0, j * p)),
                   pl.BlockSpec((SUBLANES, n_tbl), lambda p, b, j: (0, 0))],
        out_shape=[jax.ShapeDtypeStruct((B, 1, S), _I32), jax.ShapeDtypeStruct((SUBLANES, n_tbl), _I32)],
        scratch_shapes=[pltpu.VMEM((CLASS_ROWS, 1), _F32)] * 3,
        compiler_params=_params(("arbitrary", "arbitrary", "arbitrary")),
        name="moe_plan",
    )(cls)
    return slot, tbl, n_tiles


def _row_copy(src, src_row, dst, dst_row, sem):
    return pltpu.make_async_copy(src.at[pl.ds(pl.multiple_of(src_row * ROW_CHUNKS, ROW_CHUNKS), ROW_CHUNKS), :],
                                 dst.at[pl.ds(pl.multiple_of(dst_row * ROW_CHUNKS, ROW_CHUNKS), ROW_CHUNKS), :],
                                 sem)


def _dispatch_start(tbl_ref, slot_ref, h_ref, hs_ref, stage_ref, zero_ref, sem, zsem, *, tile):
    first = (pl.program_id(0) == 0) & (pl.program_id(1) == 0)
    tm = h_ref.shape[1]
    tile_rows = tile * ROW_CHUNKS

    @pl.when(first)
    def _():
        zero_ref[...] = jnp.zeros_like(zero_ref)
        n_tiles = hs_ref.shape[0] // tile_rows
        for wait in (False, True):
            for c in range(N_CLASSES):
                row = tbl_ref[TBL_LAST_TILE_ROW, c]

                @pl.when(row >= 0)
                def _():
                    at = pl.multiple_of(row * ROW_CHUNKS, ROW_CHUNKS)
                    cp = pltpu.make_async_copy(zero_ref, hs_ref.at[pl.ds(at, tile_rows), :], zsem)
                    cp.wait() if wait else cp.start()

            for k in range(n_tiles - N_CLASSES, n_tiles):
                @pl.when(tbl_ref[TBL_USED, k] == 0)
                def _():
                    cp = pltpu.make_async_copy(zero_ref, hs_ref.at[pl.ds(k * tile_rows, tile_rows), :], zsem)
                    cp.wait() if wait else cp.start()

    h = h_ref[0]
    for c in range(ROW_CHUNKS):
        stage_ref[pl.ds(c, tm, stride=ROW_CHUNKS), :] = h[:, c * LANES:(c + 1) * LANES].astype(_F32)

    def issue(g, carry):
        for k in range(ROWS_PER_TRIP):
            r = g * ROWS_PER_TRIP + k
            _row_copy(stage_ref, r, hs_ref, slot_ref[0, 0, r], sem).start(priority=k % 2)
        return carry

    lax.fori_loop(0, tm // ROWS_PER_TRIP, issue, 0)


def _dispatch_wait(stage_ref, hs_ref, sem):
    pltpu.make_async_copy(stage_ref, hs_ref.at[pl.ds(0, stage_ref.shape[0]), :], sem).wait()


def _launch(kernel_fn, grid, in_specs, out_specs, out_shape, scratch, args, name, rider=None):
    params = _params(("arbitrary",) * len(grid))
    if rider is None:
        return pl.pallas_call(kernel_fn, grid=grid, in_specs=in_specs, out_specs=out_specs, out_shape=out_shape,
                              scratch_shapes=scratch, compiler_params=params, name=name)(*args)
    tbl, slot, h2, n_tiles = rider
    tile = MOE_TILE
    steps = grid[0] * grid[1]
    Bo, So, _ = h2.shape
    share = (Bo * So) // steps
    per_seq = So // share
    n_in, n_out, n_scr = len(in_specs), len(out_specs), len(scratch)

    def drop_tbl(spec):
        if spec.index_map is None:
            return spec
        return pl.BlockSpec(spec.block_shape, lambda *a, f=spec.index_map: f(*a[:-1]), memory_space=spec.memory_space)

    def kern(tbl_ref, *refs):
        host_in, (slot_ref, h_ref) = refs[:n_in], refs[n_in:n_in + 2]
        host_out, hs_ref = refs[n_in + 2:n_in + 2 + n_out], refs[n_in + 2 + n_out]
        host_scr = refs[n_in + 3 + n_out:n_in + 3 + n_out + n_scr]
        stage_ref, zero_ref, sem, zsem = refs[n_in + 3 + n_out + n_scr:]
        _dispatch_start(tbl_ref, slot_ref, h_ref, hs_ref, stage_ref, zero_ref, sem, zsem, tile=tile)
        kernel_fn(*host_in, *host_out, *host_scr)
        _dispatch_wait(stage_ref, hs_ref, sem)

    step_of = lambda b, i: b * grid[1] + i
    rider_in = [pl.BlockSpec((1, 1, share), lambda b, i, t: (step_of(b, i) // per_seq, 0, step_of(b, i) % per_seq),
                             memory_space=pltpu.SMEM),
                pl.BlockSpec((1, share, D_MODEL), lambda b, i, t: (step_of(b, i) // per_seq, step_of(b, i) % per_seq, 0))]
    return pl.pallas_call(
        kern,
        grid_spec=pltpu.PrefetchScalarGridSpec(
            num_scalar_prefetch=1,
            grid=grid,
            in_specs=[drop_tbl(s) for s in in_specs] + rider_in,
            out_specs=[drop_tbl(s) for s in out_specs] + [pl.BlockSpec(memory_space=pl.ANY)],
            scratch_shapes=list(scratch) + [pltpu.VMEM((share * ROW_CHUNKS, LANES), _F32),
                                            pltpu.VMEM((tile * ROW_CHUNKS, LANES), _F32),
                                            pltpu.SemaphoreType.DMA(()), pltpu.SemaphoreType.DMA(())],
        ),
        out_shape=list(out_shape) + [jax.ShapeDtypeStruct((n_tiles * tile * ROW_CHUNKS, LANES), _F32)],
        compiler_params=params,
        name=name + "_dispatch",
    )(tbl, *args, slot, h2)


EXPERT_TILE_INPUTS = 11


def _experts_kernel(tbl_ref, *refs):
    n = EXPERT_TILE_INPUTS
    ins = (refs[:n], refs[n:2 * n])
    ys_ref = refs[2 * n]
    rows = ys_ref.shape[0] // 2
    k = pl.program_id(0)
    used = [tbl_ref[TBL_USED, 2 * k + j] for j in range(2)]

    def idle(j):
        ys_ref[j * rows:(j + 1) * rows, :] = jnp.zeros((rows, LANES), _F32)

    @pl.when(used[1] == 1)
    def _():
        for j in range(2):
            _expert_tile(*ins[j], ys_ref, j * rows)

    @pl.when((used[0] == 1) & (used[1] == 0))
    def _():
        _expert_tile(*ins[0], ys_ref, 0)
        idle(1)

    @pl.when(used[0] == 0)
    def _():
        idle(0)
        idle(1)


def _expert_tile(hs_ref, wra_hi_ref, wra_lo_ref, wrb_hi_ref, wrb_lo_ref,
                 wga_ref, wua_ref, wda_ref, wgb_ref, wub_ref, wdb_ref, ys_ref, row0):
    tile = hs_ref.shape[0] // ROW_CHUNKS
    x = jnp.concatenate([hs_ref[pl.ds(c, tile, stride=ROW_CHUNKS), :].astype(_BF16)
                         for c in range(ROW_CHUNKS)], axis=1)
    row_id = lax.broadcasted_iota(_I32, (2 * SUBLANES, D_MODEL), 0)
    rows = jnp.zeros((2 * SUBLANES, D_MODEL), _F32)
    for n, ref in enumerate((wra_hi_ref, wra_lo_ref, wrb_hi_ref, wrb_lo_ref)):
        rows = jnp.where(row_id == n, ref[0], rows)
    lg = lax.dot_general(x, rows.astype(_BF16), (((1,), (1,)), ((), ())),
                         preferred_element_type=_F32)
    aff_a = 1.0 / (1.0 + jnp.exp(-(lg[:, 0:1] + lg[:, 1:2])))
    aff_b = 1.0 / (1.0 + jnp.exp(-(lg[:, 2:3] + lg[:, 3:4])))
    den = aff_a + aff_b

    def hidden(wg_ref, wu_ref, gate):
        a = jnp.dot(x, wg_ref[0], preferred_element_type=_F32)
        b = jnp.dot(x, wu_ref[0], preferred_element_type=_F32)
        return ((a / (1.0 + jnp.exp(-a))) * b * gate).astype(_BF16)

    y = (jnp.dot(hidden(wga_ref, wua_ref, aff_a / den), wda_ref[0], preferred_element_type=_F32)
         + jnp.dot(hidden(wgb_ref, wub_ref, aff_b / den), wdb_ref[0], preferred_element_type=_F32))
    for c in range(ROW_CHUNKS):
        ys_ref[pl.ds(row0 + c, tile, stride=ROW_CHUNKS), :] = y[:, c * LANES:(c + 1) * LANES]


def _moe_experts(tbl, hs, wr_hi, wr_lo, wg, wu, wd, layer, n_tiles, tile):
    rows = tile * ROW_CHUNKS
    first = layer * N_EXPERTS
    a, b = TBL_EXPERT_A, TBL_EXPERT_B
    in_specs = []
    for j in range(2):
        router = lambda row, j=j: pl.BlockSpec((1, 1, D_MODEL), lambda k, tbl: (tbl[row, 2 * k + j], 0, 0))
        up = lambda row, j=j: pl.BlockSpec((1, D_MODEL, D_EXPERT),
                                           lambda k, tbl: (first + tbl[row, 2 * k + j], 0, 0))
        down = lambda row, j=j: pl.BlockSpec((1, D_EXPERT, D_MODEL),
                                             lambda k, tbl: (first + tbl[row, 2 * k + j], 0, 0))
        in_specs += [pl.BlockSpec((rows, LANES), lambda k, tbl, j=j: (tbl[TBL_TILE, 2 * k + j], 0)),
                     router(a), router(a), router(b), router(b), up(a), up(a), down(a), up(b), up(b), down(b)]
    tile_args = (hs, wr_hi, wr_lo, wr_hi, wr_lo, wg, wu, wd, wg, wu, wd)
    return pl.pallas_call(
        _experts_kernel,
        grid_spec=pltpu.PrefetchScalarGridSpec(
            num_scalar_prefetch=1,
            grid=(n_tiles // 2,),
            in_specs=in_specs,
            out_specs=pl.BlockSpec((2 * rows, LANES), lambda k, tbl: (k, 0)),
        ),
        out_shape=jax.ShapeDtypeStruct(hs.shape, _F32),
        compiler_params=_params(("arbitrary",)),
        name="moe_experts",
    )(tbl, *tile_args, *tile_args)


def _combine_kernel(slot_ref, slot_next_ref, x1_ref, mod_ref, ys_ref, *refs, project):
    if project:
        (modn_ref, gain_ref, w_ref, cos_ref, sin_ref, x2_ref, *out_refs) = refs[:16]
        buf_ref, sem, stage_ref = refs[16:19]
    else:
        gfin_ref, x2_ref, buf_ref, sem = refs[:4]
    tm = x1_ref.shape[1]
    step = pl.program_id(0) * pl.num_programs(1) + pl.program_id(1)
    last = pl.num_programs(0) * pl.num_programs(1) - 1
    gate = mod_ref[0][5:6]

    def gather(sref):
        def issue(g, carry):
            for k in range(ROWS_PER_TRIP):
                r = g * ROWS_PER_TRIP + k
                _row_copy(ys_ref, sref[0, 0, r], buf_ref, r, sem).start(priority=k % 2)
            return carry
        lax.fori_loop(0, tm // ROWS_PER_TRIP, issue, 0)

    @pl.when(step == 0)
    def _():
        gather(slot_ref)

    pltpu.make_async_copy(ys_ref.at[pl.ds(0, tm * ROW_CHUNKS), :], buf_ref, sem).wait()
    for c in range(ROW_CHUNKS):
        cols = slice(c * LANES, (c + 1) * LANES)
        x2_ref[0, :, cols] = x1_ref[0, :, cols] + gate[:, cols] * buf_ref[pl.ds(c, tm, stride=ROW_CHUNKS), :]

    @pl.when(step < last)
    def _():
        gather(slot_next_ref)

    x2 = x2_ref[0]
    if project:
        _project(x2, 0, modn_ref[0], gain_ref, w_ref, cos_ref, sin_ref, out_refs[0:3], out_refs[3:6],
                 out_refs[6:9], out_refs[9], stage_ref)
    else:
        x2_ref[0] = x2 * lax.rsqrt(jnp.mean(x2 * x2, axis=-1, keepdims=True) + RMS_EPS) * gfin_ref[...]


def _moe_combine(slot, x1, mod_prev, ys, batch_offset, tail_args, rider=None):
    B, S, _ = x1.shape
    tm = TOKEN_TILE
    n_i = S // tm
    project = len(tail_args) > 1

    def next_step(b, i):
        n = jnp.minimum(b * n_i + i + 1, B * n_i - 1)
        return n // n_i, 0, n % n_i

    tok = pl.BlockSpec((1, tm, D_MODEL), lambda b, i: (b, i, 0))
    mod_spec = pl.BlockSpec((1, 6, D_MODEL), lambda b, i: (b + batch_offset, 0, 0))
    in_specs = [pl.BlockSpec((1, 1, tm), lambda b, i: (b, 0, i), memory_space=pltpu.SMEM),
                pl.BlockSpec((1, 1, tm), next_step, memory_space=pltpu.SMEM),
                tok, mod_spec, pl.BlockSpec(memory_space=pl.ANY)]
    out_specs, out_shape = [tok], [jax.ShapeDtypeStruct((B, S, D_MODEL), _F32)]
    scratch = [pltpu.VMEM((tm * ROW_CHUNKS, LANES), _F32), pltpu.SemaphoreType.DMA(())]
    if project:
        in_specs += [mod_spec,
                     pl.BlockSpec((1, D_MODEL), lambda b, i: (0, 0)),
                     pl.BlockSpec((D_MODEL, IN_W), lambda b, i: (0, 0)),
                     pl.BlockSpec((tm, LANES), lambda b, i: (i, 0)),
                     pl.BlockSpec((tm, LANES), lambda b, i: (i, 0))]
        out_specs += [pl.BlockSpec((1, tm // d, d * GROUP_W), lambda b, i: (b, i, 0)) for _, d in DILATED_GROUPS] * 3
        out_specs += [pl.BlockSpec((1, tm, POOL_W), lambda b, i: (b, i, 0))]
        out_shape += [jax.ShapeDtypeStruct((B, S // d, d * GROUP_W), _BF16) for _, d in DILATED_GROUPS] * 3
        out_shape += [jax.ShapeDtypeStruct((B, S, POOL_W), _F32)]
        scratch += [pltpu.VMEM((tm, LANES), _F32)]
    else:
        in_specs += [pl.BlockSpec((1, D_MODEL), lambda b, i: (0, 0))]
    return _launch(
        functools.partial(_combine_kernel, project=project),
        grid=(B, n_i),
        in_specs=in_specs,
        out_specs=out_specs,
        out_shape=out_shape,
        scratch=scratch,
        args=(slot, slot, x1, mod_prev, ys, *tail_args),
        name="moe_combine_in_proj" if project else "moe_combine_norm",
        rider=rider,
    )


def _rope_tables(S):
    inv = 1.0 / (ROPE_THETA ** (jnp.arange(0, HEAD_DIM, 2, dtype=_F32) / HEAD_DIM))
    ang = jnp.arange(S, dtype=_F32)[:, None] * inv[None, :]
    cos, sin = jnp.cos(ang), jnp.sin(ang)
    reps = LANES // HEAD_DIM
    return (jnp.tile(jnp.concatenate([cos, cos], axis=1), (1, reps)),
            jnp.tile(jnp.concatenate([-sin, sin], axis=1), (1, reps)))


def _trunks(xs, batch_offsets, mod, w):
    tables = [_rope_tables(x.shape[1]) for x in xs]
    outs = [_in_proj(x, mod[0], w["norm_mix"][0], w["w_in"][0], *tables[t], batch_offsets[t])
            for t, x in enumerate(xs)]
    xs = list(xs)
    for l in range(DEPTH):
        mod_l = mod[l]

        def mix(t, rider):
            q, k, v, u = outs[t][0:3], outs[t][3:6], outs[t][6:9], outs[t][9]
            attn, lses = [], []
            for g, (window, dilation) in enumerate(DILATED_GROUPS):
                o, lse = _attention(q[g], k[g], v[g], window, dilation)
                attn.append(o)
                lses.append(lse)
            return _mix_out(xs[t], attn, lses, u, mod_l, w["wpool_bd"][l], w["pool_scale"][l], w["w_out"][l],
                            w["norm_ffn"][l], w["wr_hi"], w["wr_lo"], w["rbias"], batch_offsets[t], rider)

        def experts(tbl, hs, n_tiles):
            return _moe_experts(tbl, hs, w["wr_hi3"], w["wr_lo3"], w["w_gate"], w["w_up"], w["w_down"], l,
                                n_tiles, MOE_TILE)

        def combine(t, slot, x1, ys, rider):
            if l + 1 < DEPTH:
                tail = (mod[l + 1], w["norm_mix"][l + 1], w["w_in"][l + 1], *tables[t])
            else:
                tail = (w["norm_final"],)
            return _moe_combine(slot, x1, mod_l, ys, batch_offsets[t], tail, rider)

        x1_0, h2_0, cls_0 = mix(0, None)
        slot_0, tbl_0, n_0 = _moe_plan(cls_0, MOE_TILE)
        x1_1, h2_1, cls_1, hs_0 = mix(1, (tbl_0, slot_0, h2_0, n_0))
        slot_1, tbl_1, n_1 = _moe_plan(cls_1, MOE_TILE)
        ys_0 = experts(tbl_0, hs_0, n_0)
        xs[0], *rest = combine(0, slot_0, x1_0, ys_0, (tbl_1, slot_1, h2_1, n_1))
        outs[0], hs_1 = rest[:-1], rest[-1]
        ys_1 = experts(tbl_1, hs_1, n_1)
        xs[1], *outs[1] = combine(1, slot_1, x1_1, ys_1, None)
    return xs


def kernel(x_prompt, x_sample, c_prompt, c_sample, norm_mix, w_mod, b_mod, w_in, w_pool, pool_scale, w_out,
           norm_ffn, w_router, router_bias, w_gate, w_up, w_down, norm_final):
    n_prompt = x_prompt.shape[0]
    c = jnp.concatenate([c_prompt, c_sample], axis=0)
    mod = _modulation(c, w_mod, b_mod).reshape(DEPTH, c.shape[0], 6, D_MODEL)
    n_pool = len(POOL_WINDOWS)
    eye = jnp.eye(n_pool, dtype=w_pool.dtype)
    wpool_bd = (w_pool[:, :, :, None, :] * eye[None, :, None, :, None]).reshape(DEPTH, POOL_W, POOL_W)
    wr_t = w_router.T.astype(_F32)
    wr_hi = wr_t.astype(_BF16)
    wr_lo = (wr_t - wr_hi.astype(_F32)).astype(_BF16)
    w = {
        "norm_mix": norm_mix.reshape(DEPTH, 1, D_MODEL),
        "norm_ffn": norm_ffn.reshape(DEPTH, 1, D_MODEL),
        "norm_final": norm_final.reshape(1, D_MODEL),
        "w_in": w_in.astype(_BF16),
        "w_out": w_out.astype(_BF16),
        "wpool_bd": wpool_bd.astype(_BF16),
        "pool_scale": pool_scale.reshape(DEPTH, 1, POOL_W),
        "wr_hi": wr_hi,
        "wr_lo": wr_lo,
        "wr_hi3": wr_hi.astype(_F32).reshape(N_EXPERTS, 1, D_MODEL),
        "wr_lo3": wr_lo.astype(_F32).reshape(N_EXPERTS, 1, D_MODEL),
        "rbias": router_bias.astype(_F32).reshape(N_EXPERTS, 1),
        "w_gate": w_gate.astype(_BF16).reshape(DEPTH * N_EXPERTS, D_MODEL, D_EXPERT),
        "w_up": w_up.astype(_BF16).reshape(DEPTH * N_EXPERTS, D_MODEL, D_EXPERT),
        "w_down": w_down.astype(_BF16).reshape(DEPTH * N_EXPERTS, D_EXPERT, D_MODEL),
    }
    y_prompt, y_sample = _trunks((x_prompt, x_sample), (0, n_prompt), mod, w)
    return (y_prompt, y_sample)
```

```python
import functools

import jax
import jax.numpy as jnp
from jax import lax
from jax.experimental import pallas as pl
from jax.experimental.pallas import tpu as pltpu

D_MODEL = 1024
DEPTH = 4
HEAD_DIM = 64
HALF_HEAD = HEAD_DIM // 2
DILATED_GROUPS = ((128, 1), (512, 4), (2048, 16))
N_GROUPS = len(DILATED_GROUPS)
HEADS_PER_GROUP = 4
GROUP_W = HEADS_PER_GROUP * HEAD_DIM
ATT_W = N_GROUPS * GROUP_W
POOL_WINDOWS = (2, 4, 8, 16)
POOL_GROUP_W = 64
POOL_W = POOL_GROUP_W * len(POOL_WINDOWS)
IN_W = 3 * ATT_W + POOL_W
ROPE_THETA = 10000.0
RMS_EPS = 1e-6
N_EXPERTS = 16
N_EXPERT_GROUPS = 4
EXPERTS_PER_GROUP = N_EXPERTS // N_EXPERT_GROUPS
PAIRS_PER_GROUP = EXPERTS_PER_GROUP * (EXPERTS_PER_GROUP - 1) // 2
N_CLASSES = N_EXPERT_GROUPS * PAIRS_PER_GROUP
D_EXPERT = 512

LANES = 128
SUBLANES = 8
Q_BLOCK = 128
VMEM_LIMIT = 48 * 1024 * 1024

TOKEN_TILE = 512
MOE_TILE = 256
MIX_TILE = 1024
MIX_CHUNKS = 2
PLAN_CHUNK = 512
CLASS_ROWS = 32
ROW_CHUNKS = D_MODEL // LANES
ROWS_PER_TRIP = 32

_BF16 = jnp.bfloat16
_F32 = jnp.float32
_I32 = jnp.int32


def _params(semantics):
    return pltpu.CompilerParams(dimension_semantics=semantics, vmem_limit_bytes=VMEM_LIMIT)


def _mod_kernel(c_ref, w_ref, b_ref, o_ref):
    c = c_ref[...]
    sc = c / (1.0 + jnp.exp(-c))
    o_ref[0] = jnp.dot(sc, w_ref[0], preferred_element_type=_F32,
                       precision=lax.Precision.HIGHEST) + b_ref[0]


def _modulation(c, w_mod, b_mod):
    nb = c.shape[0]
    col = D_MODEL
    n_col = w_mod.shape[2] // col
    return pl.pallas_call(
        _mod_kernel,
        grid=(DEPTH, n_col),
        in_specs=[
            pl.BlockSpec((nb, D_MODEL), lambda l, j: (0, 0)),
            pl.BlockSpec((1, D_MODEL, col), lambda l, j: (l, 0, j)),
            pl.BlockSpec((1, 1, col), lambda l, j: (l, 0, j)),
        ],
        out_specs=pl.BlockSpec((1, nb, col), lambda l, j: (l, 0, j)),
        out_shape=jax.ShapeDtypeStruct((DEPTH, nb, w_mod.shape[2]), _F32),
        compiler_params=_params(("arbitrary", "arbitrary")),
        name="modulation",
    )(c, w_mod, b_mod.reshape(DEPTH, 1, -1))


def _rmsnorm_mod(x, gain, scale, shift):
    y = x * lax.rsqrt(jnp.mean(x * x, axis=-1, keepdims=True) + RMS_EPS)
    return (y * gain) * (1.0 + scale) + shift


def _rope_chunk(t, cos, sin_signed, first_half):
    fwd = pltpu.roll(t, HALF_HEAD, axis=1)
    bwd = pltpu.roll(t, LANES - HALF_HEAD, axis=1)
    return t * cos + jnp.where(first_half, bwd, fwd) * sin_signed


def _store_by_residue(out_ref, chunk, value, dilation, stage_ref, row0):
    rows = value.shape[0] // dilation
    at = slice(row0 // dilation, row0 // dilation + rows)
    if dilation == 1:
        out_ref[0, at, chunk * LANES:(chunk + 1) * LANES] = value.astype(_BF16)
        return
    stage_ref[...] = value
    for r in range(dilation):
        lo = r * GROUP_W + chunk * LANES
        out_ref[0, at, lo:lo + LANES] = stage_ref[pl.ds(r, rows, stride=dilation), :].astype(_BF16)


def _in_proj_kernel(x_ref, mod_ref, gain_ref, w_ref, cos_ref, sin_ref, *refs):
    _project(x_ref[0], 0, mod_ref[0], gain_ref, w_ref, cos_ref, sin_ref, refs[0:3], refs[3:6], refs[6:9], refs[9],
             refs[10])


def _project(x, row0, mod, gain_ref, w_ref, cos_ref, sin_ref, q_refs, k_refs, v_refs, u_ref, stage_ref):
    n = x.shape[0]
    h = _rmsnorm_mod(x, gain_ref[...], mod[1:2], mod[0:1])
    proj = jnp.dot(h.astype(_BF16), w_ref[...], preferred_element_type=_F32)
    cos = cos_ref[row0:row0 + n, :]
    sin_signed = sin_ref[row0:row0 + n, :]
    lane = lax.broadcasted_iota(_I32, (1, LANES), 1)
    first_half = (lane % HEAD_DIM) < HALF_HEAD
    q_scale = HEAD_DIM ** -0.5
    for g, (_, dilation) in enumerate(DILATED_GROUPS):
        for c in range(GROUP_W // LANES):
            lo = g * GROUP_W + c * LANES
            q = _rope_chunk(proj[:, lo:lo + LANES], cos, sin_signed, first_half)
            _store_by_residue(q_refs[g], c, q * q_scale, dilation, stage_ref, row0)
            k = _rope_chunk(proj[:, ATT_W + lo:ATT_W + lo + LANES], cos, sin_signed, first_half)
            _store_by_residue(k_refs[g], c, k, dilation, stage_ref, row0)
            _store_by_residue(v_refs[g], c, proj[:, 2 * ATT_W + lo:2 * ATT_W + lo + LANES], dilation, stage_ref,
                              row0)
    u_ref[0, row0:row0 + n, :] = proj[:, 3 * ATT_W:]


def _in_proj(x, mod_l, gain, w_in_bf16, cos, sin_signed, batch_offset):
    B, S, _ = x.shape
    tm = TOKEN_TILE
    grp = [jax.ShapeDtypeStruct((B, S // d, d * GROUP_W), _BF16) for _, d in DILATED_GROUPS]
    grp_spec = [pl.BlockSpec((1, tm // d, d * GROUP_W), lambda b, i: (b, i, 0)) for _, d in DILATED_GROUPS]
    return pl.pallas_call(
        _in_proj_kernel,
        grid=(B, S // tm),
        in_specs=[
            pl.BlockSpec((1, tm, D_MODEL), lambda b, i: (b, i, 0)),
            pl.BlockSpec((1, 6, D_MODEL), lambda b, i: (b + batch_offset, 0, 0)),
            pl.BlockSpec((1, D_MODEL), lambda b, i: (0, 0)),
            pl.BlockSpec((D_MODEL, IN_W), lambda b, i: (0, 0)),
            pl.BlockSpec((tm, LANES), lambda b, i: (i, 0)),
            pl.BlockSpec((tm, LANES), lambda b, i: (i, 0)),
        ],
        out_specs=grp_spec * 3 + [pl.BlockSpec((1, tm, POOL_W), lambda b, i: (b, i, 0))],
        out_shape=grp * 3 + [jax.ShapeDtypeStruct((B, S, POOL_W), _F32)],
        scratch_shapes=[pltpu.VMEM((tm, LANES), _F32)],
        compiler_params=_params(("arbitrary", "arbitrary")),
        name="in_proj",
    )(x, mod_l, gain, w_in_bf16, cos, sin_signed)


def _attn_kernel(q_ref, k_ref, v_ref, o_ref, lse_ref, bias_ref, *, length, n_side, n_res, q_unroll):
    rb = pl.program_id(1)
    key_w = min(length, Q_BLOCK + 2 * n_side)
    n_qb = length // Q_BLOCK

    @pl.when((pl.program_id(0) == 0) & (rb == 0))
    def _():
        i = lax.broadcasted_iota(_I32, (2 * Q_BLOCK, key_w), 0) % Q_BLOCK
        j = lax.broadcasted_iota(_I32, (2 * Q_BLOCK, key_w), 1)
        for n in range(3):
            bias_ref[n] = jnp.where(jnp.abs(i - j + n * n_side) <= n_side, 0.0, -jnp.inf)

    @pl.when(rb == 0)
    def _():
        lse_ref[...] = jnp.zeros_like(lse_ref)

    lane = lax.broadcasted_iota(_I32, (1, LANES), 1)
    head_mask = [(lane < HEAD_DIM).astype(_BF16), (lane >= HEAD_DIM).astype(_BF16)]
    lse_lane = lax.broadcasted_iota(_I32, (1, lse_ref.shape[2]), 1)

    def trip(it, carry):
        for u in range(q_unroll):
            q0 = pl.multiple_of((it * q_unroll + u) * Q_BLOCK, Q_BLOCK)
            k0 = pl.multiple_of(jnp.clip(q0 - n_side, 0, length - key_w), n_side)
            bias = bias_ref[(q0 - k0) // n_side]
            lse_rows = lse_ref[0, pl.ds(q0, Q_BLOCK), :]
            for rr in range(n_res):
                for pair in range(GROUP_W // LANES):
                    lo = rr * GROUP_W + pair * LANES
                    cols = slice(lo, lo + LANES)
                    q2 = q_ref[0, pl.ds(q0, Q_BLOCK), cols]
                    k2 = k_ref[0, pl.ds(k0, key_w), cols]
                    v2 = v_ref[0, pl.ds(k0, key_w), cols]
                    qs = jnp.concatenate([q2 * head_mask[0], q2 * head_mask[1]], axis=0)
                    s = lax.dot_general(qs, k2, (((1,), (1,)), ((), ())), preferred_element_type=_F32) + bias
                    m = jnp.max(s, axis=1, keepdims=True)
                    p = jnp.exp(s - m)
                    l = jnp.sum(p, axis=1, keepdims=True)
                    o = jnp.dot(p.astype(_BF16), v2, preferred_element_type=_F32) * (1.0 / l)
                    lse = m + jnp.log(l)
                    for hh in range(2):
                        col = (rb * n_res + rr) * HEADS_PER_GROUP + pair * 2 + hh
                        lse_rows = jnp.where(lse_lane == col, lse[hh * Q_BLOCK:(hh + 1) * Q_BLOCK], lse_rows)
                    o_ref[0, pl.ds(q0, Q_BLOCK), cols] = jnp.where(lane < HEAD_DIM, o[:Q_BLOCK], o[Q_BLOCK:]).astype(_BF16)
            lse_ref[0, pl.ds(q0, Q_BLOCK), :] = lse_rows
        return carry

    lax.fori_loop(0, n_qb // q_unroll, trip, 0)


ATTN_BLOCKS_PER_TRIP = 8


def _attention(q, k, v, window, dilation):
    B, L, _ = q.shape
    d = dilation
    n_side = window // (2 * d)
    n_res = min(d, ATTN_BLOCKS_PER_TRIP)
    q_unroll = min(ATTN_BLOCKS_PER_TRIP // n_res, L // Q_BLOCK)
    key_w = min(L, Q_BLOCK + 2 * n_side)
    spec = pl.BlockSpec((1, L, n_res * GROUP_W), lambda b, r: (b, 0, r))
    return pl.pallas_call(
        functools.partial(_attn_kernel, length=L, n_side=n_side, n_res=n_res, q_unroll=q_unroll),
        grid=(B, d // n_res),
        in_specs=[spec, spec, spec],
        out_specs=[spec, pl.BlockSpec((1, L, d * HEADS_PER_GROUP), lambda b, r: (b, 0, 0))],
        out_shape=[jax.ShapeDtypeStruct((B, L, d * GROUP_W), _BF16),
                   jax.ShapeDtypeStruct((B, L, d * HEADS_PER_GROUP), _F32)],
        scratch_shapes=[pltpu.VMEM((3, 2 * Q_BLOCK, key_w), _F32)],
        compiler_params=_params(("arbitrary", "arbitrary")),
        name=f"attention_d{d}",
    )(q, k, v)


def _route(sel):
    s = [sel[e:e + 1, :] for e in range(N_EXPERTS)]
    group_score = []
    for g in range(N_EXPERT_GROUPS):
        c0, c1, c2, c3 = s[EXPERTS_PER_GROUP * g:EXPERTS_PER_GROUP * (g + 1)]
        m1, n1 = jnp.maximum(c0, c1), jnp.minimum(c0, c1)
        m2, n2 = jnp.maximum(c2, c3), jnp.minimum(c2, c3)
        group_score.append(jnp.maximum(m1, m2) + jnp.maximum(jnp.minimum(m1, m2), jnp.maximum(n1, n2)))
    best, top_g = group_score[0], jnp.zeros(group_score[0].shape, _I32)
    for g in range(1, N_EXPERT_GROUPS):
        better = group_score[g] > best
        best = jnp.where(better, group_score[g], best)
        top_g = jnp.where(better, g, top_g)
    cs = []
    for j in range(EXPERTS_PER_GROUP):
        c = s[j]
        for g in range(1, N_EXPERT_GROUPS):
            c = jnp.where(top_g == g, s[EXPERTS_PER_GROUP * g + j], c)
        cs.append(c)
    s1, i1 = cs[0], jnp.zeros_like(top_g)
    for j in range(1, EXPERTS_PER_GROUP):
        better = cs[j] > s1
        s1 = jnp.where(better, cs[j], s1)
        i1 = jnp.where(better, j, i1)
    i2 = jnp.where(i1 == 0, 1, 0)
    s2 = jnp.where(i1 == 0, cs[1], cs[0])
    for j in range(1, EXPERTS_PER_GROUP):
        better = (cs[j] > s2) & (i1 != j)
        s2 = jnp.where(better, cs[j], s2)
        i2 = jnp.where(better, j, i2)
    lo, hi = jnp.minimum(i1, i2), jnp.maximum(i1, i2)
    pair = jnp.where(lo == 0, 0, jnp.where(lo == 1, 3, 5)) + (hi - lo - 1)
    return top_g * PAIRS_PER_GROUP + pair


def _mix_out_kernel(x_ref, o0_ref, o1_ref, o2_ref, l0_ref, l1_ref, l2_ref, u_ref, up_ref, un_ref,
                    mod_ref, wpool_ref, pscale_ref, wout_ref, gain_ref, wr_hi_ref, wr_lo_ref, rbias_ref,
                    x1_ref, h2_ref, cls_ref, ext_ref, lvl_a_ref, lvl_b_ref, lvl_c_ref, nat_o1a_ref, nat_o1b_ref,
                    nat_o2a_ref, nat_o2b_ref, nat_l1_ref, nat_l2_ref, *, seq_len):
    nat_o_refs = ((nat_o1a_ref, nat_o1b_ref), (nat_o2a_ref, nat_o2b_ref))
    nat_l_refs = (nat_l1_ref, nat_l2_ref)
    i = pl.program_id(1)
    n_i = pl.num_programs(1)
    tm = x_ref.shape[1]
    mod = mod_ref[0]

    for g, (o_ref, l_ref) in enumerate(((o0_ref, l0_ref), (o1_ref, l1_ref), (o2_ref, l2_ref))):
        dilation = DILATED_GROUPS[g][1]
        if dilation == 1:
            continue
        rows = tm // dilation
        lv = l_ref[0]
        lv = jnp.concatenate([lv, jnp.zeros((rows, LANES - lv.shape[1]), _F32)], axis=1)
        halves = nat_o_refs[g - 1]
        for r in range(dilation):
            for c, half_ref in enumerate(halves):
                lo = r * GROUP_W + c * LANES
                half_ref[pl.ds(r, rows, stride=dilation), :] = o_ref[0, :, lo:lo + LANES].astype(_F32)
            shifted = lv if r == 0 else pltpu.roll(lv, LANES - r * HEADS_PER_GROUP, axis=1)
            nat_l_refs[g - 1][pl.ds(r, rows, stride=dilation), :] = shifted

    pad = 2 * SUBLANES
    zeros_pad = jnp.zeros((SUBLANES, POOL_W), _F32)
    for ref in (ext_ref, lvl_a_ref, lvl_b_ref, lvl_c_ref):
        ref[0:SUBLANES, :] = zeros_pad
        ref[tm + pad + SUBLANES:tm + 2 * pad, :] = zeros_pad
    ext_ref[SUBLANES:pad, :] = jnp.where(i > 0, up_ref[0], 0.0)
    ext_ref[pad:pad + tm, :] = u_ref[0]
    ext_ref[pad + tm:pad + tm + SUBLANES, :] = jnp.where(i < n_i - 1, un_ref[0], 0.0)
    span = tm + 2 * SUBLANES
    win = lambda ref, off: ref[SUBLANES + off:SUBLANES + off + span, :]
    lvl_a_ref[SUBLANES:SUBLANES + span, :] = win(ext_ref, -1) + win(ext_ref, 0)
    lvl_b_ref[SUBLANES:SUBLANES + span, :] = win(lvl_a_ref, -1) + win(lvl_a_ref, 1)
    lvl_c_ref[SUBLANES:SUBLANES + span, :] = win(lvl_b_ref, -2) + win(lvl_b_ref, 2)
    lane = lax.broadcasted_iota(_I32, (1, POOL_W), 1)
    wgroup = lane // POOL_GROUP_W
    half_w = jnp.where(wgroup == 0, 1, jnp.where(wgroup == 1, 2, jnp.where(wgroup == 2, 4, 8)))

    half = tm // MIX_CHUNKS
    for hf in range(MIX_CHUNKS):
        r0 = hf * half
        rows = slice(r0, r0 + half)
        outs = [o0_ref[0, rows, :].astype(_F32)]
        lses = [l0_ref[0, rows, :]]
        for g in range(1, N_GROUPS):
            outs.append(jnp.concatenate([ref[rows, :] for ref in nat_o_refs[g - 1]], axis=1))
            lses.append(nat_l_refs[g - 1][rows, 0:HEADS_PER_GROUP])

        top = jnp.maximum(jnp.maximum(lses[0], lses[1]), lses[2])
        es = [jnp.exp(t - top) for t in lses]
        den = es[0] + es[1] + es[2]
        pieces = []
        for g in range(N_GROUPS):
            alpha = es[g] / den
            wide = jnp.concatenate([jnp.broadcast_to(alpha[:, h:h + 1], (half, HEAD_DIM))
                                    for h in range(HEADS_PER_GROUP)], axis=1)
            pieces.append((outs[g] * wide).astype(_BF16))

        s2 = lvl_a_ref[pad + r0:pad + r0 + half, :]
        s4 = lvl_b_ref[pad + r0:pad + r0 + half, :]
        s8 = lvl_c_ref[pad + r0:pad + r0 + half, :]
        s16 = lvl_c_ref[pad + r0 - 4:pad + r0 - 4 + half, :] + lvl_c_ref[pad + r0 + 4:pad + r0 + 4 + half, :]
        wsum = jnp.where(wgroup == 0, s2, jnp.where(wgroup == 1, s4, jnp.where(wgroup == 2, s8, s16)))
        pos = i * tm + r0 + lax.broadcasted_iota(_I32, (half, 1), 0)
        cnt = jnp.minimum(pos + half_w, seq_len) - jnp.maximum(pos - half_w, 0)
        z = wsum / cnt.astype(_F32) - u_ref[0, rows, :]
        pool = jnp.dot(z.astype(_BF16), wpool_ref[...], preferred_element_type=_F32) * pscale_ref[...]
        pieces.append(pool.astype(_BF16))

        mix = jnp.dot(jnp.concatenate(pieces, axis=1), wout_ref[...], preferred_element_type=_F32)
        x1 = x_ref[0, rows, :] + mod[2:3] * mix
        x1_ref[0, rows, :] = x1

        h2 = _rmsnorm_mod(x1, gain_ref[...], mod[4:5], mod[3:4])
        h_hi = h2.astype(_BF16)
        h2_ref[0, rows, :] = h_hi
        h_lo = (h2 - h_hi.astype(_F32)).astype(_BF16)
        nt = (((1,), (1,)), ((), ()))
        logits = (lax.dot_general(wr_hi_ref[...], h_hi, nt, preferred_element_type=_F32)
                  + lax.dot_general(wr_lo_ref[...], h_hi, nt, preferred_element_type=_F32)
                  + lax.dot_general(wr_hi_ref[...], h_lo, nt, preferred_element_type=_F32))
        aff = 1.0 / (1.0 + jnp.exp(-logits))
        cls_ref[0, :, rows] = _route(aff + rbias_ref[...])


def _mix_out(x, attn, lses, u, mod_l, wpool_bd, pool_scale, w_out_bf16, gain, wr_hi, wr_lo, rbias, batch_offset,
             rider=None):
    B, S, _ = x.shape
    tm = MIX_TILE
    n_i = S // tm
    halo_blocks = tm // SUBLANES
    tok = lambda w: pl.BlockSpec((1, tm, w), lambda b, i: (b, i, 0))
    const = lambda shape: pl.BlockSpec(shape, lambda b, i: tuple(0 for _ in shape))
    rows = tm + 4 * SUBLANES
    return _launch(
        functools.partial(_mix_out_kernel, seq_len=S),
        grid=(B, n_i),
        in_specs=[
            tok(D_MODEL),
            *[pl.BlockSpec((1, tm // d, d * GROUP_W), lambda b, i: (b, i, 0)) for _, d in DILATED_GROUPS],
            *[pl.BlockSpec((1, tm // d, d * HEADS_PER_GROUP), lambda b, i: (b, i, 0)) for _, d in DILATED_GROUPS],
            tok(POOL_W),
            pl.BlockSpec((1, SUBLANES, POOL_W), lambda b, i: (b, jnp.maximum(i * halo_blocks - 1, 0), 0)),
            pl.BlockSpec((1, SUBLANES, POOL_W),
                         lambda b, i: (b, jnp.minimum((i + 1) * halo_blocks, S // SUBLANES - 1), 0)),
            pl.BlockSpec((1, 6, D_MODEL), lambda b, i: (b + batch_offset, 0, 0)),
            const((POOL_W, POOL_W)), const((1, POOL_W)), const((D_MODEL, D_MODEL)), const((1, D_MODEL)),
            const((N_EXPERTS, D_MODEL)), const((N_EXPERTS, D_MODEL)), const((N_EXPERTS, 1)),
        ],
        out_specs=[tok(D_MODEL), tok(D_MODEL), pl.BlockSpec((1, 1, tm), lambda b, i: (b, 0, i))],
        out_shape=[jax.ShapeDtypeStruct((B, S, D_MODEL), _F32),
                   jax.ShapeDtypeStruct((B, S, D_MODEL), _BF16),
                   jax.ShapeDtypeStruct((B, 1, S), _I32)],
        scratch=[pltpu.VMEM((rows, POOL_W), _F32)] * 4 + [pltpu.VMEM((tm, LANES), _F32)] * 6,
        args=(x, *attn, *lses, u, u, u, mod_l, wpool_bd, pool_scale, w_out_bf16, gain, wr_hi, wr_lo, rbias),
        name="mix_out",
        rider=rider,
    )


TBL_EXPERT_A, TBL_EXPERT_B, TBL_TILE, TBL_USED, TBL_LAST_TILE_ROW = range(5)


def _plan_kernel(cls_ref, slot_ref, tbl_ref, cnt_ref, run_ref, base_ref, *, tile):
    phase, b, j = pl.program_id(0), pl.program_id(1), pl.program_id(2)
    first = (b == 0) & (j == 0)
    chunk = cls_ref.shape[2]
    n_tbl = tbl_ref.shape[1]
    onehot = lax.broadcasted_iota(_I32, (CLASS_ROWS, chunk), 0) == cls_ref[0]
    hits = jnp.sum(onehot.astype(_F32), axis=1, keepdims=True)

    @pl.when((phase == 0) & first)
    def _():
        cnt_ref[...] = jnp.zeros_like(cnt_ref)

    @pl.when(phase == 0)
    def _():
        cnt_ref[...] += hits

    @pl.when((phase == 1) & first)
    def _():
        padded = jnp.ceil(cnt_ref[...] * (1.0 / tile)) * tile
        r_i = lax.broadcasted_iota(_I32, (CLASS_ROWS, CLASS_ROWS), 0)
        c_i = lax.broadcasted_iota(_I32, (CLASS_ROWS, CLASS_ROWS), 1)
        padded_row = jnp.sum(jnp.where(r_i == c_i, padded, 0.0), axis=0, keepdims=True)
        base_col = jnp.sum(jnp.where(c_i < r_i, padded_row, 0.0), axis=1, keepdims=True)
        base_row = jnp.sum(jnp.where(r_i < c_i, padded, 0.0), axis=0, keepdims=True)
        total = jnp.sum(padded_row, axis=1, keepdims=True)
        base_ref[...] = base_col
        run_ref[...] = jnp.zeros_like(run_ref)

        k_row = lax.broadcasted_iota(_I32, (1, n_tbl), 1).astype(_F32)
        start = k_row * tile
        inside = (base_col <= start) & (start < base_col + padded)
        cls_id = lax.broadcasted_iota(_I32, (CLASS_ROWS, n_tbl), 0).astype(_F32)
        tile_cls = jnp.sum(jnp.where(inside, cls_id, 0.0), axis=0, keepdims=True)
        n_used = total * (1.0 / tile)
        last = jnp.maximum(n_used - 1.0, 0.0)
        last_cls = jnp.sum(jnp.where(k_row == last, tile_cls, 0.0), axis=1, keepdims=True)
        used = k_row < n_used
        tile_cls = jnp.where(used, tile_cls, last_cls)
        grp = sum(jnp.where(tile_cls >= PAIRS_PER_GROUP * g, 1.0, 0.0) for g in range(1, N_EXPERT_GROUPS))
        pair = tile_cls - grp * PAIRS_PER_GROUP
        a = jnp.where(pair < 3, 0.0, jnp.where(pair < 5, 1.0, 2.0))
        bb = jnp.where(pair < 3, pair + 1.0, jnp.where(pair < 5, pair - 1.0, 3.0))
        last_tile_row = jnp.where(padded_row > 0, base_row + padded_row - tile, -1.0)
        last_tile_row = jnp.concatenate(
            [last_tile_row, jnp.full((1, n_tbl - CLASS_ROWS), -1.0, _F32)], axis=1)
        zero = jnp.zeros((1, n_tbl), _F32)
        tbl_ref[...] = jnp.concatenate(
            [grp * EXPERTS_PER_GROUP + a, grp * EXPERTS_PER_GROUP + bb, jnp.where(used, k_row, last),
             jnp.where(used, 1.0, 0.0), last_tile_row, zero, zero, zero], axis=0).astype(_I32)

    @pl.when(phase == 1)
    def _():
        before = lax.broadcasted_iota(_I32, (chunk, chunk), 0) < lax.broadcasted_iota(_I32, (chunk, chunk), 1)
        earlier = jnp.dot(jnp.where(onehot, 1.0, 0.0).astype(_BF16), jnp.where(before, 1.0, 0.0).astype(_BF16),
                          preferred_element_type=_F32)
        dest = base_ref[...] + run_ref[...] + earlier
        slot_ref[0] = jnp.sum(jnp.where(onehot, dest, 0.0), axis=0, keepdims=True).astype(_I32)
        run_ref[...] += hits


def _moe_plan(cls, tile):
    B, _, S = cls.shape
    chunk = PLAN_CHUNK
    n_tiles = (B * S) // tile + N_CLASSES
    n_tbl = -(-n_tiles // LANES) * LANES
    slot, tbl = pl.pallas_call(
        functools.partial(_plan_kernel, tile=tile),
        grid=(2, B, S // chunk),
        in_specs=[pl.BlockSpec((1, 1, chunk), lambda p, b, j: (b, 0, j))],
        out_specs=[pl.BlockSpec((1, 1, chunk), lambda p, b, j: (b * p, 0, j * p)),
                   pl.BlockSpec((SUBLANES, n_tbl), lambda p, b, j: (0, 0))],
        out_shape=[jax.ShapeDtypeStruct((B, 1, S), _I32), jax.ShapeDtypeStruct((SUBLANES, n_tbl), _I32)],
        scratch_shapes=[pltpu.VMEM((CLASS_ROWS, 1), _F32)] * 3,
        compiler_params=_params(("arbitrary", "arbitrary", "arbitrary")),
        name="moe_plan",
    )(cls)
    return slot, tbl, n_tiles


def _row_copy(src, src_row, dst, dst_row, sem):
    return pltpu.make_async_copy(src.at[pl.ds(pl.multiple_of(src_row * ROW_CHUNKS, ROW_CHUNKS), ROW_CHUNKS), :],
                                 dst.at[pl.ds(pl.multiple_of(dst_row * ROW_CHUNKS, ROW_CHUNKS), ROW_CHUNKS), :],
                                 sem)


def _dispatch_start(tbl_ref, slot_ref, h_ref, hs_ref, stage_ref, zero_ref, sem, zsem, *, tile):
    first = (pl.program_id(0) == 0) & (pl.program_id(1) == 0)
    tm = h_ref.shape[1]
    tile_rows = tile * ROW_CHUNKS

    @pl.when(first)
    def _():
        zero_ref[...] = jnp.zeros_like(zero_ref)
        n_tiles = hs_ref.shape[0] // tile_rows
        for wait in (False, True):
            for c in range(N_CLASSES):
                row = tbl_ref[TBL_LAST_TILE_ROW, c]

                @pl.when(row >= 0)
                def _():
                    at = pl.multiple_of(row * ROW_CHUNKS, ROW_CHUNKS)
                    cp = pltpu.make_async_copy(zero_ref, hs_ref.at[pl.ds(at, tile_rows), :], zsem)
                    cp.wait() if wait else cp.start()

            for k in range(n_tiles - N_CLASSES, n_tiles):
                @pl.when(tbl_ref[TBL_USED, k] == 0)
                def _():
                    cp = pltpu.make_async_copy(zero_ref, hs_ref.at[pl.ds(k * tile_rows, tile_rows), :], zsem)
                    cp.wait() if wait else cp.start()

    h = h_ref[0]
    for c in range(ROW_CHUNKS):
        stage_ref[pl.ds(c, tm, stride=ROW_CHUNKS), :] = h[:, c * LANES:(c + 1) * LANES].astype(_F32)

    def issue(g, carry):
        for k in range(ROWS_PER_TRIP):
            r = g * ROWS_PER_TRIP + k
            _row_copy(stage_ref, r, hs_ref, slot_ref[0, 0, r], sem).start(priority=k % 2)
        return carry

    lax.fori_loop(0, tm // ROWS_PER_TRIP, issue, 0)


def _dispatch_wait(stage_ref, hs_ref, sem):
    pltpu.make_async_copy(stage_ref, hs_ref.at[pl.ds(0, stage_ref.shape[0]), :], sem).wait()


def _launch(kernel_fn, grid, in_specs, out_specs, out_shape, scratch, args, name, rider=None):
    params = _params(("arbitrary",) * len(grid))
    if rider is None:
        return pl.pallas_call(kernel_fn, grid=grid, in_specs=in_specs, out_specs=out_specs, out_shape=out_shape,
                              scratch_shapes=scratch, compiler_params=params, name=name)(*args)
    tbl, slot, h2, n_tiles = rider
    tile = MOE_TILE
    steps = grid[0] * grid[1]
    Bo, So, _ = h2.shape
    share = (Bo * So) // steps
    per_seq = So // share
    n_in, n_out, n_scr = len(in_specs), len(out_specs), len(scratch)

    def drop_tbl(spec):
        if spec.index_map is None:
            return spec
        return pl.BlockSpec(spec.block_shape, lambda *a, f=spec.index_map: f(*a[:-1]), memory_space=spec.memory_space)

    def kern(tbl_ref, *refs):
        host_in, (slot_ref, h_ref) = refs[:n_in], refs[n_in:n_in + 2]
        host_out, hs_ref = refs[n_in + 2:n_in + 2 + n_out], refs[n_in + 2 + n_out]
        host_scr = refs[n_in + 3 + n_out:n_in + 3 + n_out + n_scr]
        stage_ref, zero_ref, sem, zsem = refs[n_in + 3 + n_out + n_scr:]
        _dispatch_start(tbl_ref, slot_ref, h_ref, hs_ref, stage_ref, zero_ref, sem, zsem, tile=tile)
        kernel_fn(*host_in, *host_out, *host_scr)
        _dispatch_wait(stage_ref, hs_ref, sem)

    step_of = lambda b, i: b * grid[1] + i
    rider_in = [pl.BlockSpec((1, 1, share), lambda b, i, t: (step_of(b, i) // per_seq, 0, step_of(b, i) % per_seq),
                             memory_space=pltpu.SMEM),
                pl.BlockSpec((1, share, D_MODEL), lambda b, i, t: (step_of(b, i) // per_seq, step_of(b, i) % per_seq, 0))]
    return pl.pallas_call(
        kern,
        grid_spec=pltpu.PrefetchScalarGridSpec(
            num_scalar_prefetch=1,
            grid=grid,
            in_specs=[drop_tbl(s) for s in in_specs] + rider_in,
            out_specs=[drop_tbl(s) for s in out_specs] + [pl.BlockSpec(memory_space=pl.ANY)],
            scratch_shapes=list(scratch) + [pltpu.VMEM((share * ROW_CHUNKS, LANES), _F32),
                                            pltpu.VMEM((tile * ROW_CHUNKS, LANES), _F32),
                                            pltpu.SemaphoreType.DMA(()), pltpu.SemaphoreType.DMA(())],
        ),
        out_shape=list(out_shape) + [jax.ShapeDtypeStruct((n_tiles * tile * ROW_CHUNKS, LANES), _F32)],
        compiler_params=params,
        name=name + "_dispatch",
    )(tbl, *args, slot, h2)


EXPERT_TILE_INPUTS = 11


def _experts_kernel(tbl_ref, *refs):
    n = EXPERT_TILE_INPUTS
    ins = (refs[:n], refs[n:2 * n])
    ys_ref = refs[2 * n]
    rows = ys_ref.shape[0] // 2
    k = pl.program_id(0)
    used = [tbl_ref[TBL_USED, 2 * k + j] for j in range(2)]

    def idle(j):
        ys_ref[j * rows:(j + 1) * rows, :] = jnp.zeros((rows, LANES), _F32)

    @pl.when(used[1] == 1)
    def _():
        for j in range(2):
            _expert_tile(*ins[j], ys_ref, j * rows)

    @pl.when((used[0] == 1) & (used[1] == 0))
    def _():
        _expert_tile(*ins[0], ys_ref, 0)
        idle(1)

    @pl.when(used[0] == 0)
    def _():
        idle(0)
        idle(1)


def _expert_tile(hs_ref, wra_hi_ref, wra_lo_ref, wrb_hi_ref, wrb_lo_ref,
                 wga_ref, wua_ref, wda_ref, wgb_ref, wub_ref, wdb_ref, ys_ref, row0):
    tile = hs_ref.shape[0] // ROW_CHUNKS
    chunks = [hs_ref[pl.ds(c, tile, stride=ROW_CHUNKS), :] for c in range(ROW_CHUNKS)]
    x = jnp.concatenate([ch.astype(_BF16) for ch in chunks], axis=1)

    def affinity(hi_ref, lo_ref):
        w = hi_ref[0] + lo_ref[0]
        part = chunks[0] * w[:, 0:LANES]
        for c in range(1, ROW_CHUNKS):
            part = part + chunks[c] * w[:, c * LANES:(c + 1) * LANES]
        return 1.0 / (1.0 + jnp.exp(-jnp.sum(part, axis=1, keepdims=True)))

    aff_a = affinity(wra_hi_ref, wra_lo_ref)
    aff_b = affinity(wrb_hi_ref, wrb_lo_ref)
    den = aff_a + aff_b

    def hidden(wg_ref, wu_ref, gate):
        a = jnp.dot(x, wg_ref[0], preferred_element_type=_F32)
        b = jnp.dot(x, wu_ref[0], preferred_element_type=_F32)
        return ((a / (1.0 + jnp.exp(-a))) * b * gate).astype(_BF16)

    y = (jnp.dot(hidden(wga_ref, wua_ref, aff_a / den), wda_ref[0], preferred_element_type=_F32)
         + jnp.dot(hidden(wgb_ref, wub_ref, aff_b / den), wdb_ref[0], preferred_element_type=_F32))
    for c in range(ROW_CHUNKS):
        ys_ref[pl.ds(row0 + c, tile, stride=ROW_CHUNKS), :] = y[:, c * LANES:(c + 1) * LANES]


def _moe_experts(tbl, hs, wr_hi, wr_lo, wg, wu, wd, layer, n_tiles, tile):
    rows = tile * ROW_CHUNKS
    first = layer * N_EXPERTS
    a, b = TBL_EXPERT_A, TBL_EXPERT_B
    in_specs = []
    for j in range(2):
        router = lambda row, j=j: pl.BlockSpec((1, 1, D_MODEL), lambda k, tbl: (tbl[row, 2 * k + j], 0, 0))
        up = lambda row, j=j: pl.BlockSpec((1, D_MODEL, D_EXPERT),
                                           lambda k, tbl: (first + tbl[row, 2 * k + j], 0, 0))
        down = lambda row, j=j: pl.BlockSpec((1, D_EXPERT, D_MODEL),
                                             lambda k, tbl: (first + tbl[row, 2 * k + j], 0, 0))
        in_specs += [pl.BlockSpec((rows, LANES), lambda k, tbl, j=j: (tbl[TBL_TILE, 2 * k + j], 0)),
                     router(a), router(a), router(b), router(b), up(a), up(a), down(a), up(b), up(b), down(b)]
    tile_args = (hs, wr_hi, wr_lo, wr_hi, wr_lo, wg, wu, wd, wg, wu, wd)
    return pl.pallas_call(
        _experts_kernel,
        grid_spec=pltpu.PrefetchScalarGridSpec(
            num_scalar_prefetch=1,
            grid=(n_tiles // 2,),
            in_specs=in_specs,
            out_specs=pl.BlockSpec((2 * rows, LANES), lambda k, tbl: (k, 0)),
        ),
        out_shape=jax.ShapeDtypeStruct(hs.shape, _F32),
        compiler_params=_params(("arbitrary",)),
        name="moe_experts",
    )(tbl, *tile_args, *tile_args)


def _combine_kernel(slot_ref, slot_next_ref, x1_ref, mod_ref, ys_ref, *refs, project):
    if project:
        (modn_ref, gain_ref, w_ref, cos_ref, sin_ref, x2_ref, *out_refs) = refs[:16]
        buf_ref, sem, stage_ref = refs[16:19]
    else:
        gfin_ref, x2_ref, buf_ref, sem = refs[:4]
    tm = x1_ref.shape[1]
    step = pl.program_id(0) * pl.num_programs(1) + pl.program_id(1)
    last = pl.num_programs(0) * pl.num_programs(1) - 1
    gate = mod_ref[0][5:6]

    def gather(sref):
        def issue(g, carry):
            for k in range(ROWS_PER_TRIP):
                r = g * ROWS_PER_TRIP + k
                _row_copy(ys_ref, sref[0, 0, r], buf_ref, r, sem).start(priority=k % 2)
            return carry
        lax.fori_loop(0, tm // ROWS_PER_TRIP, issue, 0)

    @pl.when(step == 0)
    def _():
        gather(slot_ref)

    pltpu.make_async_copy(ys_ref.at[pl.ds(0, tm * ROW_CHUNKS), :], buf_ref, sem).wait()
    for c in range(ROW_CHUNKS):
        cols = slice(c * LANES, (c + 1) * LANES)
        x2_ref[0, :, cols] = x1_ref[0, :, cols] + gate[:, cols] * buf_ref[pl.ds(c, tm, stride=ROW_CHUNKS), :]

    @pl.when(step < last)
    def _():
        gather(slot_next_ref)

    x2 = x2_ref[0]
    if project:
        _project(x2, 0, modn_ref[0], gain_ref, w_ref, cos_ref, sin_ref, out_refs[0:3], out_refs[3:6],
                 out_refs[6:9], out_refs[9], stage_ref)
    else:
        x2_ref[0] = x2 * lax.rsqrt(jnp.mean(x2 * x2, axis=-1, keepdims=True) + RMS_EPS) * gfin_ref[...]


def _moe_combine(slot, x1, mod_prev, ys, batch_offset, tail_args, rider=None):
    B, S, _ = x1.shape
    tm = TOKEN_TILE
    n_i = S // tm
    project = len(tail_args) > 1

    def next_step(b, i):
        n = jnp.minimum(b * n_i + i + 1, B * n_i - 1)
        return n // n_i, 0, n % n_i

    tok = pl.BlockSpec((1, tm, D_MODEL), lambda b, i: (b, i, 0))
    mod_spec = pl.BlockSpec((1, 6, D_MODEL), lambda b, i: (b + batch_offset, 0, 0))
    in_specs = [pl.BlockSpec((1, 1, tm), lambda b, i: (b, 0, i), memory_space=pltpu.SMEM),
                pl.BlockSpec((1, 1, tm), next_step, memory_space=pltpu.SMEM),
                tok, mod_spec, pl.BlockSpec(memory_space=pl.ANY)]
    out_specs, out_shape = [tok], [jax.ShapeDtypeStruct((B, S, D_MODEL), _F32)]
    scratch = [pltpu.VMEM((tm * ROW_CHUNKS, LANES), _F32), pltpu.SemaphoreType.DMA(())]
    if project:
        in_specs += [mod_spec,
                     pl.BlockSpec((1, D_MODEL), lambda b, i: (0, 0)),
                     pl.BlockSpec((D_MODEL, IN_W), lambda b, i: (0, 0)),
                     pl.BlockSpec((tm, LANES), lambda b, i: (i, 0)),
                     pl.BlockSpec((tm, LANES), lambda b, i: (i, 0))]
        out_specs += [pl.BlockSpec((1, tm // d, d * GROUP_W), lambda b, i: (b, i, 0)) for _, d in DILATED_GROUPS] * 3
        out_specs += [pl.BlockSpec((1, tm, POOL_W), lambda b, i: (b, i, 0))]
        out_shape += [jax.ShapeDtypeStruct((B, S // d, d * GROUP_W), _BF16) for _, d in DILATED_GROUPS] * 3
        out_shape += [jax.ShapeDtypeStruct((B, S, POOL_W), _F32)]
        scratch += [pltpu.VMEM((tm, LANES), _F32)]
    else:
        in_specs += [pl.BlockSpec((1, D_MODEL), lambda b, i: (0, 0))]
    return _launch(
        functools.partial(_combine_kernel, project=project),
        grid=(B, n_i),
        in_specs=in_specs,
        out_specs=out_specs,
        out_shape=out_shape,
        scratch=scratch,
        args=(slot, slot, x1, mod_prev, ys, *tail_args),
        name="moe_combine_in_proj" if project else "moe_combine_norm",
        rider=rider,
    )


def _rope_tables(S):
    inv = 1.0 / (ROPE_THETA ** (jnp.arange(0, HEAD_DIM, 2, dtype=_F32) / HEAD_DIM))
    ang = jnp.arange(S, dtype=_F32)[:, None] * inv[None, :]
    cos, sin = jnp.cos(ang), jnp.sin(ang)
    reps = LANES // HEAD_DIM
    return (jnp.tile(jnp.concatenate([cos, cos], axis=1), (1, reps)),
            jnp.tile(jnp.concatenate([-sin, sin], axis=1), (1, reps)))


def _trunks(xs, batch_offsets, mod, w):
    tables = [_rope_tables(x.shape[1]) for x in xs]
    outs = [_in_proj(x, mod[0], w["norm_mix"][0], w["w_in"][0], *tables[t], batch_offsets[t])
            for t, x in enumerate(xs)]
    xs = list(xs)
    for l in range(DEPTH):
        mod_l = mod[l]

        def mix(t, rider):
            q, k, v, u = outs[t][0:3], outs[t][3:6], outs[t][6:9], outs[t][9]
            attn, lses = [], []
            for g, (window, dilation) in enumerate(DILATED_GROUPS):
                o, lse = _attention(q[g], k[g], v[g], window, dilation)
                attn.append(o)
                lses.append(lse)
            return _mix_out(xs[t], attn, lses, u, mod_l, w["wpool_bd"][l], w["pool_scale"][l], w["w_out"][l],
                            w["norm_ffn"][l], w["wr_hi"], w["wr_lo"], w["rbias"], batch_offsets[t], rider)

        def experts(tbl, hs, n_tiles):
            return _moe_experts(tbl, hs, w["wr_hi3"], w["wr_lo3"], w["w_gate"], w["w_up"], w["w_down"], l,
                                n_tiles, MOE_TILE)

        def combine(t, slot, x1, ys, rider):
            if l + 1 < DEPTH:
                tail = (mod[l + 1], w["norm_mix"][l + 1], w["w_in"][l + 1], *tables[t])
            else:
                tail = (w["norm_final"],)
            return _moe_combine(slot, x1, mod_l, ys, batch_offsets[t], tail, rider)

        x1_0, h2_0, cls_0 = mix(0, None)
        slot_0, tbl_0, n_0 = _moe_plan(cls_0, MOE_TILE)
        x1_1, h2_1, cls_1, hs_0 = mix(1, (tbl_0, slot_0, h2_0, n_0))
        slot_1, tbl_1, n_1 = _moe_plan(cls_1, MOE_TILE)
        ys_0 = experts(tbl_0, hs_0, n_0)
        xs[0], *rest = combine(0, slot_0, x1_0, ys_0, (tbl_1, slot_1, h2_1, n_1))
        outs[0], hs_1 = rest[:-1], rest[-1]
        ys_1 = experts(tbl_1, hs_1, n_1)
        xs[1], *outs[1] = combine(1, slot_1, x1_1, ys_1, None)
    return xs


def kernel(x_prompt, x_sample, c_prompt, c_sample, norm_mix, w_mod, b_mod, w_in, w_pool, pool_scale, w_out,
           norm_ffn, w_router, router_bias, w_gate, w_up, w_down, norm_final):
    n_prompt = x_prompt.shape[0]
    c = jnp.concatenate([c_prompt, c_sample], axis=0)
    mod = _modulation(c, w_mod, b_mod).reshape(DEPTH, c.shape[0], 6, D_MODEL)
    n_pool = len(POOL_WINDOWS)
    eye = jnp.eye(n_pool, dtype=w_pool.dtype)
    wpool_bd = (w_pool[:, :, :, None, :] * eye[None, :, None, :, None]).reshape(DEPTH, POOL_W, POOL_W)
    wr_t = w_router.T.astype(_F32)
    wr_hi = wr_t.astype(_BF16)
    wr_lo = (wr_t - wr_hi.astype(_F32)).astype(_BF16)
    w = {
        "norm_mix": norm_mix.reshape(DEPTH, 1, D_MODEL),
        "norm_ffn": norm_ffn.reshape(DEPTH, 1, D_MODEL),
        "norm_final": norm_final.reshape(1, D_MODEL),
        "w_in": w_in.astype(_BF16),
        "w_out": w_out.astype(_BF16),
        "wpool_bd": wpool_bd.astype(_BF16),
        "pool_scale": pool_scale.reshape(DEPTH, 1, POOL_W),
        "wr_hi": wr_hi,
        "wr_lo": wr_lo,
        "wr_hi3": wr_hi.astype(_F32).reshape(N_EXPERTS, 1, D_MODEL),
        "wr_lo3": wr_lo.astype(_F32).reshape(N_EXPERTS, 1, D_MODEL),
        "rbias": router_bias.astype(_F32).reshape(N_EXPERTS, 1),
        "w_gate": w_gate.astype(_BF16).reshape(DEPTH * N_EXPERTS, D_MODEL, D_EXPERT),
        "w_up": w_up.astype(_BF16).reshape(DEPTH * N_EXPERTS, D_MODEL, D_EXPERT),
        "w_down": w_down.astype(_BF16).reshape(DEPTH * N_EXPERTS, D_EXPERT, D_MODEL),
    }
    y_prompt, y_sample = _trunks((x_prompt, x_sample), (0, n_prompt), mod, w)
    return (y_prompt, y_sample)
```

```python
import functools

import jax
import jax.numpy as jnp
from jax import lax
from jax.experimental import pallas as pl
from jax.experimental.pallas import tpu as pltpu

D_MODEL = 1024
DEPTH = 4
HEAD_DIM = 64
HALF_HEAD = HEAD_DIM // 2
DILATED_GROUPS = ((128, 1), (512, 4), (2048, 16))
N_GROUPS = len(DILATED_GROUPS)
HEADS_PER_GROUP = 4
GROUP_W = HEADS_PER_GROUP * HEAD_DIM
ATT_W = N_GROUPS * GROUP_W
POOL_WINDOWS = (2, 4, 8, 16)
POOL_GROUP_W = 64
POOL_W = POOL_GROUP_W * len(POOL_WINDOWS)
IN_W = 3 * ATT_W + POOL_W
ROPE_THETA = 10000.0
RMS_EPS = 1e-6
N_EXPERTS = 16
N_EXPERT_GROUPS = 4
EXPERTS_PER_GROUP = N_EXPERTS // N_EXPERT_GROUPS
PAIRS_PER_GROUP = EXPERTS_PER_GROUP * (EXPERTS_PER_GROUP - 1) // 2
N_CLASSES = N_EXPERT_GROUPS * PAIRS_PER_GROUP
D_EXPERT = 512

LANES = 128
SUBLANES = 8
Q_BLOCK = 128
VMEM_LIMIT = 48 * 1024 * 1024

TOKEN_TILE = 512
MOE_TILE = 256
MIX_TILE = 1024
MIX_CHUNKS = 2
PLAN_CHUNK = 512
CLASS_ROWS = 32
ROW_CHUNKS = D_MODEL // LANES
ROWS_PER_TRIP = 32

_BF16 = jnp.bfloat16
_F32 = jnp.float32
_I32 = jnp.int32


def _params(semantics):
    return pltpu.CompilerParams(dimension_semantics=semantics, vmem_limit_bytes=VMEM_LIMIT)


def _mod_kernel(c_ref, w_ref, b_ref, o_ref):
    c = c_ref[...]
    sc = c / (1.0 + jnp.exp(-c))
    o_ref[0] = jnp.dot(sc, w_ref[0], preferred_element_type=_F32,
                       precision=lax.Precision.HIGHEST) + b_ref[0]


def _modulation(c, w_mod, b_mod):
    nb = c.shape[0]
    col = D_MODEL
    n_col = w_mod.shape[2] // col
    return pl.pallas_call(
        _mod_kernel,
        grid=(DEPTH, n_col),
        in_specs=[
            pl.BlockSpec((nb, D_MODEL), lambda l, j: (0, 0)),
            pl.BlockSpec((1, D_MODEL, col), lambda l, j: (l, 0, j)),
            pl.BlockSpec((1, 1, col), lambda l, j: (l, 0, j)),
        ],
        out_specs=pl.BlockSpec((1, nb, col), lambda l, j: (l, 0, j)),
        out_shape=jax.ShapeDtypeStruct((DEPTH, nb, w_mod.shape[2]), _F32),
        compiler_params=_params(("arbitrary", "arbitrary")),
        name="modulation",
    )(c, w_mod, b_mod.reshape(DEPTH, 1, -1))


def _rmsnorm_mod(x, gain, scale, shift):
    y = x * lax.rsqrt(jnp.mean(x * x, axis=-1, keepdims=True) + RMS_EPS)
    return (y * gain) * (1.0 + scale) + shift


def _rope_chunk(t, cos, sin_signed, first_half):
    fwd = pltpu.roll(t, HALF_HEAD, axis=1)
    bwd = pltpu.roll(t, LANES - HALF_HEAD, axis=1)
    return t * cos + jnp.where(first_half, bwd, fwd) * sin_signed


def _store_by_residue(out_ref, chunk, value, dilation, stage_ref, row0):
    rows = value.shape[0] // dilation
    at = slice(row0 // dilation, row0 // dilation + rows)
    if dilation == 1:
        out_ref[0, at, chunk * LANES:(chunk + 1) * LANES] = value.astype(_BF16)
        return
    stage_ref[...] = value
    for r in range(dilation):
        lo = r * GROUP_W + chunk * LANES
        out_ref[0, at, lo:lo + LANES] = stage_ref[pl.ds(r, rows, stride=dilation), :].astype(_BF16)


def _in_proj_kernel(x_ref, mod_ref, gain_ref, w_ref, cos_ref, sin_ref, *refs):
    _project(x_ref[0], 0, mod_ref[0], gain_ref, w_ref, cos_ref, sin_ref, refs[0:3], refs[3:6], refs[6:9], refs[9],
             refs[10])


def _project(x, row0, mod, gain_ref, w_ref, cos_ref, sin_ref, q_refs, k_refs, v_refs, u_ref, stage_ref,
             between=None):
    n = x.shape[0]
    n_chunks = N_GROUPS * (GROUP_W // LANES)
    h = _rmsnorm_mod(x, gain_ref[...], mod[1:2], mod[0:1])
    proj = jnp.dot(h.astype(_BF16), w_ref[...], preferred_element_type=_F32)
    cos = cos_ref[row0:row0 + n, :]
    sin_signed = sin_ref[row0:row0 + n, :]
    lane = lax.broadcasted_iota(_I32, (1, LANES), 1)
    first_half = (lane % HEAD_DIM) < HALF_HEAD
    q_scale = HEAD_DIM ** -0.5
    for g, (_, dilation) in enumerate(DILATED_GROUPS):
        for c in range(GROUP_W // LANES):
            lo = g * GROUP_W + c * LANES
            if between is not None:
                between(g * (GROUP_W // LANES) + c, n_chunks)
            q = _rope_chunk(proj[:, lo:lo + LANES], cos, sin_signed, first_half)
            _store_by_residue(q_refs[g], c, q * q_scale, dilation, stage_ref, row0)
            k = _rope_chunk(proj[:, ATT_W + lo:ATT_W + lo + LANES], cos, sin_signed, first_half)
            _store_by_residue(k_refs[g], c, k, dilation, stage_ref, row0)
            _store_by_residue(v_refs[g], c, proj[:, 2 * ATT_W + lo:2 * ATT_W + lo + LANES], dilation, stage_ref,
                              row0)
    u_ref[0, row0:row0 + n, :] = proj[:, 3 * ATT_W:]


def _in_proj(x, mod_l, gain, w_in_bf16, cos, sin_signed, batch_offset):
    B, S, _ = x.shape
    tm = TOKEN_TILE
    grp = [jax.ShapeDtypeStruct((B, S // d, d * GROUP_W), _BF16) for _, d in DILATED_GROUPS]
    grp_spec = [pl.BlockSpec((1, tm // d, d * GROUP_W), lambda b, i: (b, i, 0)) for _, d in DILATED_GROUPS]
    return pl.pallas_call(
        _in_proj_kernel,
        grid=(B, S // tm),
        in_specs=[
            pl.BlockSpec((1, tm, D_MODEL), lambda b, i: (b, i, 0)),
            pl.BlockSpec((1, 6, D_MODEL), lambda b, i: (b + batch_offset, 0, 0)),
            pl.BlockSpec((1, D_MODEL), lambda b, i: (0, 0)),
            pl.BlockSpec((D_MODEL, IN_W), lambda b, i: (0, 0)),
            pl.BlockSpec((tm, LANES), lambda b, i: (i, 0)),
            pl.BlockSpec((tm, LANES), lambda b, i: (i, 0)),
        ],
        out_specs=grp_spec * 3 + [pl.BlockSpec((1, tm, POOL_W), lambda b, i: (b, i, 0))],
        out_shape=grp * 3 + [jax.ShapeDtypeStruct((B, S, POOL_W), _F32)],
        scratch_shapes=[pltpu.VMEM((tm, LANES), _F32)],
        compiler_params=_params(("arbitrary", "arbitrary")),
        name="in_proj",
    )(x, mod_l, gain, w_in_bf16, cos, sin_signed)


def _attn_kernel(q_ref, k_ref, v_ref, o_ref, lse_ref, bias_ref, *, length, n_side, n_res, q_unroll):
    rb = pl.program_id(1)
    key_w = min(length, Q_BLOCK + 2 * n_side)
    n_qb = length // Q_BLOCK

    @pl.when((pl.program_id(0) == 0) & (rb == 0))
    def _():
        i = lax.broadcasted_iota(_I32, (2 * Q_BLOCK, key_w), 0) % Q_BLOCK
        j = lax.broadcasted_iota(_I32, (2 * Q_BLOCK, key_w), 1)
        for n in range(3):
            bias_ref[n] = jnp.where(jnp.abs(i - j + n * n_side) <= n_side, 0.0, -jnp.inf)

    @pl.when(rb == 0)
    def _():
        lse_ref[...] = jnp.zeros_like(lse_ref)

    lane = lax.broadcasted_iota(_I32, (1, LANES), 1)
    head_mask = [(lane < HEAD_DIM).astype(_BF16), (lane >= HEAD_DIM).astype(_BF16)]
    lse_lane = lax.broadcasted_iota(_I32, (1, lse_ref.shape[2]), 1)

    def trip(it, carry):
        for u in range(q_unroll):
            q0 = pl.multiple_of((it * q_unroll + u) * Q_BLOCK, Q_BLOCK)
            k0 = pl.multiple_of(jnp.clip(q0 - n_side, 0, length - key_w), n_side)
            bias = bias_ref[(q0 - k0) // n_side]
            lse_rows = lse_ref[0, pl.ds(q0, Q_BLOCK), :]
            for rr in range(n_res):
                for pair in range(GROUP_W // LANES):
                    lo = rr * GROUP_W + pair * LANES
                    cols = slice(lo, lo + LANES)
                    q2 = q_ref[0, pl.ds(q0, Q_BLOCK), cols]
                    k2 = k_ref[0, pl.ds(k0, key_w), cols]
                    v2 = v_ref[0, pl.ds(k0, key_w), cols]
                    qs = jnp.concatenate([q2 * head_mask[0], q2 * head_mask[1]], axis=0)
                    s = lax.dot_general(qs, k2, (((1,), (1,)), ((), ())), preferred_element_type=_F32) + bias
                    m = jnp.max(s, axis=1, keepdims=True)
                    p = jnp.exp(s - m)
                    l = jnp.sum(p, axis=1, keepdims=True)
                    o = jnp.dot(p.astype(_BF16), v2, preferred_element_type=_F32) * (1.0 / l)
                    lse = m + jnp.log(l)
                    for hh in range(2):
                        col = (rb * n_res + rr) * HEADS_PER_GROUP + pair * 2 + hh
                        lse_rows = jnp.where(lse_lane == col, lse[hh * Q_BLOCK:(hh + 1) * Q_BLOCK], lse_rows)
                    o_ref[0, pl.ds(q0, Q_BLOCK), cols] = jnp.where(lane < HEAD_DIM, o[:Q_BLOCK], o[Q_BLOCK:]).astype(_BF16)
            lse_ref[0, pl.ds(q0, Q_BLOCK), :] = lse_rows
        return carry

    lax.fori_loop(0, n_qb // q_unroll, trip, 0)


ATTN_BLOCKS_PER_TRIP = 16


def _attention(q, k, v, window, dilation):
    B, L, _ = q.shape
    d = dilation
    n_side = window // (2 * d)
    n_res = min(d, ATTN_BLOCKS_PER_TRIP)
    q_unroll = min(ATTN_BLOCKS_PER_TRIP // n_res, L // Q_BLOCK)
    key_w = min(L, Q_BLOCK + 2 * n_side)
    spec = pl.BlockSpec((1, L, n_res * GROUP_W), lambda b, r: (b, 0, r))
    return pl.pallas_call(
        functools.partial(_attn_kernel, length=L, n_side=n_side, n_res=n_res, q_unroll=q_unroll),
        grid=(B, d // n_res),
        in_specs=[spec, spec, spec],
        out_specs=[spec, pl.BlockSpec((1, L, d * HEADS_PER_GROUP), lambda b, r: (b, 0, 0))],
        out_shape=[jax.ShapeDtypeStruct((B, L, d * GROUP_W), _BF16),
                   jax.ShapeDtypeStruct((B, L, d * HEADS_PER_GROUP), _F32)],
        scratch_shapes=[pltpu.VMEM((3, 2 * Q_BLOCK, key_w), _F32)],
        compiler_params=_params(("arbitrary", "arbitrary")),
        name=f"attention_d{d}",
    )(q, k, v)


def _route(sel):
    s = [sel[e:e + 1, :] for e in range(N_EXPERTS)]
    group_score = []
    for g in range(N_EXPERT_GROUPS):
        c0, c1, c2, c3 = s[EXPERTS_PER_GROUP * g:EXPERTS_PER_GROUP * (g + 1)]
        m1, n1 = jnp.maximum(c0, c1), jnp.minimum(c0, c1)
        m2, n2 = jnp.maximum(c2, c3), jnp.minimum(c2, c3)
        group_score.append(jnp.maximum(m1, m2) + jnp.maximum(jnp.minimum(m1, m2), jnp.maximum(n1, n2)))
    best, top_g = group_score[0], jnp.zeros(group_score[0].shape, _I32)
    for g in range(1, N_EXPERT_GROUPS):
        better = group_score[g] > best
        best = jnp.where(better, group_score[g], best)
        top_g = jnp.where(better, g, top_g)
    cs = []
    for j in range(EXPERTS_PER_GROUP):
        c = s[j]
        for g in range(1, N_EXPERT_GROUPS):
            c = jnp.where(top_g == g, s[EXPERTS_PER_GROUP * g + j], c)
        cs.append(c)
    s1, i1 = cs[0], jnp.zeros_like(top_g)
    for j in range(1, EXPERTS_PER_GROUP):
        better = cs[j] > s1
        s1 = jnp.where(better, cs[j], s1)
        i1 = jnp.where(better, j, i1)
    i2 = jnp.where(i1 == 0, 1, 0)
    s2 = jnp.where(i1 == 0, cs[1], cs[0])
    for j in range(1, EXPERTS_PER_GROUP):
        better = (cs[j] > s2) & (i1 != j)
        s2 = jnp.where(better, cs[j], s2)
        i2 = jnp.where(better, j, i2)
    lo, hi = jnp.minimum(i1, i2), jnp.maximum(i1, i2)
    pair = jnp.where(lo == 0, 0, jnp.where(lo == 1, 3, 5)) + (hi - lo - 1)
    return top_g * PAIRS_PER_GROUP + pair


def _mix_out_kernel(x_ref, o0_ref, o1_ref, o2_ref, l0_ref, l1_ref, l2_ref, u_ref, up_ref, un_ref,
                    mod_ref, wpool_ref, pscale_ref, wout_ref, gain_ref, wr_hi_ref, wr_lo_ref, rbias_ref,
                    x1_ref, h2_ref, cls_ref, ext_ref, lvl_a_ref, lvl_b_ref, lvl_c_ref, nat_o1a_ref, nat_o1b_ref,
                    nat_o2a_ref, nat_o2b_ref, nat_l1_ref, nat_l2_ref, *, seq_len, rider_between=None):
    nat_o_refs = ((nat_o1a_ref, nat_o1b_ref), (nat_o2a_ref, nat_o2b_ref))
    nat_l_refs = (nat_l1_ref, nat_l2_ref)
    i = pl.program_id(1)
    n_i = pl.num_programs(1)
    tm = x_ref.shape[1]
    mod = mod_ref[0]

    n_slices = sum(d for _, d in DILATED_GROUPS if d > 1)
    done = 0
    for g, (o_ref, l_ref) in enumerate(((o0_ref, l0_ref), (o1_ref, l1_ref), (o2_ref, l2_ref))):
        dilation = DILATED_GROUPS[g][1]
        if dilation == 1:
            continue
        rows = tm // dilation
        lv = l_ref[0]
        lv = jnp.concatenate([lv, jnp.zeros((rows, LANES - lv.shape[1]), _F32)], axis=1)
        halves = nat_o_refs[g - 1]
        for r in range(dilation):
            if rider_between is not None:
                rider_between(done, n_slices)
            done += 1
            for c, half_ref in enumerate(halves):
                lo = r * GROUP_W + c * LANES
                half_ref[pl.ds(r, rows, stride=dilation), :] = o_ref[0, :, lo:lo + LANES].astype(_F32)
            shifted = lv if r == 0 else pltpu.roll(lv, LANES - r * HEADS_PER_GROUP, axis=1)
            nat_l_refs[g - 1][pl.ds(r, rows, stride=dilation), :] = shifted

    pad = 2 * SUBLANES
    zeros_pad = jnp.zeros((SUBLANES, POOL_W), _F32)
    for ref in (ext_ref, lvl_a_ref, lvl_b_ref, lvl_c_ref):
        ref[0:SUBLANES, :] = zeros_pad
        ref[tm + pad + SUBLANES:tm + 2 * pad, :] = zeros_pad
    ext_ref[SUBLANES:pad, :] = jnp.where(i > 0, up_ref[0], 0.0)
    ext_ref[pad:pad + tm, :] = u_ref[0]
    ext_ref[pad + tm:pad + tm + SUBLANES, :] = jnp.where(i < n_i - 1, un_ref[0], 0.0)
    span = tm + 2 * SUBLANES
    win = lambda ref, off: ref[SUBLANES + off:SUBLANES + off + span, :]
    lvl_a_ref[SUBLANES:SUBLANES + span, :] = win(ext_ref, -1) + win(ext_ref, 0)
    lvl_b_ref[SUBLANES:SUBLANES + span, :] = win(lvl_a_ref, -1) + win(lvl_a_ref, 1)
    lvl_c_ref[SUBLANES:SUBLANES + span, :] = win(lvl_b_ref, -2) + win(lvl_b_ref, 2)
    lane = lax.broadcasted_iota(_I32, (1, POOL_W), 1)
    wgroup = lane // POOL_GROUP_W
    half_w = jnp.where(wgroup == 0, 1, jnp.where(wgroup == 1, 2, jnp.where(wgroup == 2, 4, 8)))

    half = tm // MIX_CHUNKS
    for hf in range(MIX_CHUNKS):
        r0 = hf * half
        rows = slice(r0, r0 + half)
        outs = [o0_ref[0, rows, :].astype(_F32)]
        lses = [l0_ref[0, rows, :]]
        for g in range(1, N_GROUPS):
            outs.append(jnp.concatenate([ref[rows, :] for ref in nat_o_refs[g - 1]], axis=1))
            lses.append(nat_l_refs[g - 1][rows, 0:HEADS_PER_GROUP])

        top = jnp.maximum(jnp.maximum(lses[0], lses[1]), lses[2])
        es = [jnp.exp(t - top) for t in lses]
        den = es[0] + es[1] + es[2]
        alpha = jnp.concatenate([e / den for e in es]
                                + [jnp.zeros((half, LANES - N_GROUPS * HEADS_PER_GROUP), _F32)], axis=1)
        a_hi = alpha.astype(_BF16)
        a_lo = (alpha - a_hi.astype(_F32)).astype(_BF16)
        head_of_lane = lax.broadcasted_iota(_I32, (LANES, ATT_W), 1) // HEAD_DIM
        spread = jnp.where(head_of_lane == lax.broadcasted_iota(_I32, (LANES, ATT_W), 0), 1.0, 0.0).astype(_BF16)
        wide = (jnp.dot(a_hi, spread, preferred_element_type=_F32)
                + jnp.dot(a_lo, spread, preferred_element_type=_F32))
        pieces = [(outs[g] * wide[:, g * GROUP_W:(g + 1) * GROUP_W]).astype(_BF16) for g in range(N_GROUPS)]

        s2 = lvl_a_ref[pad + r0:pad + r0 + half, :]
        s4 = lvl_b_ref[pad + r0:pad + r0 + half, :]
        s8 = lvl_c_ref[pad + r0:pad + r0 + half, :]
        s16 = lvl_c_ref[pad + r0 - 4:pad + r0 - 4 + half, :] + lvl_c_ref[pad + r0 + 4:pad + r0 + 4 + half, :]
        wsum = jnp.where(wgroup == 0, s2, jnp.where(wgroup == 1, s4, jnp.where(wgroup == 2, s8, s16)))
        pos = i * tm + r0 + lax.broadcasted_iota(_I32, (half, 1), 0)
        cnt = jnp.minimum(pos + half_w, seq_len) - jnp.maximum(pos - half_w, 0)
        z = wsum / cnt.astype(_F32) - u_ref[0, rows, :]
        pool = jnp.dot(z.astype(_BF16), wpool_ref[...], preferred_element_type=_F32) * pscale_ref[...]
        pieces.append(pool.astype(_BF16))

        mix = jnp.dot(jnp.concatenate(pieces, axis=1), wout_ref[...], preferred_element_type=_F32)
        x1 = x_ref[0, rows, :] + mod[2:3] * mix
        x1_ref[0, rows, :] = x1

        h2 = _rmsnorm_mod(x1, gain_ref[...], mod[4:5], mod[3:4])
        h_hi = h2.astype(_BF16)
        h2_ref[0, rows, :] = h_hi
        h_lo = (h2 - h_hi.astype(_F32)).astype(_BF16)
        nt = (((1,), (1,)), ((), ()))
        logits = (lax.dot_general(wr_hi_ref[...], h_hi, nt, preferred_element_type=_F32)
                  + lax.dot_general(wr_lo_ref[...], h_hi, nt, preferred_element_type=_F32)
                  + lax.dot_general(wr_hi_ref[...], h_lo, nt, preferred_element_type=_F32))
        aff = 1.0 / (1.0 + jnp.exp(-logits))
        cls_ref[0, :, rows] = _route(aff + rbias_ref[...])


def _mix_out(x, attn, lses, u, mod_l, wpool_bd, pool_scale, w_out_bf16, gain, wr_hi, wr_lo, rbias, batch_offset,
             rider=None):
    B, S, _ = x.shape
    tm = MIX_TILE
    n_i = S // tm
    halo_blocks = tm // SUBLANES
    tok = lambda w: pl.BlockSpec((1, tm, w), lambda b, i: (b, i, 0))
    const = lambda shape: pl.BlockSpec(shape, lambda b, i: tuple(0 for _ in shape))
    rows = tm + 4 * SUBLANES
    return _launch(
        functools.partial(_mix_out_kernel, seq_len=S),
        grid=(B, n_i),
        in_specs=[
            tok(D_MODEL),
            *[pl.BlockSpec((1, tm // d, d * GROUP_W), lambda b, i: (b, i, 0)) for _, d in DILATED_GROUPS],
            *[pl.BlockSpec((1, tm // d, d * HEADS_PER_GROUP), lambda b, i: (b, i, 0)) for _, d in DILATED_GROUPS],
            tok(POOL_W),
            pl.BlockSpec((1, SUBLANES, POOL_W), lambda b, i: (b, jnp.maximum(i * halo_blocks - 1, 0), 0)),
            pl.BlockSpec((1, SUBLANES, POOL_W),
                         lambda b, i: (b, jnp.minimum((i + 1) * halo_blocks, S // SUBLANES - 1), 0)),
            pl.BlockSpec((1, 6, D_MODEL), lambda b, i: (b + batch_offset, 0, 0)),
            const((POOL_W, POOL_W)), const((1, POOL_W)), const((D_MODEL, D_MODEL)), const((1, D_MODEL)),
            const((N_EXPERTS, D_MODEL)), const((N_EXPERTS, D_MODEL)), const((N_EXPERTS, 1)),
        ],
        out_specs=[tok(D_MODEL), tok(D_MODEL), pl.BlockSpec((1, 1, tm), lambda b, i: (b, 0, i))],
        out_shape=[jax.ShapeDtypeStruct((B, S, D_MODEL), _F32),
                   jax.ShapeDtypeStruct((B, S, D_MODEL), _BF16),
                   jax.ShapeDtypeStruct((B, 1, S), _I32)],
        scratch=[pltpu.VMEM((rows, POOL_W), _F32)] * 4 + [pltpu.VMEM((tm, LANES), _F32)] * 6,
        args=(x, *attn, *lses, u, u, u, mod_l, wpool_bd, pool_scale, w_out_bf16, gain, wr_hi, wr_lo, rbias),
        name="mix_out",
        rider=rider,
    )


TBL_EXPERT_A, TBL_EXPERT_B, TBL_TILE, TBL_USED, TBL_LAST_TILE_ROW = range(5)


def _plan_kernel(cls_ref, slot_ref, tbl_ref, cnt_ref, run_ref, base_ref, *, tile):
    phase, b, j = pl.program_id(0), pl.program_id(1), pl.program_id(2)
    first = (b == 0) & (j == 0)
    chunk = cls_ref.shape[2]
    n_tbl = tbl_ref.shape[1]
    onehot = lax.broadcasted_iota(_I32, (CLASS_ROWS, chunk), 0) == cls_ref[0]
    hits = jnp.sum(onehot.astype(_F32), axis=1, keepdims=True)

    @pl.when((phase == 0) & first)
    def _():
        cnt_ref[...] = jnp.zeros_like(cnt_ref)

    @pl.when(phase == 0)
    def _():
        cnt_ref[...] += hits

    @pl.when((phase == 1) & first)
    def _():
        padded = jnp.ceil(cnt_ref[...] * (1.0 / tile)) * tile
        r_i = lax.broadcasted_iota(_I32, (CLASS_ROWS, CLASS_ROWS), 0)
        c_i = lax.broadcasted_iota(_I32, (CLASS_ROWS, CLASS_ROWS), 1)
        padded_row = jnp.sum(jnp.where(r_i == c_i, padded, 0.0), axis=0, keepdims=True)
        base_col = jnp.sum(jnp.where(c_i < r_i, padded_row, 0.0), axis=1, keepdims=True)
        base_row = jnp.sum(jnp.where(r_i < c_i, padded, 0.0), axis=0, keepdims=True)
        total = jnp.sum(padded_row, axis=1, keepdims=True)
        base_ref[...] = base_col
        run_ref[...] = jnp.zeros_like(run_ref)

        k_row = lax.broadcasted_iota(_I32, (1, n_tbl), 1).astype(_F32)
        start = k_row * tile
        inside = (base_col <= start) & (start < base_col + padded)
        cls_id = lax.broadcasted_iota(_I32, (CLASS_ROWS, n_tbl), 0).astype(_F32)
        tile_cls = jnp.sum(jnp.where(inside, cls_id, 0.0), axis=0, keepdims=True)
        n_used = total * (1.0 / tile)
        last = jnp.maximum(n_used - 1.0, 0.0)
        last_cls = jnp.sum(jnp.where(k_row == last, tile_cls, 0.0), axis=1, keepdims=True)
        used = k_row < n_used
        tile_cls = jnp.where(used, tile_cls, last_cls)
        grp = sum(jnp.where(tile_cls >= PAIRS_PER_GROUP * g, 1.0, 0.0) for g in range(1, N_EXPERT_GROUPS))
        pair = tile_cls - grp * PAIRS_PER_GROUP
        a = jnp.where(pair < 3, 0.0, jnp.where(pair < 5, 1.0, 2.0))
        bb = jnp.where(pair < 3, pair + 1.0, jnp.where(pair < 5, pair - 1.0, 3.0))
        last_tile_row = jnp.where(padded_row > 0, base_row + padded_row - tile, -1.0)
        last_tile_row = jnp.concatenate(
            [last_tile_row, jnp.full((1, n_tbl - CLASS_ROWS), -1.0, _F32)], axis=1)
        zero = jnp.zeros((1, n_tbl), _F32)
        tbl_ref[...] = jnp.concatenate(
            [grp * EXPERTS_PER_GROUP + a, grp * EXPERTS_PER_GROUP + bb, jnp.where(used, k_row, last),
             jnp.where(used, 1.0, 0.0), last_tile_row, zero, zero, zero], axis=0).astype(_I32)

    @pl.when(phase == 1)
    def _():
        before = lax.broadcasted_iota(_I32, (chunk, chunk), 0) < lax.broadcasted_iota(_I32, (chunk, chunk), 1)
        earlier = jnp.dot(jnp.where(onehot, 1.0, 0.0).astype(_BF16), jnp.where(before, 1.0, 0.0).astype(_BF16),
                          preferred_element_type=_F32)
        dest = base_ref[...] + run_ref[...] + earlier
        slot_ref[0] = jnp.sum(jnp.where(onehot, dest, 0.0), axis=0, keepdims=True).astype(_I32)
        run_ref[...] += hits


def _moe_plan(cls, tile):
    B, _, S = cls.shape
    chunk = PLAN_CHUNK
    n_tiles = (B * S) // tile + N_CLASSES
    n_tbl = -(-n_tiles // LANES) * LANES
    slot, tbl = pl.pallas_call(
        functools.partial(_plan_kernel, tile=tile),
        grid=(2, B, S // chunk),
        in_specs=[pl.BlockSpec((1, 1, chunk), lambda p, b, j: (b, 0, j))],
        out_specs=[pl.BlockSpec((1, 1, chunk), lambda p, b, j: (b * p, 0, j * p)),
                   pl.BlockSpec((SUBLANES, n_tbl), lambda p, b, j: (0, 0))],
        out_shape=[jax.ShapeDtypeStruct((B, 1, S), _I32), jax.ShapeDtypeStruct((SUBLANES, n_tbl), _I32)],
        scratch_shapes=[pltpu.VMEM((CLASS_ROWS, 1), _F32)] * 3,
        compiler_params=_params(("arbitrary", "arbitrary", "arbitrary")),
        name="moe_plan",
    )(cls)
    return slot, tbl, n_tiles


def _row_copy(src, src_row, dst, dst_row, sem):
    return pltpu.make_async_copy(src.at[pl.ds(pl.multiple_of(src_row * ROW_CHUNKS, ROW_CHUNKS), ROW_CHUNKS), :],
                                 dst.at[pl.ds(pl.multiple_of(dst_row * ROW_CHUNKS, ROW_CHUNKS), ROW_CHUNKS), :],
                                 sem)


def _dispatch_start(tbl_ref, slot_ref, h_ref, hs_ref, stage_ref, zero_ref, sem, zsem, *, tile):
    first = (pl.program_id(0) == 0) & (pl.program_id(1) == 0)
    tm = h_ref.shape[1]
    tile_rows = tile * ROW_CHUNKS

    @pl.when(first)
    def _():
        zero_ref[...] = jnp.zeros_like(zero_ref)
        n_tiles = hs_ref.shape[0] // tile_rows
        for wait in (False, True):
            for c in range(N_CLASSES):
                row = tbl_ref[TBL_LAST_TILE_ROW, c]

                @pl.when(row >= 0)
                def _():
                    at = pl.multiple_of(row * ROW_CHUNKS, ROW_CHUNKS)
                    cp = pltpu.make_async_copy(zero_ref, hs_ref.at[pl.ds(at, tile_rows), :], zsem)
                    cp.wait() if wait else cp.start()

            for k in range(n_tiles - N_CLASSES, n_tiles):
                @pl.when(tbl_ref[TBL_USED, k] == 0)
                def _():
                    cp = pltpu.make_async_copy(zero_ref, hs_ref.at[pl.ds(k * tile_rows, tile_rows), :], zsem)
                    cp.wait() if wait else cp.start()

    h = h_ref[0]
    for c in range(ROW_CHUNKS):
        stage_ref[pl.ds(c, tm, stride=ROW_CHUNKS), :] = h[:, c * LANES:(c + 1) * LANES].astype(_F32)

    def start_slice(j, n_j):
        for r in range(j * tm // n_j, (j + 1) * tm // n_j):
            _row_copy(stage_ref, r, hs_ref, slot_ref[0, 0, r], sem).start(priority=r % 2)

    return start_slice


def _dispatch_wait(stage_ref, hs_ref, sem):
    pltpu.make_async_copy(stage_ref, hs_ref.at[pl.ds(0, stage_ref.shape[0]), :], sem).wait()


def _launch(kernel_fn, grid, in_specs, out_specs, out_shape, scratch, args, name, rider=None):
    params = _params(("arbitrary",) * len(grid))
    if rider is None:
        return pl.pallas_call(kernel_fn, grid=grid, in_specs=in_specs, out_specs=out_specs, out_shape=out_shape,
                              scratch_shapes=scratch, compiler_params=params, name=name)(*args)
    tbl, slot, h2, n_tiles = rider
    tile = MOE_TILE
    steps = grid[0] * grid[1]
    Bo, So, _ = h2.shape
    share = (Bo * So) // steps
    per_seq = So // share
    n_in, n_out, n_scr = len(in_specs), len(out_specs), len(scratch)

    def drop_tbl(spec):
        if spec.index_map is None:
            return spec
        return pl.BlockSpec(spec.block_shape, lambda *a, f=spec.index_map: f(*a[:-1]), memory_space=spec.memory_space)

    def kern(tbl_ref, *refs):
        host_in, (slot_ref, h_ref) = refs[:n_in], refs[n_in:n_in + 2]
        host_out, hs_ref = refs[n_in + 2:n_in + 2 + n_out], refs[n_in + 2 + n_out]
        host_scr = refs[n_in + 3 + n_out:n_in + 3 + n_out + n_scr]
        stage_ref, zero_ref, sem, zsem = refs[n_in + 3 + n_out + n_scr:]
        start_slice = _dispatch_start(tbl_ref, slot_ref, h_ref, hs_ref, stage_ref, zero_ref, sem, zsem, tile=tile)
        kernel_fn(*host_in, *host_out, *host_scr, rider_between=start_slice)
        _dispatch_wait(stage_ref, hs_ref, sem)

    step_of = lambda b, i: b * grid[1] + i
    rider_in = [pl.BlockSpec((1, 1, share), lambda b, i, t: (step_of(b, i) // per_seq, 0, step_of(b, i) % per_seq),
                             memory_space=pltpu.SMEM),
                pl.BlockSpec((1, share, D_MODEL), lambda b, i, t: (step_of(b, i) // per_seq, step_of(b, i) % per_seq, 0))]
    return pl.pallas_call(
        kern,
        grid_spec=pltpu.PrefetchScalarGridSpec(
            num_scalar_prefetch=1,
            grid=grid,
            in_specs=[drop_tbl(s) for s in in_specs] + rider_in,
            out_specs=[drop_tbl(s) for s in out_specs] + [pl.BlockSpec(memory_space=pl.ANY)],
            scratch_shapes=list(scratch) + [pltpu.VMEM((share * ROW_CHUNKS, LANES), _F32),
                                            pltpu.VMEM((tile * ROW_CHUNKS, LANES), _F32),
                                            pltpu.SemaphoreType.DMA(()), pltpu.SemaphoreType.DMA(())],
        ),
        out_shape=list(out_shape) + [jax.ShapeDtypeStruct((n_tiles * tile * ROW_CHUNKS, LANES), _F32)],
        compiler_params=params,
        name=name + "_dispatch",
    )(tbl, *args, slot, h2)


EXPERT_TILE_INPUTS = 11


def _experts_kernel(tbl_ref, *refs):
    n = EXPERT_TILE_INPUTS
    ins = (refs[:n], refs[n:2 * n])
    ys_ref = refs[2 * n]
    rows = ys_ref.shape[0] // 2
    k = pl.program_id(0)
    used = [tbl_ref[TBL_USED, 2 * k + j] for j in range(2)]

    def idle(j):
        ys_ref[j * rows:(j + 1) * rows, :] = jnp.zeros((rows, LANES), _F32)

    @pl.when(used[1] == 1)
    def _():
        for j in range(2):
            _expert_tile(*ins[j], ys_ref, j * rows)

    @pl.when((used[0] == 1) & (used[1] == 0))
    def _():
        _expert_tile(*ins[0], ys_ref, 0)
        idle(1)

    @pl.when(used[0] == 0)
    def _():
        idle(0)
        idle(1)


def _expert_tile(hs_ref, wra_hi_ref, wra_lo_ref, wrb_hi_ref, wrb_lo_ref,
                 wga_ref, wua_ref, wda_ref, wgb_ref, wub_ref, wdb_ref, ys_ref, row0):
    tile = hs_ref.shape[0] // ROW_CHUNKS
    chunks = [hs_ref[pl.ds(c, tile, stride=ROW_CHUNKS), :] for c in range(ROW_CHUNKS)]
    x = jnp.concatenate([ch.astype(_BF16) for ch in chunks], axis=1)

    def affinity(hi_ref, lo_ref):
        w = hi_ref[0] + lo_ref[0]
        part = chunks[0] * w[:, 0:LANES]
        for c in range(1, ROW_CHUNKS):
            part = part + chunks[c] * w[:, c * LANES:(c + 1) * LANES]
        return 1.0 / (1.0 + jnp.exp(-jnp.sum(part, axis=1, keepdims=True)))

    aff_a = affinity(wra_hi_ref, wra_lo_ref)
    aff_b = affinity(wrb_hi_ref, wrb_lo_ref)
    den = aff_a + aff_b

    def hidden(wg_ref, wu_ref, gate):
        a = jnp.dot(x, wg_ref[0], preferred_element_type=_F32)
        b = jnp.dot(x, wu_ref[0], preferred_element_type=_F32)
        return ((a / (1.0 + jnp.exp(-a))) * b * gate).astype(_BF16)

    y = (jnp.dot(hidden(wga_ref, wua_ref, aff_a / den), wda_ref[0], preferred_element_type=_F32)
         + jnp.dot(hidden(wgb_ref, wub_ref, aff_b / den), wdb_ref[0], preferred_element_type=_F32))
    for c in range(ROW_CHUNKS):
        ys_ref[pl.ds(row0 + c, tile, stride=ROW_CHUNKS), :] = y[:, c * LANES:(c + 1) * LANES]


def _moe_experts(tbl, hs, wr_hi, wr_lo, wg, wu, wd, layer, n_tiles, tile):
    rows = tile * ROW_CHUNKS
    first = layer * N_EXPERTS
    a, b = TBL_EXPERT_A, TBL_EXPERT_B
    in_specs = []
    for j in range(2):
        router = lambda row, j=j: pl.BlockSpec((1, 1, D_MODEL), lambda k, tbl: (tbl[row, 2 * k + j], 0, 0))
        up = lambda row, j=j: pl.BlockSpec((1, D_MODEL, D_EXPERT),
                                           lambda k, tbl: (first + tbl[row, 2 * k + j], 0, 0))
        down = lambda row, j=j: pl.BlockSpec((1, D_EXPERT, D_MODEL),
                                             lambda k, tbl: (first + tbl[row, 2 * k + j], 0, 0))
        in_specs += [pl.BlockSpec((rows, LANES), lambda k, tbl, j=j: (tbl[TBL_TILE, 2 * k + j], 0)),
                     router(a), router(a), router(b), router(b), up(a), up(a), down(a), up(b), up(b), down(b)]
    tile_args = (hs, wr_hi, wr_lo, wr_hi, wr_lo, wg, wu, wd, wg, wu, wd)
    return pl.pallas_call(
        _experts_kernel,
        grid_spec=pltpu.PrefetchScalarGridSpec(
            num_scalar_prefetch=1,
            grid=(n_tiles // 2,),
            in_specs=in_specs,
            out_specs=pl.BlockSpec((2 * rows, LANES), lambda k, tbl: (k, 0)),
        ),
        out_shape=jax.ShapeDtypeStruct(hs.shape, _F32),
        compiler_params=_params(("arbitrary",)),
        name="moe_experts",
    )(tbl, *tile_args, *tile_args)


def _combine_kernel(slot_ref, slot_next_ref, x1_ref, mod_ref, ys_ref, *refs, project, rider_between=None):
    if project:
        (modn_ref, gain_ref, w_ref, cos_ref, sin_ref, x2_ref, *out_refs) = refs[:16]
        buf_ref, sem, stage_ref = refs[16:19]
    else:
        gfin_ref, x2_ref, buf_ref, sem = refs[:4]
    tm = x1_ref.shape[1]
    step = pl.program_id(0) * pl.num_programs(1) + pl.program_id(1)
    last = pl.num_programs(0) * pl.num_programs(1) - 1
    gate = mod_ref[0][5:6]

    def gather(sref):
        def issue(g, carry):
            for k in range(ROWS_PER_TRIP):
                r = g * ROWS_PER_TRIP + k
                _row_copy(ys_ref, sref[0, 0, r], buf_ref, r, sem).start(priority=k % 2)
            return carry
        lax.fori_loop(0, tm // ROWS_PER_TRIP, issue, 0)

    @pl.when(step == 0)
    def _():
        gather(slot_ref)

    pltpu.make_async_copy(ys_ref.at[pl.ds(0, tm * ROW_CHUNKS), :], buf_ref, sem).wait()
    for c in range(ROW_CHUNKS):
        cols = slice(c * LANES, (c + 1) * LANES)
        x2_ref[0, :, cols] = x1_ref[0, :, cols] + gate[:, cols] * buf_ref[pl.ds(c, tm, stride=ROW_CHUNKS), :]

    x2 = x2_ref[0]
    if project:
        def start_slice(j, n_j):
            for r in range(j * tm // n_j, (j + 1) * tm // n_j):
                _row_copy(ys_ref, slot_next_ref[0, 0, r], buf_ref, r, sem).start(priority=r % 2)
            if rider_between is not None:
                rider_between(j, n_j)

        _project(x2, 0, modn_ref[0], gain_ref, w_ref, cos_ref, sin_ref, out_refs[0:3], out_refs[3:6],
                 out_refs[6:9], out_refs[9], stage_ref, between=start_slice)

        @pl.when(step == last)
        def _():
            pltpu.make_async_copy(ys_ref.at[pl.ds(0, tm * ROW_CHUNKS), :], buf_ref, sem).wait()
    else:
        @pl.when(step < last)
        def _():
            gather(slot_next_ref)

        if rider_between is not None:
            rider_between(0, 1)
        x2_ref[0] = x2 * lax.rsqrt(jnp.mean(x2 * x2, axis=-1, keepdims=True) + RMS_EPS) * gfin_ref[...]


def _moe_combine(slot, x1, mod_prev, ys, batch_offset, tail_args, rider=None):
    B, S, _ = x1.shape
    tm = TOKEN_TILE
    n_i = S // tm
    project = len(tail_args) > 1

    def next_step(b, i):
        n = jnp.minimum(b * n_i + i + 1, B * n_i - 1)
        return n // n_i, 0, n % n_i

    tok = pl.BlockSpec((1, tm, D_MODEL), lambda b, i: (b, i, 0))
    mod_spec = pl.BlockSpec((1, 6, D_MODEL), lambda b, i: (b + batch_offset, 0, 0))
    in_specs = [pl.BlockSpec((1, 1, tm), lambda b, i: (b, 0, i), memory_space=pltpu.SMEM),
                pl.BlockSpec((1, 1, tm), next_step, memory_space=pltpu.SMEM),
                tok, mod_spec, pl.BlockSpec(memory_space=pl.ANY)]
    out_specs, out_shape = [tok], [jax.ShapeDtypeStruct((B, S, D_MODEL), _F32)]
    scratch = [pltpu.VMEM((tm * ROW_CHUNKS, LANES), _F32), pltpu.SemaphoreType.DMA(())]
    if project:
        in_specs += [mod_spec,
                     pl.BlockSpec((1, D_MODEL), lambda b, i: (0, 0)),
                     pl.BlockSpec((D_MODEL, IN_W), lambda b, i: (0, 0)),
                     pl.BlockSpec((tm, LANES), lambda b, i: (i, 0)),
                     pl.BlockSpec((tm, LANES), lambda b, i: (i, 0))]
        out_specs += [pl.BlockSpec((1, tm // d, d * GROUP_W), lambda b, i: (b, i, 0)) for _, d in DILATED_GROUPS] * 3
        out_specs += [pl.BlockSpec((1, tm, POOL_W), lambda b, i: (b, i, 0))]
        out_shape += [jax.ShapeDtypeStruct((B, S // d, d * GROUP_W), _BF16) for _, d in DILATED_GROUPS] * 3
        out_shape += [jax.ShapeDtypeStruct((B, S, POOL_W), _F32)]
        scratch += [pltpu.VMEM((tm, LANES), _F32)]
    else:
        in_specs += [pl.BlockSpec((1, D_MODEL), lambda b, i: (0, 0))]
    return _launch(
        functools.partial(_combine_kernel, project=project),
        grid=(B, n_i),
        in_specs=in_specs,
        out_specs=out_specs,
        out_shape=out_shape,
        scratch=scratch,
        args=(slot, slot, x1, mod_prev, ys, *tail_args),
        name="moe_combine_in_proj" if project else "moe_combine_norm",
        rider=rider,
    )


def _rope_tables(S):
    inv = 1.0 / (ROPE_THETA ** (jnp.arange(0, HEAD_DIM, 2, dtype=_F32) / HEAD_DIM))
    ang = jnp.arange(S, dtype=_F32)[:, None] * inv[None, :]
    cos, sin = jnp.cos(ang), jnp.sin(ang)
    reps = LANES // HEAD_DIM
    return (jnp.tile(jnp.concatenate([cos, cos], axis=1), (1, reps)),
            jnp.tile(jnp.concatenate([-sin, sin], axis=1), (1, reps)))


def _trunks(xs, batch_offsets, mod, w):
    tables = [_rope_tables(x.shape[1]) for x in xs]
    outs = [_in_proj(x, mod[0], w["norm_mix"][0], w["w_in"][0], *tables[t], batch_offsets[t])
            for t, x in enumerate(xs)]
    xs = list(xs)
    for l in range(DEPTH):
        mod_l = mod[l]

        def mix(t, rider):
            q, k, v, u = outs[t][0:3], outs[t][3:6], outs[t][6:9], outs[t][9]
            attn, lses = [], []
            for g, (window, dilation) in enumerate(DILATED_GROUPS):
                o, lse = _attention(q[g], k[g], v[g], window, dilation)
                attn.append(o)
                lses.append(lse)
            return _mix_out(xs[t], attn, lses, u, mod_l, w["wpool_bd"][l], w["pool_scale"][l], w["w_out"][l],
                            w["norm_ffn"][l], w["wr_hi"], w["wr_lo"], w["rbias"], batch_offsets[t], rider)

        def experts(tbl, hs, n_tiles):
            return _moe_experts(tbl, hs, w["wr_hi3"], w["wr_lo3"], w["w_gate"], w["w_up"], w["w_down"], l,
                                n_tiles, MOE_TILE)

        def combine(t, slot, x1, ys, rider):
            if l + 1 < DEPTH:
                tail = (mod[l + 1], w["norm_mix"][l + 1], w["w_in"][l + 1], *tables[t])
            else:
                tail = (w["norm_final"],)
            return _moe_combine(slot, x1, mod_l, ys, batch_offsets[t], tail, rider)

        x1_0, h2_0, cls_0 = mix(0, None)
        slot_0, tbl_0, n_0 = _moe_plan(cls_0, MOE_TILE)
        x1_1, h2_1, cls_1, hs_0 = mix(1, (tbl_0, slot_0, h2_0, n_0))
        slot_1, tbl_1, n_1 = _moe_plan(cls_1, MOE_TILE)
        ys_0 = experts(tbl_0, hs_0, n_0)
        xs[0], *rest = combine(0, slot_0, x1_0, ys_0, (tbl_1, slot_1, h2_1, n_1))
        outs[0], hs_1 = rest[:-1], rest[-1]
        ys_1 = experts(tbl_1, hs_1, n_1)
        xs[1], *outs[1] = combine(1, slot_1, x1_1, ys_1, None)
    return xs


def kernel(x_prompt, x_sample, c_prompt, c_sample, norm_mix, w_mod, b_mod, w_in, w_pool, pool_scale, w_out,
           norm_ffn, w_router, router_bias, w_gate, w_up, w_down, norm_final):
    n_prompt = x_prompt.shape[0]
    c = jnp.concatenate([c_prompt, c_sample], axis=0)
    mod = _modulation(c, w_mod, b_mod).reshape(DEPTH, c.shape[0], 6, D_MODEL)
    n_pool = len(POOL_WINDOWS)
    eye = jnp.eye(n_pool, dtype=w_pool.dtype)
    wpool_bd = (w_pool[:, :, :, None, :] * eye[None, :, None, :, None]).reshape(DEPTH, POOL_W, POOL_W)
    wr_t = w_router.T.astype(_F32)
    wr_hi = wr_t.astype(_BF16)
    wr_lo = (wr_t - wr_hi.astype(_F32)).astype(_BF16)
    w = {
        "norm_mix": norm_mix.reshape(DEPTH, 1, D_MODEL),
        "norm_ffn": norm_ffn.reshape(DEPTH, 1, D_MODEL),
        "norm_final": norm_final.reshape(1, D_MODEL),
        "w_in": w_in.astype(_BF16),
        "w_out": w_out.astype(_BF16),
        "wpool_bd": wpool_bd.astype(_BF16),
        "pool_scale": pool_scale.reshape(DEPTH, 1, POOL_W),
        "wr_hi": wr_hi,
        "wr_lo": wr_lo,
        "wr_hi3": wr_hi.astype(_F32).reshape(N_EXPERTS, 1, D_MODEL),
        "wr_lo3": wr_lo.astype(_F32).reshape(N_EXPERTS, 1, D_MODEL),
        "rbias": router_bias.astype(_F32).reshape(N_EXPERTS, 1),
        "w_gate": w_gate.astype(_BF16).reshape(DEPTH * N_EXPERTS, D_MODEL, D_EXPERT),
        "w_up": w_up.astype(_BF16).reshape(DEPTH * N_EXPERTS, D_MODEL, D_EXPERT),
        "w_down": w_down.astype(_BF16).reshape(DEPTH * N_EXPERTS, D_EXPERT, D_MODEL),
    }
    y_prompt, y_sample = _trunks((x_prompt, x_sample), (0, n_prompt), mod, w)
    return (y_prompt, y_sample)
```

```python
import functools

import jax
import jax.numpy as jnp
from jax import lax
from jax.experimental import pallas as pl
from jax.experimental.pallas import tpu as pltpu

D_MODEL = 1024
DEPTH = 4
HEAD_DIM = 64
HALF_HEAD = HEAD_DIM // 2
DILATED_GROUPS = ((128, 1), (512, 4), (2048, 16))
N_GROUPS = len(DILATED_GROUPS)
HEADS_PER_GROUP = 4
GROUP_W = HEADS_PER_GROUP * HEAD_DIM
ATT_W = N_GROUPS * GROUP_W
POOL_WINDOWS = (2, 4, 8, 16)
POOL_GROUP_W = 64
POOL_W = POOL_GROUP_W * len(POOL_WINDOWS)
IN_W = 3 * ATT_W + POOL_W
ROPE_THETA = 10000.0
RMS_EPS = 1e-6
N_EXPERTS = 16
N_EXPERT_GROUPS = 4
EXPERTS_PER_GROUP = N_EXPERTS // N_EXPERT_GROUPS
PAIRS_PER_GROUP = EXPERTS_PER_GROUP * (EXPERTS_PER_GROUP - 1) // 2
N_CLASSES = N_EXPERT_GROUPS * PAIRS_PER_GROUP
D_EXPERT = 512

LANES = 128
SUBLANES = 8
Q_BLOCK = 128
VMEM_LIMIT = 48 * 1024 * 1024

TOKEN_TILE = 512
MOE_TILE = 256
MIX_TILE = 1024
MIX_CHUNKS = 2
PLAN_CHUNK = 512
CLASS_ROWS = 32
ROW_CHUNKS = D_MODEL // LANES
ROWS_PER_TRIP = 32

_BF16 = jnp.bfloat16
_F32 = jnp.float32
_I32 = jnp.int32


def _params(semantics):
    return pltpu.CompilerParams(dimension_semantics=semantics, vmem_limit_bytes=VMEM_LIMIT)


def _mod_kernel(c_ref, w_ref, b_ref, o_ref):
    c = c_ref[...]
    sc = c / (1.0 + jnp.exp(-c))
    o_ref[0] = jnp.dot(sc, w_ref[0], preferred_element_type=_F32,
                       precision=lax.Precision.HIGHEST) + b_ref[0]


def _modulation(c, w_mod, b_mod):
    nb = c.shape[0]
    col = D_MODEL
    n_col = w_mod.shape[2] // col
    return pl.pallas_call(
        _mod_kernel,
        grid=(DEPTH, n_col),
        in_specs=[
            pl.BlockSpec((nb, D_MODEL), lambda l, j: (0, 0)),
            pl.BlockSpec((1, D_MODEL, col), lambda l, j: (l, 0, j)),
            pl.BlockSpec((1, 1, col), lambda l, j: (l, 0, j)),
        ],
        out_specs=pl.BlockSpec((1, nb, col), lambda l, j: (l, 0, j)),
        out_shape=jax.ShapeDtypeStruct((DEPTH, nb, w_mod.shape[2]), _F32),
        compiler_params=_params(("arbitrary", "arbitrary")),
        name="modulation",
    )(c, w_mod, b_mod.reshape(DEPTH, 1, -1))


def _rmsnorm_mod(x, gain, scale, shift):
    y = x * lax.rsqrt(jnp.mean(x * x, axis=-1, keepdims=True) + RMS_EPS)
    return y * (gain * (1.0 + scale)) + shift


def _rope_chunk(t, cos, sin_signed, first_half):
    fwd = pltpu.roll(t, HALF_HEAD, axis=1)
    bwd = pltpu.roll(t, LANES - HALF_HEAD, axis=1)
    return t * cos + jnp.where(first_half, bwd, fwd) * sin_signed


def _store_by_residue(out_ref, chunk, value, dilation, stage_ref, row0):
    rows = value.shape[0] // dilation
    at = slice(row0 // dilation, row0 // dilation + rows)
    if dilation == 1:
        out_ref[0, at, chunk * LANES:(chunk + 1) * LANES] = value.astype(_BF16)
        return
    stage_ref[...] = value
    for r in range(dilation):
        lo = r * GROUP_W + chunk * LANES
        out_ref[0, at, lo:lo + LANES] = stage_ref[pl.ds(r, rows, stride=dilation), :].astype(_BF16)


def _in_proj_kernel(x_ref, mod_ref, gain_ref, w_ref, cos_ref, sin_ref, *refs):
    _project(x_ref[0], 0, mod_ref[0], gain_ref, w_ref, cos_ref, sin_ref, refs[0:3], refs[3:6], refs[6:9], refs[9],
             refs[10])


def _project(x, row0, mod, gain_ref, w_ref, cos_ref, sin_ref, q_refs, k_refs, v_refs, u_ref, stage_ref,
             between=None):
    n = x.shape[0]
    n_chunks = N_GROUPS * (GROUP_W // LANES)
    h = _rmsnorm_mod(x, gain_ref[...], mod[1:2], mod[0:1])
    proj = jnp.dot(h.astype(_BF16), w_ref[...], preferred_element_type=_F32)
    cos = cos_ref[row0:row0 + n, :]
    sin_signed = sin_ref[row0:row0 + n, :]
    lane = lax.broadcasted_iota(_I32, (1, LANES), 1)
    first_half = (lane % HEAD_DIM) < HALF_HEAD
    q_scale = HEAD_DIM ** -0.5
    for g, (_, dilation) in enumerate(DILATED_GROUPS):
        for c in range(GROUP_W // LANES):
            lo = g * GROUP_W + c * LANES
            if between is not None:
                between(g * (GROUP_W // LANES) + c, n_chunks)
            q = _rope_chunk(proj[:, lo:lo + LANES], cos, sin_signed, first_half)
            _store_by_residue(q_refs[g], c, q * q_scale, dilation, stage_ref, row0)
            k = _rope_chunk(proj[:, ATT_W + lo:ATT_W + lo + LANES], cos, sin_signed, first_half)
            _store_by_residue(k_refs[g], c, k, dilation, stage_ref, row0)
            _store_by_residue(v_refs[g], c, proj[:, 2 * ATT_W + lo:2 * ATT_W + lo + LANES], dilation, stage_ref,
                              row0)
    u_ref[0, row0:row0 + n, :] = proj[:, 3 * ATT_W:]


def _in_proj(x, mod_l, gain, w_in_bf16, cos, sin_signed, batch_offset):
    B, S, _ = x.shape
    tm = TOKEN_TILE
    grp = [jax.ShapeDtypeStruct((B, S // d, d * GROUP_W), _BF16) for _, d in DILATED_GROUPS]
    grp_spec = [pl.BlockSpec((1, tm // d, d * GROUP_W), lambda b, i: (b, i, 0)) for _, d in DILATED_GROUPS]
    return pl.pallas_call(
        _in_proj_kernel,
        grid=(B, S // tm),
        in_specs=[
            pl.BlockSpec((1, tm, D_MODEL), lambda b, i: (b, i, 0)),
            pl.BlockSpec((1, 6, D_MODEL), lambda b, i: (b + batch_offset, 0, 0)),
            pl.BlockSpec((1, D_MODEL), lambda b, i: (0, 0)),
            pl.BlockSpec((D_MODEL, IN_W), lambda b, i: (0, 0)),
            pl.BlockSpec((tm, LANES), lambda b, i: (i, 0)),
            pl.BlockSpec((tm, LANES), lambda b, i: (i, 0)),
        ],
        out_specs=grp_spec * 3 + [pl.BlockSpec((1, tm, POOL_W), lambda b, i: (b, i, 0))],
        out_shape=grp * 3 + [jax.ShapeDtypeStruct((B, S, POOL_W), _F32)],
        scratch_shapes=[pltpu.VMEM((tm, LANES), _F32)],
        compiler_params=_params(("arbitrary", "arbitrary")),
        name="in_proj",
    )(x, mod_l, gain, w_in_bf16, cos, sin_signed)


def _attn_kernel(q_ref, k_ref, v_ref, o_ref, lse_ref, bias_ref, *, length, n_side, n_res, q_unroll):
    rb = pl.program_id(1)
    key_w = min(length, Q_BLOCK + 2 * n_side)
    n_qb = length // Q_BLOCK

    @pl.when((pl.program_id(0) == 0) & (rb == 0))
    def _():
        i = lax.broadcasted_iota(_I32, (2 * Q_BLOCK, key_w), 0) % Q_BLOCK
        j = lax.broadcasted_iota(_I32, (2 * Q_BLOCK, key_w), 1)
        for n in range(3):
            bias_ref[n] = jnp.where(jnp.abs(i - j + n * n_side) <= n_side, 0.0, -jnp.inf)

    @pl.when(rb == 0)
    def _():
        lse_ref[...] = jnp.zeros_like(lse_ref)

    lane = lax.broadcasted_iota(_I32, (1, LANES), 1)
    head_mask = [(lane < HEAD_DIM).astype(_BF16), (lane >= HEAD_DIM).astype(_BF16)]
    lse_lane = lax.broadcasted_iota(_I32, (1, lse_ref.shape[2]), 1)

    def trip(it, carry):
        for u in range(q_unroll):
            q0 = pl.multiple_of((it * q_unroll + u) * Q_BLOCK, Q_BLOCK)
            k0 = pl.multiple_of(jnp.clip(q0 - n_side, 0, length - key_w), n_side)
            bias = bias_ref[(q0 - k0) // n_side]
            lse_rows = lse_ref[0, pl.ds(q0, Q_BLOCK), :]
            for rr in range(n_res):
                for pair in range(GROUP_W // LANES):
                    lo = rr * GROUP_W + pair * LANES
                    cols = slice(lo, lo + LANES)
                    q2 = q_ref[0, pl.ds(q0, Q_BLOCK), cols]
                    k2 = k_ref[0, pl.ds(k0, key_w), cols]
                    v2 = v_ref[0, pl.ds(k0, key_w), cols]
                    qs = jnp.concatenate([q2 * head_mask[0], q2 * head_mask[1]], axis=0)
                    s = lax.dot_general(qs, k2, (((1,), (1,)), ((), ())), preferred_element_type=_F32) + bias
                    m = jnp.max(s, axis=1, keepdims=True)
                    p = jnp.exp(s - m)
                    l = jnp.sum(p, axis=1, keepdims=True)
                    o = jnp.dot(p.astype(_BF16), v2, preferred_element_type=_F32) * (1.0 / l)
                    lse = m + jnp.log(l)
                    for hh in range(2):
                        col = (rb * n_res + rr) * HEADS_PER_GROUP + pair * 2 + hh
                        lse_rows = jnp.where(lse_lane == col, lse[hh * Q_BLOCK:(hh + 1) * Q_BLOCK], lse_rows)
                    o_ref[0, pl.ds(q0, Q_BLOCK), cols] = jnp.where(lane < HEAD_DIM, o[:Q_BLOCK], o[Q_BLOCK:]).astype(_BF16)
            lse_ref[0, pl.ds(q0, Q_BLOCK), :] = lse_rows
        return carry

    lax.fori_loop(0, n_qb // q_unroll, trip, 0)


ATTN_BLOCKS_PER_TRIP = 16


def _attention(q, k, v, window, dilation):
    B, L, _ = q.shape
    d = dilation
    n_side = window // (2 * d)
    n_res = min(d, ATTN_BLOCKS_PER_TRIP)
    q_unroll = min(ATTN_BLOCKS_PER_TRIP // n_res, L // Q_BLOCK)
    key_w = min(L, Q_BLOCK + 2 * n_side)
    spec = pl.BlockSpec((1, L, n_res * GROUP_W), lambda b, r: (b, 0, r))
    return pl.pallas_call(
        functools.partial(_attn_kernel, length=L, n_side=n_side, n_res=n_res, q_unroll=q_unroll),
        grid=(B, d // n_res),
        in_specs=[spec, spec, spec],
        out_specs=[spec, pl.BlockSpec((1, L, d * HEADS_PER_GROUP), lambda b, r: (b, 0, 0))],
        out_shape=[jax.ShapeDtypeStruct((B, L, d * GROUP_W), _BF16),
                   jax.ShapeDtypeStruct((B, L, d * HEADS_PER_GROUP), _F32)],
        scratch_shapes=[pltpu.VMEM((3, 2 * Q_BLOCK, key_w), _F32)],
        compiler_params=_params(("arbitrary", "arbitrary")),
        name=f"attention_d{d}",
    )(q, k, v)


def _route(sel):
    s = [sel[e:e + 1, :] for e in range(N_EXPERTS)]
    group_score = []
    for g in range(N_EXPERT_GROUPS):
        c0, c1, c2, c3 = s[EXPERTS_PER_GROUP * g:EXPERTS_PER_GROUP * (g + 1)]
        m1, n1 = jnp.maximum(c0, c1), jnp.minimum(c0, c1)
        m2, n2 = jnp.maximum(c2, c3), jnp.minimum(c2, c3)
        group_score.append(jnp.maximum(m1, m2) + jnp.maximum(jnp.minimum(m1, m2), jnp.maximum(n1, n2)))
    best, top_g = group_score[0], jnp.zeros(group_score[0].shape, _I32)
    for g in range(1, N_EXPERT_GROUPS):
        better = group_score[g] > best
        best = jnp.where(better, group_score[g], best)
        top_g = jnp.where(better, g, top_g)
    cs = []
    for j in range(EXPERTS_PER_GROUP):
        c = s[j]
        for g in range(1, N_EXPERT_GROUPS):
            c = jnp.where(top_g == g, s[EXPERTS_PER_GROUP * g + j], c)
        cs.append(c)
    s1, i1 = cs[0], jnp.zeros_like(top_g)
    for j in range(1, EXPERTS_PER_GROUP):
        better = cs[j] > s1
        s1 = jnp.where(better, cs[j], s1)
        i1 = jnp.where(better, j, i1)
    i2 = jnp.where(i1 == 0, 1, 0)
    s2 = jnp.where(i1 == 0, cs[1], cs[0])
    for j in range(1, EXPERTS_PER_GROUP):
        better = (cs[j] > s2) & (i1 != j)
        s2 = jnp.where(better, cs[j], s2)
        i2 = jnp.where(better, j, i2)
    lo, hi = jnp.minimum(i1, i2), jnp.maximum(i1, i2)
    pair = jnp.where(lo == 0, 0, jnp.where(lo == 1, 3, 5)) + (hi - lo - 1)
    return top_g * PAIRS_PER_GROUP + pair


def _mix_out_kernel(x_ref, o0_ref, o1_ref, o2_ref, l0_ref, l1_ref, l2_ref, u_ref, up_ref, un_ref,
                    mod_ref, wpool_ref, pscale_ref, wout_ref, gain_ref, wr_hi_ref, wr_lo_ref, rbias_ref,
                    x1_ref, h2_ref, cls_ref, ext_ref, lvl_a_ref, lvl_b_ref, lvl_c_ref, nat_o1a_ref, nat_o1b_ref,
                    nat_o2a_ref, nat_o2b_ref, nat_l1_ref, nat_l2_ref, *, seq_len, rider_between=None):
    nat_o_refs = ((nat_o1a_ref, nat_o1b_ref), (nat_o2a_ref, nat_o2b_ref))
    nat_l_refs = (nat_l1_ref, nat_l2_ref)
    i = pl.program_id(1)
    n_i = pl.num_programs(1)
    tm = x_ref.shape[1]
    mod = mod_ref[0]

    n_slices = sum(d for _, d in DILATED_GROUPS if d > 1)
    done = 0
    for g, (o_ref, l_ref) in enumerate(((o0_ref, l0_ref), (o1_ref, l1_ref), (o2_ref, l2_ref))):
        dilation = DILATED_GROUPS[g][1]
        if dilation == 1:
            continue
        rows = tm // dilation
        lv = l_ref[0]
        lv = jnp.concatenate([lv, jnp.zeros((rows, LANES - lv.shape[1]), _F32)], axis=1)
        halves = nat_o_refs[g - 1]
        for r in range(dilation):
            if rider_between is not None:
                rider_between(done, n_slices)
            done += 1
            for c, half_ref in enumerate(halves):
                lo = r * GROUP_W + c * LANES
                half_ref[pl.ds(r, rows, stride=dilation), :] = o_ref[0, :, lo:lo + LANES].astype(_F32)
            shifted = lv if r == 0 else pltpu.roll(lv, LANES - r * HEADS_PER_GROUP, axis=1)
            nat_l_refs[g - 1][pl.ds(r, rows, stride=dilation), :] = shifted

    pad = 2 * SUBLANES
    zeros_pad = jnp.zeros((SUBLANES, POOL_W), _F32)
    for ref in (ext_ref, lvl_a_ref, lvl_b_ref, lvl_c_ref):
        ref[0:SUBLANES, :] = zeros_pad
        ref[tm + pad + SUBLANES:tm + 2 * pad, :] = zeros_pad
    ext_ref[SUBLANES:pad, :] = jnp.where(i > 0, up_ref[0], 0.0)
    ext_ref[pad:pad + tm, :] = u_ref[0]
    ext_ref[pad + tm:pad + tm + SUBLANES, :] = jnp.where(i < n_i - 1, un_ref[0], 0.0)
    span = tm + 2 * SUBLANES
    win = lambda ref, off: ref[SUBLANES + off:SUBLANES + off + span, :]
    lvl_a_ref[SUBLANES:SUBLANES + span, :] = win(ext_ref, -1) + win(ext_ref, 0)
    lvl_b_ref[SUBLANES:SUBLANES + span, :] = win(lvl_a_ref, -1) + win(lvl_a_ref, 1)
    lvl_c_ref[SUBLANES:SUBLANES + span, :] = win(lvl_b_ref, -2) + win(lvl_b_ref, 2)
    lane = lax.broadcasted_iota(_I32, (1, POOL_W), 1)
    wgroup = lane // POOL_GROUP_W
    half_w = jnp.where(wgroup == 0, 1, jnp.where(wgroup == 1, 2, jnp.where(wgroup == 2, 4, 8)))

    half = tm // MIX_CHUNKS
    for hf in range(MIX_CHUNKS):
        r0 = hf * half
        rows = slice(r0, r0 + half)
        outs = [o0_ref[0, rows, :].astype(_F32)]
        lses = [l0_ref[0, rows, :]]
        for g in range(1, N_GROUPS):
            outs.append(jnp.concatenate([ref[rows, :] for ref in nat_o_refs[g - 1]], axis=1))
            lses.append(nat_l_refs[g - 1][rows, 0:HEADS_PER_GROUP])

        top = jnp.maximum(jnp.maximum(lses[0], lses[1]), lses[2])
        es = [jnp.exp(t - top) for t in lses]
        den = es[0] + es[1] + es[2]
        alpha = jnp.concatenate([e / den for e in es]
                                + [jnp.zeros((half, LANES - N_GROUPS * HEADS_PER_GROUP), _F32)], axis=1)
        a_hi = alpha.astype(_BF16)
        a_lo = (alpha - a_hi.astype(_F32)).astype(_BF16)
        head_of_lane = lax.broadcasted_iota(_I32, (LANES, ATT_W), 1) // HEAD_DIM
        spread = jnp.where(head_of_lane == lax.broadcasted_iota(_I32, (LANES, ATT_W), 0), 1.0, 0.0).astype(_BF16)
        wide = (jnp.dot(a_hi, spread, preferred_element_type=_F32)
                + jnp.dot(a_lo, spread, preferred_element_type=_F32))
        pieces = [(outs[g] * wide[:, g * GROUP_W:(g + 1) * GROUP_W]).astype(_BF16) for g in range(N_GROUPS)]

        s2 = lvl_a_ref[pad + r0:pad + r0 + half, :]
        s4 = lvl_b_ref[pad + r0:pad + r0 + half, :]
        s8 = lvl_c_ref[pad + r0:pad + r0 + half, :]
        s16 = lvl_c_ref[pad + r0 - 4:pad + r0 - 4 + half, :] + lvl_c_ref[pad + r0 + 4:pad + r0 + 4 + half, :]
        wsum = jnp.where(wgroup == 0, s2, jnp.where(wgroup == 1, s4, jnp.where(wgroup == 2, s8, s16)))
        pos = i * tm + r0 + lax.broadcasted_iota(_I32, (half, 1), 0)
        cnt = jnp.minimum(pos + half_w, seq_len) - jnp.maximum(pos - half_w, 0)
        z = wsum / cnt.astype(_F32) - u_ref[0, rows, :]
        pool = jnp.dot(z.astype(_BF16), wpool_ref[...], preferred_element_type=_F32) * pscale_ref[...]
        pieces.append(pool.astype(_BF16))

        mix = jnp.dot(jnp.concatenate(pieces, axis=1), wout_ref[...], preferred_element_type=_F32)
        x1 = x_ref[0, rows, :] + mod[2:3] * mix
        x1_ref[0, rows, :] = x1

        h2 = _rmsnorm_mod(x1, gain_ref[...], mod[4:5], mod[3:4])
        h_hi = h2.astype(_BF16)
        h2_ref[0, rows, :] = h_hi
        h_lo = (h2 - h_hi.astype(_F32)).astype(_BF16)
        nt = (((1,), (1,)), ((), ()))
        logits = (lax.dot_general(wr_hi_ref[...], h_hi, nt, preferred_element_type=_F32)
                  + lax.dot_general(wr_lo_ref[...], h_hi, nt, preferred_element_type=_F32)
                  + lax.dot_general(wr_hi_ref[...], h_lo, nt, preferred_element_type=_F32))
        aff = 1.0 / (1.0 + jnp.exp(-logits))
        cls_ref[0, :, rows] = _route(aff + rbias_ref[...])


def _mix_out(x, attn, lses, u, mod_l, wpool_bd, pool_scale, w_out_bf16, gain, wr_hi, wr_lo, rbias, batch_offset,
             rider=None):
    B, S, _ = x.shape
    tm = MIX_TILE
    n_i = S // tm
    halo_blocks = tm // SUBLANES
    tok = lambda w: pl.BlockSpec((1, tm, w), lambda b, i: (b, i, 0))
    const = lambda shape: pl.BlockSpec(shape, lambda b, i: tuple(0 for _ in shape))
    rows = tm + 4 * SUBLANES
    return _launch(
        functools.partial(_mix_out_kernel, seq_len=S),
        grid=(B, n_i),
        in_specs=[
            tok(D_MODEL),
            *[pl.BlockSpec((1, tm // d, d * GROUP_W), lambda b, i: (b, i, 0)) for _, d in DILATED_GROUPS],
            *[pl.BlockSpec((1, tm // d, d * HEADS_PER_GROUP), lambda b, i: (b, i, 0)) for _, d in DILATED_GROUPS],
            tok(POOL_W),
            pl.BlockSpec((1, SUBLANES, POOL_W), lambda b, i: (b, jnp.maximum(i * halo_blocks - 1, 0), 0)),
            pl.BlockSpec((1, SUBLANES, POOL_W),
                         lambda b, i: (b, jnp.minimum((i + 1) * halo_blocks, S // SUBLANES - 1), 0)),
            pl.BlockSpec((1, 6, D_MODEL), lambda b, i: (b + batch_offset, 0, 0)),
            const((POOL_W, POOL_W)), const((1, POOL_W)), const((D_MODEL, D_MODEL)), const((1, D_MODEL)),
            const((N_EXPERTS, D_MODEL)), const((N_EXPERTS, D_MODEL)), const((N_EXPERTS, 1)),
        ],
        out_specs=[tok(D_MODEL), tok(D_MODEL), pl.BlockSpec((1, 1, tm), lambda b, i: (b, 0, i))],
        out_shape=[jax.ShapeDtypeStruct((B, S, D_MODEL), _F32),
                   jax.ShapeDtypeStruct((B, S, D_MODEL), _BF16),
                   jax.ShapeDtypeStruct((B, 1, S), _I32)],
        scratch=[pltpu.VMEM((rows, POOL_W), _F32)] * 4 + [pltpu.VMEM((tm, LANES), _F32)] * 6,
        args=(x, *attn, *lses, u, u, u, mod_l, wpool_bd, pool_scale, w_out_bf16, gain, wr_hi, wr_lo, rbias),
        name="mix_out",
        rider=rider,
    )


TBL_EXPERT_A, TBL_EXPERT_B, TBL_TILE, TBL_USED, TBL_LAST_TILE_ROW = range(5)


def _plan_kernel(cls_ref, slot_ref, tbl_ref, cnt_ref, run_ref, base_ref, *, tile):
    phase, b, j = pl.program_id(0), pl.program_id(1), pl.program_id(2)
    first = (b == 0) & (j == 0)
    chunk = cls_ref.shape[2]
    n_tbl = tbl_ref.shape[1]
    onehot = lax.broadcasted_iota(_I32, (CLASS_ROWS, chunk), 0) == cls_ref[0]
    hits = jnp.sum(onehot.astype(_F32), axis=1, keepdims=True)

    @pl.when((phase == 0) & first)
    def _():
        cnt_ref[...] = jnp.zeros_like(cnt_ref)

    @pl.when(phase == 0)
    def _():
        cnt_ref[...] += hits

    @pl.when((phase == 1) & first)
    def _():
        padded = jnp.ceil(cnt_ref[...] * (1.0 / tile)) * tile
        r_i = lax.broadcasted_iota(_I32, (CLASS_ROWS, CLASS_ROWS), 0)
        c_i = lax.broadcasted_iota(_I32, (CLASS_ROWS, CLASS_ROWS), 1)
        padded_row = jnp.sum(jnp.where(r_i == c_i, padded, 0.0), axis=0, keepdims=True)
        base_col = jnp.sum(jnp.where(c_i < r_i, padded_row, 0.0), axis=1, keepdims=True)
        base_row = jnp.sum(jnp.where(r_i < c_i, padded, 0.0), axis=0, keepdims=True)
        total = jnp.sum(padded_row, axis=1, keepdims=True)
        base_ref[...] = base_col
        run_ref[...] = jnp.zeros_like(run_ref)

        k_row = lax.broadcasted_iota(_I32, (1, n_tbl), 1).astype(_F32)
        start = k_row * tile
        inside = (base_col <= start) & (start < base_col + padded)
        cls_id = lax.broadcasted_iota(_I32, (CLASS_ROWS, n_tbl), 0).astype(_F32)
        tile_cls = jnp.sum(jnp.where(inside, cls_id, 0.0), axis=0, keepdims=True)
        n_used = total * (1.0 / tile)
        last = jnp.maximum(n_used - 1.0, 0.0)
        last_cls = jnp.sum(jnp.where(k_row == last, tile_cls, 0.0), axis=1, keepdims=True)
        used = k_row < n_used
        tile_cls = jnp.where(used, tile_cls, last_cls)
        grp = sum(jnp.where(tile_cls >= PAIRS_PER_GROUP * g, 1.0, 0.0) for g in range(1, N_EXPERT_GROUPS))
        pair = tile_cls - grp * PAIRS_PER_GROUP
        a = jnp.where(pair < 3, 0.0, jnp.where(pair < 5, 1.0, 2.0))
        bb = jnp.where(pair < 3, pair + 1.0, jnp.where(pair < 5, pair - 1.0, 3.0))
        last_tile_row = jnp.where(padded_row > 0, base_row + padded_row - tile, -1.0)
        last_tile_row = jnp.concatenate(
            [last_tile_row, jnp.full((1, n_tbl - CLASS_ROWS), -1.0, _F32)], axis=1)
        zero = jnp.zeros((1, n_tbl), _F32)
        tbl_ref[...] = jnp.concatenate(
            [grp * EXPERTS_PER_GROUP + a, grp * EXPERTS_PER_GROUP + bb, jnp.where(used, k_row, last),
             jnp.where(used, 1.0, 0.0), last_tile_row, zero, zero, zero], axis=0).astype(_I32)

    @pl.when(phase == 1)
    def _():
        before = lax.broadcasted_iota(_I32, (chunk, chunk), 0) < lax.broadcasted_iota(_I32, (chunk, chunk), 1)
        earlier = jnp.dot(jnp.where(onehot, 1.0, 0.0).astype(_BF16), jnp.where(before, 1.0, 0.0).astype(_BF16),
                          preferred_element_type=_F32)
        dest = base_ref[...] + run_ref[...] + earlier
        slot_ref[0] = jnp.sum(jnp.where(onehot, dest, 0.0), axis=0, keepdims=True).astype(_I32)
        run_ref[...] += hits


def _moe_plan(cls, tile):
    B, _, S = cls.shape
    chunk = PLAN_CHUNK
    n_tiles = (B * S) // tile + N_CLASSES
    n_tbl = -(-n_tiles // LANES) * LANES
    slot, tbl = pl.pallas_call(
        functools.partial(_plan_kernel, tile=tile),
        grid=(2, B, S // chunk),
        in_specs=[pl.BlockSpec((1, 1, chunk), lambda p, b, j: (b, 0, j))],
        out_specs=[pl.BlockSpec((1, 1, chunk), lambda p, b, j: (b * p, 0, j * p)),
                   pl.BlockSpec((SUBLANES, n_tbl), lambda p, b, j: (0, 0))],
        out_shape=[jax.ShapeDtypeStruct((B, 1, S), _I32), jax.ShapeDtypeStruct((SUBLANES, n_tbl), _I32)],
        scratch_shapes=[pltpu.VMEM((CLASS_ROWS, 1), _F32)] * 3,
        compiler_params=_params(("arbitrary", "arbitrary", "arbitrary")),
        name="moe_plan",
    )(cls)
    return slot, tbl, n_tiles


def _row_copy(src, src_row, dst, dst_row, sem):
    return pltpu.make_async_copy(src.at[pl.ds(pl.multiple_of(src_row * ROW_CHUNKS, ROW_CHUNKS), ROW_CHUNKS), :],
                                 dst.at[pl.ds(pl.multiple_of(dst_row * ROW_CHUNKS, ROW_CHUNKS), ROW_CHUNKS), :],
                                 sem)


def _dispatch_start(tbl_ref, slot_ref, h_ref, hs_ref, stage_ref, zero_ref, sem, zsem, *, tile):
    first = (pl.program_id(0) == 0) & (pl.program_id(1) == 0)
    tm = h_ref.shape[1]
    tile_rows = tile * ROW_CHUNKS

    @pl.when(first)
    def _():
        zero_ref[...] = jnp.zeros_like(zero_ref)
        n_tiles = hs_ref.shape[0] // tile_rows
        for wait in (False, True):
            for c in range(N_CLASSES):
                row = tbl_ref[TBL_LAST_TILE_ROW, c]

                @pl.when(row >= 0)
                def _():
                    at = pl.multiple_of(row * ROW_CHUNKS, ROW_CHUNKS)
                    cp = pltpu.make_async_copy(zero_ref, hs_ref.at[pl.ds(at, tile_rows), :], zsem)
                    cp.wait() if wait else cp.start()

            for k in range(n_tiles - N_CLASSES, n_tiles):
                @pl.when(tbl_ref[TBL_USED, k] == 0)
                def _():
                    cp = pltpu.make_async_copy(zero_ref, hs_ref.at[pl.ds(k * tile_rows, tile_rows), :], zsem)
                    cp.wait() if wait else cp.start()

    h = h_ref[0]
    for c in range(ROW_CHUNKS):
        stage_ref[pl.ds(c, tm, stride=ROW_CHUNKS), :] = h[:, c * LANES:(c + 1) * LANES].astype(_F32)

    def start_slice(j, n_j):
        for r in range(j * tm // n_j, (j + 1) * tm // n_j):
            _row_copy(stage_ref, r, hs_ref, slot_ref[0, 0, r], sem).start(priority=r % 2)

    return start_slice


def _dispatch_wait(stage_ref, hs_ref, sem):
    pltpu.make_async_copy(stage_ref, hs_ref.at[pl.ds(0, stage_ref.shape[0]), :], sem).wait()


def _launch(kernel_fn, grid, in_specs, out_specs, out_shape, scratch, args, name, rider=None):
    params = _params(("arbitrary",) * len(grid))
    if rider is None:
        return pl.pallas_call(kernel_fn, grid=grid, in_specs=in_specs, out_specs=out_specs, out_shape=out_shape,
                              scratch_shapes=scratch, compiler_params=params, name=name)(*args)
    tbl, slot, h2, n_tiles = rider
    tile = MOE_TILE
    steps = grid[0] * grid[1]
    Bo, So, _ = h2.shape
    share = (Bo * So) // steps
    per_seq = So // share
    n_in, n_out, n_scr = len(in_specs), len(out_specs), len(scratch)

    def drop_tbl(spec):
        if spec.index_map is None:
            return spec
        return pl.BlockSpec(spec.block_shape, lambda *a, f=spec.index_map: f(*a[:-1]), memory_space=spec.memory_space)

    def kern(tbl_ref, *refs):
        host_in, (slot_ref, h_ref) = refs[:n_in], refs[n_in:n_in + 2]
        host_out, hs_ref = refs[n_in + 2:n_in + 2 + n_out], refs[n_in + 2 + n_out]
        host_scr = refs[n_in + 3 + n_out:n_in + 3 + n_out + n_scr]
        stage_ref, zero_ref, sem, zsem = refs[n_in + 3 + n_out + n_scr:]
        start_slice = _dispatch_start(tbl_ref, slot_ref, h_ref, hs_ref, stage_ref, zero_ref, sem, zsem, tile=tile)
        kernel_fn(*host_in, *host_out, *host_scr, rider_between=start_slice)
        _dispatch_wait(stage_ref, hs_ref, sem)

    step_of = lambda b, i: b * grid[1] + i
    rider_in = [pl.BlockSpec((1, 1, share), lambda b, i, t: (step_of(b, i) // per_seq, 0, step_of(b, i) % per_seq),
                             memory_space=pltpu.SMEM),
                pl.BlockSpec((1, share, D_MODEL), lambda b, i, t: (step_of(b, i) // per_seq, step_of(b, i) % per_seq, 0))]
    return pl.pallas_call(
        kern,
        grid_spec=pltpu.PrefetchScalarGridSpec(
            num_scalar_prefetch=1,
            grid=grid,
            in_specs=[drop_tbl(s) for s in in_specs] + rider_in,
            out_specs=[drop_tbl(s) for s in out_specs] + [pl.BlockSpec(memory_space=pl.ANY)],
            scratch_shapes=list(scratch) + [pltpu.VMEM((share * ROW_CHUNKS, LANES), _F32),
                                            pltpu.VMEM((tile * ROW_CHUNKS, LANES), _F32),
                                            pltpu.SemaphoreType.DMA(()), pltpu.SemaphoreType.DMA(())],
        ),
        out_shape=list(out_shape) + [jax.ShapeDtypeStruct((n_tiles * tile * ROW_CHUNKS, LANES), _F32)],
        compiler_params=params,
        name=name + "_dispatch",
    )(tbl, *args, slot, h2)


EXPERT_TILE_INPUTS = 11


def _experts_kernel(tbl_ref, *refs):
    n = EXPERT_TILE_INPUTS
    ins = (refs[:n], refs[n:2 * n])
    ys_ref = refs[2 * n]
    rows = ys_ref.shape[0] // 2
    k = pl.program_id(0)
    used = [tbl_ref[TBL_USED, 2 * k + j] for j in range(2)]

    def idle(j):
        ys_ref[j * rows:(j + 1) * rows, :] = jnp.zeros((rows, LANES), _F32)

    @pl.when(used[1] == 1)
    def _():
        for j in range(2):
            _expert_tile(*ins[j], ys_ref, j * rows)

    @pl.when((used[0] == 1) & (used[1] == 0))
    def _():
        _expert_tile(*ins[0], ys_ref, 0)
        idle(1)

    @pl.when(used[0] == 0)
    def _():
        idle(0)
        idle(1)


def _expert_tile(hs_ref, wra_hi_ref, wra_lo_ref, wrb_hi_ref, wrb_lo_ref,
                 wga_ref, wua_ref, wda_ref, wgb_ref, wub_ref, wdb_ref, ys_ref, row0):
    tile = hs_ref.shape[0] // ROW_CHUNKS
    chunks = [hs_ref[pl.ds(c, tile, stride=ROW_CHUNKS), :] for c in range(ROW_CHUNKS)]
    x = jnp.concatenate([ch.astype(_BF16) for ch in chunks], axis=1)

    def affinity(hi_ref, lo_ref):
        w = hi_ref[0] + lo_ref[0]
        part = chunks[0] * w[:, 0:LANES]
        for c in range(1, ROW_CHUNKS):
            part = part + chunks[c] * w[:, c * LANES:(c + 1) * LANES]
        return 1.0 / (1.0 + jnp.exp(-jnp.sum(part, axis=1, keepdims=True)))

    aff_a = affinity(wra_hi_ref, wra_lo_ref)
    aff_b = affinity(wrb_hi_ref, wrb_lo_ref)
    den = aff_a + aff_b

    def hidden(wg_ref, wu_ref, gate):
        a = jnp.dot(x, wg_ref[0], preferred_element_type=_F32)
        b = jnp.dot(x, wu_ref[0], preferred_element_type=_F32)
        return ((a / (1.0 + jnp.exp(-a))) * b * gate).astype(_BF16)

    y = (jnp.dot(hidden(wga_ref, wua_ref, aff_a / den), wda_ref[0], preferred_element_type=_F32)
         + jnp.dot(hidden(wgb_ref, wub_ref, aff_b / den), wdb_ref[0], preferred_element_type=_F32))
    for c in range(ROW_CHUNKS):
        ys_ref[pl.ds(row0 + c, tile, stride=ROW_CHUNKS), :] = y[:, c * LANES:(c + 1) * LANES]


def _moe_experts(tbl, hs, wr_hi, wr_lo, wg, wu, wd, layer, n_tiles, tile):
    rows = tile * ROW_CHUNKS
    first = layer * N_EXPERTS
    a, b = TBL_EXPERT_A, TBL_EXPERT_B
    in_specs = []
    for j in range(2):
        router = lambda row, j=j: pl.BlockSpec((1, 1, D_MODEL), lambda k, tbl: (tbl[row, 2 * k + j], 0, 0))
        up = lambda row, j=j: pl.BlockSpec((1, D_MODEL, D_EXPERT),
                                           lambda k, tbl: (first + tbl[row, 2 * k + j], 0, 0))
        down = lambda row, j=j: pl.BlockSpec((1, D_EXPERT, D_MODEL),
                                             lambda k, tbl: (first + tbl[row, 2 * k + j], 0, 0))
        in_specs += [pl.BlockSpec((rows, LANES), lambda k, tbl, j=j: (tbl[TBL_TILE, 2 * k + j], 0)),
                     router(a), router(a), router(b), router(b), up(a), up(a), down(a), up(b), up(b), down(b)]
    tile_args = (hs, wr_hi, wr_lo, wr_hi, wr_lo, wg, wu, wd, wg, wu, wd)
    return pl.pallas_call(
        _experts_kernel,
        grid_spec=pltpu.PrefetchScalarGridSpec(
            num_scalar_prefetch=1,
            grid=(n_tiles // 2,),
            in_specs=in_specs,
            out_specs=pl.BlockSpec((2 * rows, LANES), lambda k, tbl: (k, 0)),
        ),
        out_shape=jax.ShapeDtypeStruct(hs.shape, _F32),
        compiler_params=_params(("arbitrary",)),
        name="moe_experts",
    )(tbl, *tile_args, *tile_args)


def _combine_kernel(slot_ref, slot_next_ref, x1_ref, mod_ref, ys_ref, *refs, project, rider_between=None):
    if project:
        (modn_ref, gain_ref, w_ref, cos_ref, sin_ref, x2_ref, *out_refs) = refs[:16]
        buf_ref, sem, stage_ref = refs[16:19]
    else:
        gfin_ref, x2_ref, buf_ref, sem = refs[:4]
    tm = x1_ref.shape[1]
    step = pl.program_id(0) * pl.num_programs(1) + pl.program_id(1)
    last = pl.num_programs(0) * pl.num_programs(1) - 1
    gate = mod_ref[0][5:6]

    def gather(sref):
        def issue(g, carry):
            for k in range(ROWS_PER_TRIP):
                r = g * ROWS_PER_TRIP + k
                _row_copy(ys_ref, sref[0, 0, r], buf_ref, r, sem).start(priority=k % 2)
            return carry
        lax.fori_loop(0, tm // ROWS_PER_TRIP, issue, 0)

    @pl.when(step == 0)
    def _():
        gather(slot_ref)

    pltpu.make_async_copy(ys_ref.at[pl.ds(0, tm * ROW_CHUNKS), :], buf_ref, sem).wait()
    for c in range(ROW_CHUNKS):
        cols = slice(c * LANES, (c + 1) * LANES)
        x2_ref[0, :, cols] = x1_ref[0, :, cols] + gate[:, cols] * buf_ref[pl.ds(c, tm, stride=ROW_CHUNKS), :]

    x2 = x2_ref[0]
    if project:
        def start_slice(j, n_j):
            for r in range(j * tm // n_j, (j + 1) * tm // n_j):
                _row_copy(ys_ref, slot_next_ref[0, 0, r], buf_ref, r, sem).start(priority=r % 2)
            if rider_between is not None:
                rider_between(j, n_j)

        _project(x2, 0, modn_ref[0], gain_ref, w_ref, cos_ref, sin_ref, out_refs[0:3], out_refs[3:6],
                 out_refs[6:9], out_refs[9], stage_ref, between=start_slice)

        @pl.when(step == last)
        def _():
            pltpu.make_async_copy(ys_ref.at[pl.ds(0, tm * ROW_CHUNKS), :], buf_ref, sem).wait()
    else:
        @pl.when(step < last)
        def _():
            gather(slot_next_ref)

        if rider_between is not None:
            rider_between(0, 1)
        x2_ref[0] = x2 * lax.rsqrt(jnp.mean(x2 * x2, axis=-1, keepdims=True) + RMS_EPS) * gfin_ref[...]


def _moe_combine(slot, x1, mod_prev, ys, batch_offset, tail_args, rider=None):
    B, S, _ = x1.shape
    tm = TOKEN_TILE
    n_i = S // tm
    project = len(tail_args) > 1

    def next_step(b, i):
        n = jnp.minimum(b * n_i + i + 1, B * n_i - 1)
        return n // n_i, 0, n % n_i

    tok = pl.BlockSpec((1, tm, D_MODEL), lambda b, i: (b, i, 0))
    mod_spec = pl.BlockSpec((1, 6, D_MODEL), lambda b, i: (b + batch_offset, 0, 0))
    in_specs = [pl.BlockSpec((1, 1, tm), lambda b, i: (b, 0, i), memory_space=pltpu.SMEM),
                pl.BlockSpec((1, 1, tm), next_step, memory_space=pltpu.SMEM),
                tok, mod_spec, pl.BlockSpec(memory_space=pl.ANY)]
    out_specs, out_shape = [tok], [jax.ShapeDtypeStruct((B, S, D_MODEL), _F32)]
    scratch = [pltpu.VMEM((tm * ROW_CHUNKS, LANES), _F32), pltpu.SemaphoreType.DMA(())]
    if project:
        in_specs += [mod_spec,
                     pl.BlockSpec((1, D_MODEL), lambda b, i: (0, 0)),
                     pl.BlockSpec((D_MODEL, IN_W), lambda b, i: (0, 0)),
                     pl.BlockSpec((tm, LANES), lambda b, i: (i, 0)),
                     pl.BlockSpec((tm, LANES), lambda b, i: (i, 0))]
        out_specs += [pl.BlockSpec((1, tm // d, d * GROUP_W), lambda b, i: (b, i, 0)) for _, d in DILATED_GROUPS] * 3
        out_specs += [pl.BlockSpec((1, tm, POOL_W), lambda b, i: (b, i, 0))]
        out_shape += [jax.ShapeDtypeStruct((B, S // d, d * GROUP_W), _BF16) for _, d in DILATED_GROUPS] * 3
        out_shape += [jax.ShapeDtypeStruct((B, S, POOL_W), _F32)]
        scratch += [pltpu.VMEM((tm, LANES), _F32)]
    else:
        in_specs += [pl.BlockSpec((1, D_MODEL), lambda b, i: (0, 0))]
    return _launch(
        functools.partial(_combine_kernel, project=project),
        grid=(B, n_i),
        in_specs=in_specs,
        out_specs=out_specs,
        out_shape=out_shape,
        scratch=scratch,
        args=(slot, slot, x1, mod_prev, ys, *tail_args),
        name="moe_combine_in_proj" if project else "moe_combine_norm",
        rider=rider,
    )


def _rope_tables(S):
    inv = 1.0 / (ROPE_THETA ** (jnp.arange(0, HEAD_DIM, 2, dtype=_F32) / HEAD_DIM))
    ang = jnp.arange(S, dtype=_F32)[:, None] * inv[None, :]
    cos, sin = jnp.cos(ang), jnp.sin(ang)
    reps = LANES // HEAD_DIM
    return (jnp.tile(jnp.concatenate([cos, cos], axis=1), (1, reps)),
            jnp.tile(jnp.concatenate([-sin, sin], axis=1), (1, reps)))


def _trunks(xs, batch_offsets, mod, w):
    tables = [_rope_tables(x.shape[1]) for x in xs]
    outs = [_in_proj(x, mod[0], w["norm_mix"][0], w["w_in"][0], *tables[t], batch_offsets[t])
            for t, x in enumerate(xs)]
    xs = list(xs)
    for l in range(DEPTH):
        mod_l = mod[l]

        def mix(t, rider):
            q, k, v, u = outs[t][0:3], outs[t][3:6], outs[t][6:9], outs[t][9]
            attn, lses = [], []
            for g, (window, dilation) in enumerate(DILATED_GROUPS):
                o, lse = _attention(q[g], k[g], v[g], window, dilation)
                attn.append(o)
                lses.append(lse)
            return _mix_out(xs[t], attn, lses, u, mod_l, w["wpool_bd"][l], w["pool_scale"][l], w["w_out"][l],
                            w["norm_ffn"][l], w["wr_hi"], w["wr_lo"], w["rbias"], batch_offsets[t], rider)

        def experts(tbl, hs, n_tiles):
            return _moe_experts(tbl, hs, w["wr_hi3"], w["wr_lo3"], w["w_gate"], w["w_up"], w["w_down"], l,
                                n_tiles, MOE_TILE)

        def combine(t, slot, x1, ys, rider):
            if l + 1 < DEPTH:
                tail = (mod[l + 1], w["norm_mix"][l + 1], w["w_in"][l + 1], *tables[t])
            else:
                tail = (w["norm_final"],)
            return _moe_combine(slot, x1, mod_l, ys, batch_offsets[t], tail, rider)

        x1_0, h2_0, cls_0 = mix(0, None)
        slot_0, tbl_0, n_0 = _moe_plan(cls_0, MOE_TILE)
        x1_1, h2_1, cls_1, hs_0 = mix(1, (tbl_0, slot_0, h2_0, n_0))
        slot_1, tbl_1, n_1 = _moe_plan(cls_1, MOE_TILE)
        ys_0 = experts(tbl_0, hs_0, n_0)
        xs[0], *rest = combine(0, slot_0, x1_0, ys_0, (tbl_1, slot_1, h2_1, n_1))
        outs[0], hs_1 = rest[:-1], rest[-1]
        ys_1 = experts(tbl_1, hs_1, n_1)
        xs[1], *outs[1] = combine(1, slot_1, x1_1, ys_1, None)
    return xs


def kernel(x_prompt, x_sample, c_prompt, c_sample, norm_mix, w_mod, b_mod, w_in, w_pool, pool_scale, w_out,
           norm_ffn, w_router, router_bias, w_gate, w_up, w_down, norm_final):
    n_prompt = x_prompt.shape[0]
    c = jnp.concatenate([c_prompt, c_sample], axis=0)
    mod = _modulation(c, w_mod, b_mod).reshape(DEPTH, c.shape[0], 6, D_MODEL)
    n_pool = len(POOL_WINDOWS)
    eye = jnp.eye(n_pool, dtype=w_pool.dtype)
    wpool_bd = (w_pool[:, :, :, None, :] * eye[None, :, None, :, None]).reshape(DEPTH, POOL_W, POOL_W)
    wr_t = w_router.T.astype(_F32)
    wr_hi = wr_t.astype(_BF16)
    wr_lo = (wr_t - wr_hi.astype(_F32)).astype(_BF16)
    w = {
        "norm_mix": norm_mix.reshape(DEPTH, 1, D_MODEL),
        "norm_ffn": norm_ffn.reshape(DEPTH, 1, D_MODEL),
        "norm_final": norm_final.reshape(1, D_MODEL),
        "w_in": w_in.astype(_BF16),
        "w_out": w_out.astype(_BF16),
        "wpool_bd": wpool_bd.astype(_BF16),
        "pool_scale": pool_scale.reshape(DEPTH, 1, POOL_W),
        "wr_hi": wr_hi,
        "wr_lo": wr_lo,
        "wr_hi3": wr_hi.astype(_F32).reshape(N_EXPERTS, 1, D_MODEL),
        "wr_lo3": wr_lo.astype(_F32).reshape(N_EXPERTS, 1, D_MODEL),
        "rbias": router_bias.astype(_F32).reshape(N_EXPERTS, 1),
        "w_gate": w_gate.astype(_BF16).reshape(DEPTH * N_EXPERTS, D_MODEL, D_EXPERT),
        "w_up": w_up.astype(_BF16).reshape(DEPTH * N_EXPERTS, D_MODEL, D_EXPERT),
        "w_down": w_down.astype(_BF16).reshape(DEPTH * N_EXPERTS, D_EXPERT, D_MODEL),
    }
    y_prompt, y_sample = _trunks((x_prompt, x_sample), (0, n_prompt), mod, w)
    return (y_prompt, y_sample)
```

```python
import functools

import jax
import jax.numpy as jnp
from jax import lax
from jax.experimental import pallas as pl
from jax.experimental.pallas import tpu as pltpu

D_MODEL = 1024
DEPTH = 4
HEAD_DIM = 64
HALF_HEAD = HEAD_DIM // 2
DILATED_GROUPS = ((128, 1), (512, 4), (2048, 16))
N_GROUPS = len(DILATED_GROUPS)
HEADS_PER_GROUP = 4
GROUP_W = HEADS_PER_GROUP * HEAD_DIM
ATT_W = N_GROUPS * GROUP_W
POOL_WINDOWS = (2, 4, 8, 16)
POOL_GROUP_W = 64
POOL_W = POOL_GROUP_W * len(POOL_WINDOWS)
IN_W = 3 * ATT_W + POOL_W
ROPE_THETA = 10000.0
RMS_EPS = 1e-6
N_EXPERTS = 16
N_EXPERT_GROUPS = 4
EXPERTS_PER_GROUP = N_EXPERTS // N_EXPERT_GROUPS
PAIRS_PER_GROUP = EXPERTS_PER_GROUP * (EXPERTS_PER_GROUP - 1) // 2
N_CLASSES = N_EXPERT_GROUPS * PAIRS_PER_GROUP
D_EXPERT = 512

LANES = 128
SUBLANES = 8
Q_BLOCK = 128
VMEM_LIMIT = 48 * 1024 * 1024

TOKEN_TILE = 512
MOE_TILE = 256
MIX_TILE = 1024
MIX_CHUNKS = 2
PLAN_CHUNK = 512
CLASS_ROWS = 32
ROW_CHUNKS = D_MODEL // LANES
ROWS_PER_TRIP = 32

_BF16 = jnp.bfloat16
_F32 = jnp.float32
_I32 = jnp.int32


def _params(semantics):
    return pltpu.CompilerParams(dimension_semantics=semantics, vmem_limit_bytes=VMEM_LIMIT)


def _mod_kernel(c_ref, w_ref, b_ref, o_ref):
    c = c_ref[...]
    sc = c / (1.0 + jnp.exp(-c))
    o_ref[0] = jnp.dot(sc, w_ref[0], preferred_element_type=_F32,
                       precision=lax.Precision.HIGHEST) + b_ref[0]


def _modulation(c, w_mod, b_mod):
    nb = c.shape[0]
    col = D_MODEL
    n_col = w_mod.shape[2] // col
    return pl.pallas_call(
        _mod_kernel,
        grid=(DEPTH, n_col),
        in_specs=[
            pl.BlockSpec((nb, D_MODEL), lambda l, j: (0, 0)),
            pl.BlockSpec((1, D_MODEL, col), lambda l, j: (l, 0, j)),
            pl.BlockSpec((1, 1, col), lambda l, j: (l, 0, j)),
        ],
        out_specs=pl.BlockSpec((1, nb, col), lambda l, j: (l, 0, j)),
        out_shape=jax.ShapeDtypeStruct((DEPTH, nb, w_mod.shape[2]), _F32),
        compiler_params=_params(("arbitrary", "arbitrary")),
        name="modulation",
    )(c, w_mod, b_mod.reshape(DEPTH, 1, -1))


def _rmsnorm_mod(x, gain, scale, shift):
    y = x * lax.rsqrt(jnp.mean(x * x, axis=-1, keepdims=True) + RMS_EPS)
    return y * (gain * (1.0 + scale)) + shift


def _rope_chunk(t, cos, sin_signed, first_half):
    fwd = pltpu.roll(t, HALF_HEAD, axis=1)
    bwd = pltpu.roll(t, LANES - HALF_HEAD, axis=1)
    return t * cos + jnp.where(first_half, bwd, fwd) * sin_signed


def _store_by_residue(out_ref, chunk, value, dilation, stage_ref, row0):
    rows = value.shape[0] // dilation
    at = slice(row0 // dilation, row0 // dilation + rows)
    if dilation == 1:
        out_ref[0, at, chunk * LANES:(chunk + 1) * LANES] = value.astype(_BF16)
        return
    stage_ref[...] = value
    for r in range(dilation):
        lo = r * GROUP_W + chunk * LANES
        out_ref[0, at, lo:lo + LANES] = stage_ref[pl.ds(r, rows, stride=dilation), :].astype(_BF16)


def _in_proj_kernel(x_ref, mod_ref, gain_ref, w_ref, cos_ref, sin_ref, *refs):
    _project(x_ref[0], 0, mod_ref[0], gain_ref, w_ref, cos_ref, sin_ref, refs[0:3], refs[3:6], refs[6:9], refs[9],
             refs[10])


def _project(x, row0, mod, gain_ref, w_ref, cos_ref, sin_ref, q_refs, k_refs, v_refs, u_ref, stage_ref,
             between=None):
    n = x.shape[0]
    n_chunks = N_GROUPS * (GROUP_W // LANES)
    h = _rmsnorm_mod(x, gain_ref[...], mod[1:2], mod[0:1])
    proj = jnp.dot(h.astype(_BF16), w_ref[...], preferred_element_type=_F32)
    cos = cos_ref[row0:row0 + n, :]
    sin_signed = sin_ref[row0:row0 + n, :]
    lane = lax.broadcasted_iota(_I32, (1, LANES), 1)
    first_half = (lane % HEAD_DIM) < HALF_HEAD
    q_scale = HEAD_DIM ** -0.5
    for g, (_, dilation) in enumerate(DILATED_GROUPS):
        for c in range(GROUP_W // LANES):
            lo = g * GROUP_W + c * LANES
            if between is not None:
                between(g * (GROUP_W // LANES) + c, n_chunks)
            q = _rope_chunk(proj[:, lo:lo + LANES], cos, sin_signed, first_half)
            _store_by_residue(q_refs[g], c, q * q_scale, dilation, stage_ref, row0)
            k = _rope_chunk(proj[:, ATT_W + lo:ATT_W + lo + LANES], cos, sin_signed, first_half)
            _store_by_residue(k_refs[g], c, k, dilation, stage_ref, row0)
            _store_by_residue(v_refs[g], c, proj[:, 2 * ATT_W + lo:2 * ATT_W + lo + LANES], dilation, stage_ref,
                              row0)
    u_ref[0, row0:row0 + n, :] = proj[:, 3 * ATT_W:]


def _in_proj(x, mod_l, gain, w_in_bf16, cos, sin_signed, batch_offset):
    B, S, _ = x.shape
    tm = TOKEN_TILE
    grp = [jax.ShapeDtypeStruct((B, S // d, d * GROUP_W), _BF16) for _, d in DILATED_GROUPS]
    grp_spec = [pl.BlockSpec((1, tm // d, d * GROUP_W), lambda b, i: (b, i, 0)) for _, d in DILATED_GROUPS]
    return pl.pallas_call(
        _in_proj_kernel,
        grid=(B, S // tm),
        in_specs=[
            pl.BlockSpec((1, tm, D_MODEL), lambda b, i: (b, i, 0)),
            pl.BlockSpec((1, 6, D_MODEL), lambda b, i: (b + batch_offset, 0, 0)),
            pl.BlockSpec((1, D_MODEL), lambda b, i: (0, 0)),
            pl.BlockSpec((D_MODEL, IN_W), lambda b, i: (0, 0)),
            pl.BlockSpec((tm, LANES), lambda b, i: (i, 0)),
            pl.BlockSpec((tm, LANES), lambda b, i: (i, 0)),
        ],
        out_specs=grp_spec * 3 + [pl.BlockSpec((1, tm, POOL_W), lambda b, i: (b, i, 0))],
        out_shape=grp * 3 + [jax.ShapeDtypeStruct((B, S, POOL_W), _F32)],
        scratch_shapes=[pltpu.VMEM((tm, LANES), _F32)],
        compiler_params=_params(("arbitrary", "arbitrary")),
        name="in_proj",
    )(x, mod_l, gain, w_in_bf16, cos, sin_signed)


def _attn_kernel(q_ref, k_ref, v_ref, o_ref, lse_ref, bias_ref, *, length, n_side, n_res, q_unroll):
    rb = pl.program_id(1)
    key_w = min(length, Q_BLOCK + 2 * n_side)
    n_qb = length // Q_BLOCK

    @pl.when((pl.program_id(0) == 0) & (rb == 0))
    def _():
        i = lax.broadcasted_iota(_I32, (2 * Q_BLOCK, key_w), 0) % Q_BLOCK
        j = lax.broadcasted_iota(_I32, (2 * Q_BLOCK, key_w), 1)
        for n in range(3):
            bias_ref[n] = jnp.where(jnp.abs(i - j + n * n_side) <= n_side, 0.0, -jnp.inf)

    @pl.when(rb == 0)
    def _():
        lse_ref[...] = jnp.zeros_like(lse_ref)

    lane = lax.broadcasted_iota(_I32, (1, LANES), 1)
    head_mask = [(lane < HEAD_DIM).astype(_BF16), (lane >= HEAD_DIM).astype(_BF16)]
    lse_lane = lax.broadcasted_iota(_I32, (1, lse_ref.shape[2]), 1)

    def trip(it, carry):
        for u in range(q_unroll):
            q0 = pl.multiple_of((it * q_unroll + u) * Q_BLOCK, Q_BLOCK)
            k0 = pl.multiple_of(jnp.clip(q0 - n_side, 0, length - key_w), n_side)
            bias = bias_ref[(q0 - k0) // n_side]
            lse_rows = lse_ref[0, pl.ds(q0, Q_BLOCK), :]
            for rr in range(n_res):
                for pair in range(GROUP_W // LANES):
                    lo = rr * GROUP_W + pair * LANES
                    cols = slice(lo, lo + LANES)
                    q2 = q_ref[0, pl.ds(q0, Q_BLOCK), cols]
                    k2 = k_ref[0, pl.ds(k0, key_w), cols]
                    v2 = v_ref[0, pl.ds(k0, key_w), cols]
                    qs = jnp.concatenate([q2 * head_mask[0], q2 * head_mask[1]], axis=0)
                    s = lax.dot_general(qs, k2, (((1,), (1,)), ((), ())), preferred_element_type=_F32) + bias
                    m = jnp.max(s, axis=1, keepdims=True)
                    p = jnp.exp(s - m)
                    l = jnp.sum(p, axis=1, keepdims=True)
                    o = jnp.dot(p.astype(_BF16), v2, preferred_element_type=_F32) * (1.0 / l)
                    lse = m + jnp.log(l)
                    for hh in range(2):
                        col = (rb * n_res + rr) * HEADS_PER_GROUP + pair * 2 + hh
                        lse_rows = jnp.where(lse_lane == col, lse[hh * Q_BLOCK:(hh + 1) * Q_BLOCK], lse_rows)
                    o_ref[0, pl.ds(q0, Q_BLOCK), cols] = jnp.where(lane < HEAD_DIM, o[:Q_BLOCK], o[Q_BLOCK:]).astype(_BF16)
            lse_ref[0, pl.ds(q0, Q_BLOCK), :] = lse_rows
        return carry

    lax.fori_loop(0, n_qb // q_unroll, trip, 0)


ATTN_BLOCKS_PER_TRIP = 16


def _attention(q, k, v, window, dilation):
    B, L, _ = q.shape
    d = dilation
    n_side = window // (2 * d)
    n_res = min(d, ATTN_BLOCKS_PER_TRIP)
    q_unroll = min(ATTN_BLOCKS_PER_TRIP // n_res, L // Q_BLOCK)
    key_w = min(L, Q_BLOCK + 2 * n_side)
    spec = pl.BlockSpec((1, L, n_res * GROUP_W), lambda b, r: (b, 0, r))
    return pl.pallas_call(
        functools.partial(_attn_kernel, length=L, n_side=n_side, n_res=n_res, q_unroll=q_unroll),
        grid=(B, d // n_res),
        in_specs=[spec, spec, spec],
        out_specs=[spec, pl.BlockSpec((1, L, d * HEADS_PER_GROUP), lambda b, r: (b, 0, 0))],
        out_shape=[jax.ShapeDtypeStruct((B, L, d * GROUP_W), _BF16),
                   jax.ShapeDtypeStruct((B, L, d * HEADS_PER_GROUP), _F32)],
        scratch_shapes=[pltpu.VMEM((3, 2 * Q_BLOCK, key_w), _F32)],
        compiler_params=_params(("arbitrary", "arbitrary")),
        name=f"attention_d{d}",
    )(q, k, v)


def _route(sel):
    s = [sel[e:e + 1, :] for e in range(N_EXPERTS)]
    group_score = []
    for g in range(N_EXPERT_GROUPS):
        c0, c1, c2, c3 = s[EXPERTS_PER_GROUP * g:EXPERTS_PER_GROUP * (g + 1)]
        m1, n1 = jnp.maximum(c0, c1), jnp.minimum(c0, c1)
        m2, n2 = jnp.maximum(c2, c3), jnp.minimum(c2, c3)
        group_score.append(jnp.maximum(m1, m2) + jnp.maximum(jnp.minimum(m1, m2), jnp.maximum(n1, n2)))
    best, top_g = group_score[0], jnp.zeros(group_score[0].shape, _I32)
    for g in range(1, N_EXPERT_GROUPS):
        better = group_score[g] > best
        best = jnp.where(better, group_score[g], best)
        top_g = jnp.where(better, g, top_g)
    cs = []
    for j in range(EXPERTS_PER_GROUP):
        c = s[j]
        for g in range(1, N_EXPERT_GROUPS):
            c = jnp.where(top_g == g, s[EXPERTS_PER_GROUP * g + j], c)
        cs.append(c)
    s1, i1 = cs[0], jnp.zeros_like(top_g)
    for j in range(1, EXPERTS_PER_GROUP):
        better = cs[j] > s1
        s1 = jnp.where(better, cs[j], s1)
        i1 = jnp.where(better, j, i1)
    i2 = jnp.where(i1 == 0, 1, 0)
    s2 = jnp.where(i1 == 0, cs[1], cs[0])
    for j in range(1, EXPERTS_PER_GROUP):
        better = (cs[j] > s2) & (i1 != j)
        s2 = jnp.where(better, cs[j], s2)
        i2 = jnp.where(better, j, i2)
    lo, hi = jnp.minimum(i1, i2), jnp.maximum(i1, i2)
    pair = jnp.where(lo == 0, 0, jnp.where(lo == 1, 3, 5)) + (hi - lo - 1)
    return top_g * PAIRS_PER_GROUP + pair


def _mix_out_kernel(x_ref, o0_ref, o1_ref, o2_ref, l0_ref, l1_ref, l2_ref, u_ref, up_ref, un_ref,
                    mod_ref, wpool_ref, pscale_ref, wout_ref, gain_ref, wr_hi_ref, wr_lo_ref, rbias_ref,
                    x1_ref, h2_ref, cls_ref, ext_ref, lvl_a_ref, lvl_b_ref, lvl_c_ref, nat_o1a_ref, nat_o1b_ref,
                    nat_o2a_ref, nat_o2b_ref, nat_l1_ref, nat_l2_ref, *, seq_len, rider_between=None):
    nat_o_refs = ((nat_o1a_ref, nat_o1b_ref), (nat_o2a_ref, nat_o2b_ref))
    nat_l_refs = (nat_l1_ref, nat_l2_ref)
    i = pl.program_id(1)
    n_i = pl.num_programs(1)
    tm = x_ref.shape[1]
    mod = mod_ref[0]

    n_slices = sum(d for _, d in DILATED_GROUPS if d > 1)
    done = 0
    for g, (o_ref, l_ref) in enumerate(((o0_ref, l0_ref), (o1_ref, l1_ref), (o2_ref, l2_ref))):
        dilation = DILATED_GROUPS[g][1]
        if dilation == 1:
            continue
        rows = tm // dilation
        lv = l_ref[0]
        lv = jnp.concatenate([lv, jnp.zeros((rows, LANES - lv.shape[1]), _F32)], axis=1)
        halves = nat_o_refs[g - 1]
        for r in range(dilation):
            if rider_between is not None:
                rider_between(done, n_slices)
            done += 1
            for c, half_ref in enumerate(halves):
                lo = r * GROUP_W + c * LANES
                half_ref[pl.ds(r, rows, stride=dilation), :] = o_ref[0, :, lo:lo + LANES].astype(_F32)
            shifted = lv if r == 0 else pltpu.roll(lv, LANES - r * HEADS_PER_GROUP, axis=1)
            nat_l_refs[g - 1][pl.ds(r, rows, stride=dilation), :] = shifted

    pad = 2 * SUBLANES
    zeros_pad = jnp.zeros((SUBLANES, POOL_W), _F32)
    for ref in (ext_ref, lvl_a_ref, lvl_b_ref, lvl_c_ref):
        ref[0:SUBLANES, :] = zeros_pad
        ref[tm + pad + SUBLANES:tm + 2 * pad, :] = zeros_pad
    ext_ref[SUBLANES:pad, :] = jnp.where(i > 0, up_ref[0], 0.0)
    ext_ref[pad:pad + tm, :] = u_ref[0]
    ext_ref[pad + tm:pad + tm + SUBLANES, :] = jnp.where(i < n_i - 1, un_ref[0], 0.0)
    span = tm + 2 * SUBLANES
    win = lambda ref, off: ref[SUBLANES + off:SUBLANES + off + span, :]
    lvl_a_ref[SUBLANES:SUBLANES + span, :] = win(ext_ref, -1) + win(ext_ref, 0)
    lvl_b_ref[SUBLANES:SUBLANES + span, :] = win(lvl_a_ref, -1) + win(lvl_a_ref, 1)
    lvl_c_ref[SUBLANES:SUBLANES + span, :] = win(lvl_b_ref, -2) + win(lvl_b_ref, 2)
    lane = lax.broadcasted_iota(_I32, (1, POOL_W), 1)
    wgroup = lane // POOL_GROUP_W
    half_w = jnp.where(wgroup == 0, 1, jnp.where(wgroup == 1, 2, jnp.where(wgroup == 2, 4, 8)))

    half = tm // MIX_CHUNKS
    for hf in range(MIX_CHUNKS):
        r0 = hf * half
        rows = slice(r0, r0 + half)
        outs = [o0_ref[0, rows, :].astype(_F32)]
        lses = [l0_ref[0, rows, :]]
        for g in range(1, N_GROUPS):
            outs.append(jnp.concatenate([ref[rows, :] for ref in nat_o_refs[g - 1]], axis=1))
            lses.append(nat_l_refs[g - 1][rows, 0:HEADS_PER_GROUP])

        top = jnp.maximum(jnp.maximum(lses[0], lses[1]), lses[2])
        es = [jnp.exp(t - top) for t in lses]
        den = es[0] + es[1] + es[2]
        alpha = jnp.concatenate([e / den for e in es]
                                + [jnp.zeros((half, LANES - N_GROUPS * HEADS_PER_GROUP), _F32)], axis=1)
        a_hi = alpha.astype(_BF16)
        a_lo = (alpha - a_hi.astype(_F32)).astype(_BF16)
        head_of_lane = lax.broadcasted_iota(_I32, (LANES, ATT_W), 1) // HEAD_DIM
        spread = jnp.where(head_of_lane == lax.broadcasted_iota(_I32, (LANES, ATT_W), 0), 1.0, 0.0).astype(_BF16)
        wide = (jnp.dot(a_hi, spread, preferred_element_type=_F32)
                + jnp.dot(a_lo, spread, preferred_element_type=_F32))
        pieces = [(outs[g] * wide[:, g * GROUP_W:(g + 1) * GROUP_W]).astype(_BF16) for g in range(N_GROUPS)]

        s2 = lvl_a_ref[pad + r0:pad + r0 + half, :]
        s4 = lvl_b_ref[pad + r0:pad + r0 + half, :]
        s8 = lvl_c_ref[pad + r0:pad + r0 + half, :]
        s16 = lvl_c_ref[pad + r0 - 4:pad + r0 - 4 + half, :] + lvl_c_ref[pad + r0 + 4:pad + r0 + 4 + half, :]
        wsum = jnp.where(wgroup == 0, s2, jnp.where(wgroup == 1, s4, jnp.where(wgroup == 2, s8, s16)))
        pos = i * tm + r0 + lax.broadcasted_iota(_I32, (half, 1), 0)
        cnt = jnp.minimum(pos + half_w, seq_len) - jnp.maximum(pos - half_w, 0)
        z = wsum / cnt.astype(_F32) - u_ref[0, rows, :]
        pool = jnp.dot(z.astype(_BF16), wpool_ref[...], preferred_element_type=_F32) * pscale_ref[...]
        pieces.append(pool.astype(_BF16))

        mix = jnp.dot(jnp.concatenate(pieces, axis=1), wout_ref[...], preferred_element_type=_F32)
        x1 = x_ref[0, rows, :] + mod[2:3] * mix
        x1_ref[0, rows, :] = x1

        h2 = _rmsnorm_mod(x1, gain_ref[...], mod[4:5], mod[3:4])
        h_hi = h2.astype(_BF16)
        h2_ref[0, rows, :] = h_hi
        h_lo = (h2 - h_hi.astype(_F32)).astype(_BF16)
        nt = (((1,), (1,)), ((), ()))
        logits = (lax.dot_general(wr_hi_ref[...], h_hi, nt, preferred_element_type=_F32)
                  + lax.dot_general(wr_lo_ref[...], h_hi, nt, preferred_element_type=_F32)
                  + lax.dot_general(wr_hi_ref[...], h_lo, nt, preferred_element_type=_F32))
        aff = 1.0 / (1.0 + jnp.exp(-logits))
        cls_ref[0, :, rows] = _route(aff + rbias_ref[...])


def _mix_out(x, attn, lses, u, mod_l, wpool_bd, pool_scale, w_out_bf16, gain, wr_hi, wr_lo, rbias, batch_offset,
             rider=None):
    B, S, _ = x.shape
    tm = MIX_TILE
    n_i = S // tm
    halo_blocks = tm // SUBLANES
    tok = lambda w: pl.BlockSpec((1, tm, w), lambda b, i: (b, i, 0))
    const = lambda shape: pl.BlockSpec(shape, lambda b, i: tuple(0 for _ in shape))
    rows = tm + 4 * SUBLANES
    return _launch(
        functools.partial(_mix_out_kernel, seq_len=S),
        grid=(B, n_i),
        in_specs=[
            tok(D_MODEL),
            *[pl.BlockSpec((1, tm // d, d * GROUP_W), lambda b, i: (b, i, 0)) for _, d in DILATED_GROUPS],
            *[pl.BlockSpec((1, tm // d, d * HEADS_PER_GROUP), lambda b, i: (b, i, 0)) for _, d in DILATED_GROUPS],
            tok(POOL_W),
            pl.BlockSpec((1, SUBLANES, POOL_W), lambda b, i: (b, jnp.maximum(i * halo_blocks - 1, 0), 0)),
            pl.BlockSpec((1, SUBLANES, POOL_W),
                         lambda b, i: (b, jnp.minimum((i + 1) * halo_blocks, S // SUBLANES - 1), 0)),
            pl.BlockSpec((1, 6, D_MODEL), lambda b, i: (b + batch_offset, 0, 0)),
            const((POOL_W, POOL_W)), const((1, POOL_W)), const((D_MODEL, D_MODEL)), const((1, D_MODEL)),
            const((N_EXPERTS, D_MODEL)), const((N_EXPERTS, D_MODEL)), const((N_EXPERTS, 1)),
        ],
        out_specs=[tok(D_MODEL), tok(D_MODEL), pl.BlockSpec((1, 1, tm), lambda b, i: (b, 0, i))],
        out_shape=[jax.ShapeDtypeStruct((B, S, D_MODEL), _F32),
                   jax.ShapeDtypeStruct((B, S, D_MODEL), _BF16),
                   jax.ShapeDtypeStruct((B, 1, S), _I32)],
        scratch=[pltpu.VMEM((rows, POOL_W), _F32)] * 4 + [pltpu.VMEM((tm, LANES), _F32)] * 6,
        args=(x, *attn, *lses, u, u, u, mod_l, wpool_bd, pool_scale, w_out_bf16, gain, wr_hi, wr_lo, rbias),
        name="mix_out",
        rider=rider,
    )


TBL_EXPERT_A, TBL_EXPERT_B, TBL_TILE, TBL_USED, TBL_LAST_TILE_ROW = range(5)


def _plan_kernel(cls_ref, slot_ref, tbl_ref, cnt_ref, run_ref, base_ref, *, tile):
    phase, b, j = pl.program_id(0), pl.program_id(1), pl.program_id(2)
    first = (b == 0) & (j == 0)
    chunk = cls_ref.shape[2]
    n_tbl = tbl_ref.shape[1]
    onehot = lax.broadcasted_iota(_I32, (CLASS_ROWS, chunk), 0) == cls_ref[0]
    hits = jnp.sum(onehot.astype(_F32), axis=1, keepdims=True)

    @pl.when((phase == 0) & first)
    def _():
        cnt_ref[...] = jnp.zeros_like(cnt_ref)

    @pl.when(phase == 0)
    def _():
        cnt_ref[...] += hits

    @pl.when((phase == 1) & first)
    def _():
        padded = jnp.ceil(cnt_ref[...] * (1.0 / tile)) * tile
        r_i = lax.broadcasted_iota(_I32, (CLASS_ROWS, CLASS_ROWS), 0)
        c_i = lax.broadcasted_iota(_I32, (CLASS_ROWS, CLASS_ROWS), 1)
        padded_row = jnp.sum(jnp.where(r_i == c_i, padded, 0.0), axis=0, keepdims=True)
        base_col = jnp.sum(jnp.where(c_i < r_i, padded_row, 0.0), axis=1, keepdims=True)
        base_row = jnp.sum(jnp.where(r_i < c_i, padded, 0.0), axis=0, keepdims=True)
        total = jnp.sum(padded_row, axis=1, keepdims=True)
        base_ref[...] = base_col
        run_ref[...] = jnp.zeros_like(run_ref)

        k_row = lax.broadcasted_iota(_I32, (1, n_tbl), 1).astype(_F32)
        start = k_row * tile
        inside = (base_col <= start) & (start < base_col + padded)
        cls_id = lax.broadcasted_iota(_I32, (CLASS_ROWS, n_tbl), 0).astype(_F32)
        tile_cls = jnp.sum(jnp.where(inside, cls_id, 0.0), axis=0, keepdims=True)
        n_used = total * (1.0 / tile)
        last = jnp.maximum(n_used - 1.0, 0.0)
        last_cls = jnp.sum(jnp.where(k_row == last, tile_cls, 0.0), axis=1, keepdims=True)
        used = k_row < n_used
        tile_cls = jnp.where(used, tile_cls, last_cls)
        grp = sum(jnp.where(tile_cls >= PAIRS_PER_GROUP * g, 1.0, 0.0) for g in range(1, N_EXPERT_GROUPS))
        pair = tile_cls - grp * PAIRS_PER_GROUP
        a = jnp.where(pair < 3, 0.0, jnp.where(pair < 5, 1.0, 2.0))
        bb = jnp.where(pair < 3, pair + 1.0, jnp.where(pair < 5, pair - 1.0, 3.0))
        last_tile_row = jnp.where(padded_row > 0, base_row + padded_row - tile, -1.0)
        last_tile_row = jnp.concatenate(
            [last_tile_row, jnp.full((1, n_tbl - CLASS_ROWS), -1.0, _F32)], axis=1)
        zero = jnp.zeros((1, n_tbl), _F32)
        tbl_ref[...] = jnp.concatenate(
            [grp * EXPERTS_PER_GROUP + a, grp * EXPERTS_PER_GROUP + bb, jnp.where(used, k_row, last),
             jnp.where(used, 1.0, 0.0), last_tile_row, zero, zero, zero], axis=0).astype(_I32)

    @pl.when(phase == 1)
    def _():
        before = lax.broadcasted_iota(_I32, (chunk, chunk), 0) < lax.broadcasted_iota(_I32, (chunk, chunk), 1)
        earlier = jnp.dot(jnp.where(onehot, 1.0, 0.0).astype(_BF16), jnp.where(before, 1.0, 0.0).astype(_BF16),
                          preferred_element_type=_F32)
        dest = base_ref[...] + run_ref[...] + earlier
        slot_ref[0] = jnp.sum(jnp.where(onehot, dest, 0.0), axis=0, keepdims=True).astype(_I32)
        run_ref[...] += hits


def _moe_plan(cls, tile):
    B, _, S = cls.shape
    chunk = PLAN_CHUNK
    n_tiles = (B * S) // tile + N_CLASSES
    n_tbl = -(-n_tiles // LANES) * LANES
    slot, tbl = pl.pallas_call(
        functools.partial(_plan_kernel, tile=tile),
        grid=(2, B, S // chunk),
        in_specs=[pl.BlockSpec((1, 1, chunk), lambda p, b, j: (b, 0, j))],
        out_specs=[pl.BlockSpec((1, 1, chunk), lambda p, b, j: (b * p, 0, j * p)),
                   pl.BlockSpec((SUBLANES, n_tbl), lambda p, b, j: (0, 0))],
        out_shape=[jax.ShapeDtypeStruct((B, 1, S), _I32), jax.ShapeDtypeStruct((SUBLANES, n_tbl), _I32)],
        scratch_shapes=[pltpu.VMEM((CLASS_ROWS, 1), _F32)] * 3,
        compiler_params=_params(("arbitrary", "arbitrary", "arbitrary")),
        name="moe_plan",
    )(cls)
    return slot, tbl, n_tiles


def _row_copy(src, src_row, dst, dst_row, sem):
    return pltpu.make_async_copy(src.at[pl.ds(pl.multiple_of(src_row * ROW_CHUNKS, ROW_CHUNKS), ROW_CHUNKS), :],
                                 dst.at[pl.ds(pl.multiple_of(dst_row * ROW_CHUNKS, ROW_CHUNKS), ROW_CHUNKS), :],
                                 sem)


def _dispatch_start(tbl_ref, slot_ref, h_ref, hs_ref, stage_ref, zero_ref, sem, zsem, *, tile):
    first = (pl.program_id(0) == 0) & (pl.program_id(1) == 0)
    tm = h_ref.shape[1]
    tile_rows = tile * ROW_CHUNKS

    @pl.when(first)
    def _():
        zero_ref[...] = jnp.zeros_like(zero_ref)
        n_tiles = hs_ref.shape[0] // tile_rows
        for wait in (False, True):
            for c in range(N_CLASSES):
                row = tbl_ref[TBL_LAST_TILE_ROW, c]

                @pl.when(row >= 0)
                def _():
                    at = pl.multiple_of(row * ROW_CHUNKS, ROW_CHUNKS)
                    cp = pltpu.make_async_copy(zero_ref, hs_ref.at[pl.ds(at, tile_rows), :], zsem)
                    cp.wait() if wait else cp.start()

            for k in range(n_tiles - N_CLASSES, n_tiles):
                @pl.when(tbl_ref[TBL_USED, k] == 0)
                def _():
                    cp = pltpu.make_async_copy(zero_ref, hs_ref.at[pl.ds(k * tile_rows, tile_rows), :], zsem)
                    cp.wait() if wait else cp.start()

    h = h_ref[0]
    for c in range(ROW_CHUNKS):
        stage_ref[pl.ds(c, tm, stride=ROW_CHUNKS), :] = h[:, c * LANES:(c + 1) * LANES].astype(_F32)

    def start_slice(j, n_j):
        for r in range(j * tm // n_j, (j + 1) * tm // n_j):
            _row_copy(stage_ref, r, hs_ref, slot_ref[0, 0, r], sem).start(priority=r % 2)

    return start_slice


def _dispatch_wait(stage_ref, hs_ref, sem):
    pltpu.make_async_copy(stage_ref, hs_ref.at[pl.ds(0, stage_ref.shape[0]), :], sem).wait()


def _launch(kernel_fn, grid, in_specs, out_specs, out_shape, scratch, args, name, rider=None):
    params = _params(("arbitrary",) * len(grid))
    if rider is None:
        return pl.pallas_call(kernel_fn, grid=grid, in_specs=in_specs, out_specs=out_specs, out_shape=out_shape,
                              scratch_shapes=scratch, compiler_params=params, name=name)(*args)
    tbl, slot, h2, n_tiles = rider
    tile = MOE_TILE
    steps = grid[0] * grid[1]
    Bo, So, _ = h2.shape
    share = (Bo * So) // steps
    per_seq = So // share
    n_in, n_out, n_scr = len(in_specs), len(out_specs), len(scratch)

    def drop_tbl(spec):
        if spec.index_map is None:
            return spec
        return pl.BlockSpec(spec.block_shape, lambda *a, f=spec.index_map: f(*a[:-1]), memory_space=spec.memory_space)

    def kern(tbl_ref, *refs):
        host_in, (slot_ref, h_ref) = refs[:n_in], refs[n_in:n_in + 2]
        host_out, hs_ref = refs[n_in + 2:n_in + 2 + n_out], refs[n_in + 2 + n_out]
        host_scr = refs[n_in + 3 + n_out:n_in + 3 + n_out + n_scr]
        stage_ref, zero_ref, sem, zsem = refs[n_in + 3 + n_out + n_scr:]
        start_slice = _dispatch_start(tbl_ref, slot_ref, h_ref, hs_ref, stage_ref, zero_ref, sem, zsem, tile=tile)
        kernel_fn(*host_in, *host_out, *host_scr, rider_between=start_slice)
        _dispatch_wait(stage_ref, hs_ref, sem)

    step_of = lambda b, i: b * grid[1] + i
    rider_in = [pl.BlockSpec((1, 1, share), lambda b, i, t: (step_of(b, i) // per_seq, 0, step_of(b, i) % per_seq),
                             memory_space=pltpu.SMEM),
                pl.BlockSpec((1, share, D_MODEL), lambda b, i, t: (step_of(b, i) // per_seq, step_of(b, i) % per_seq, 0))]
    return pl.pallas_call(
        kern,
        grid_spec=pltpu.PrefetchScalarGridSpec(
            num_scalar_prefetch=1,
            grid=grid,
            in_specs=[drop_tbl(s) for s in in_specs] + rider_in,
            out_specs=[drop_tbl(s) for s in out_specs] + [pl.BlockSpec(memory_space=pl.ANY)],
            scratch_shapes=list(scratch) + [pltpu.VMEM((share * ROW_CHUNKS, LANES), _F32),
                                            pltpu.VMEM((tile * ROW_CHUNKS, LANES), _F32),
                                            pltpu.SemaphoreType.DMA(()), pltpu.SemaphoreType.DMA(())],
        ),
        out_shape=list(out_shape) + [jax.ShapeDtypeStruct((n_tiles * tile * ROW_CHUNKS, LANES), _F32)],
        compiler_params=params,
        name=name + "_dispatch",
    )(tbl, *args, slot, h2)


EXPERT_TILE_INPUTS = 11


def _experts_kernel(tbl_ref, *refs):
    n = EXPERT_TILE_INPUTS
    ins = (refs[:n], refs[n:2 * n])
    ys_ref = refs[2 * n]
    rows = ys_ref.shape[1]
    k = pl.program_id(0)
    used = [tbl_ref[TBL_USED, j * pl.num_programs(0) + k] for j in range(2)]

    def idle(j):
        ys_ref[j] = jnp.zeros((rows, LANES), _F32)

    @pl.when(used[1] == 1)
    def _():
        for j in range(2):
            _expert_tile(*ins[j], ys_ref, j)

    @pl.when((used[0] == 1) & (used[1] == 0))
    def _():
        _expert_tile(*ins[0], ys_ref, 0)
        idle(1)

    @pl.when(used[0] == 0)
    def _():
        idle(0)
        idle(1)


def _expert_tile(hs_ref, wra_hi_ref, wra_lo_ref, wrb_hi_ref, wrb_lo_ref,
                 wga_ref, wua_ref, wda_ref, wgb_ref, wub_ref, wdb_ref, ys_ref, half):
    tile = hs_ref.shape[0] // ROW_CHUNKS
    chunks = [hs_ref[pl.ds(c, tile, stride=ROW_CHUNKS), :] for c in range(ROW_CHUNKS)]
    x = jnp.concatenate([ch.astype(_BF16) for ch in chunks], axis=1)

    def affinity(hi_ref, lo_ref):
        w = hi_ref[0] + lo_ref[0]
        part = chunks[0] * w[:, 0:LANES]
        for c in range(1, ROW_CHUNKS):
            part = part + chunks[c] * w[:, c * LANES:(c + 1) * LANES]
        return 1.0 / (1.0 + jnp.exp(-jnp.sum(part, axis=1, keepdims=True)))

    aff_a = affinity(wra_hi_ref, wra_lo_ref)
    aff_b = affinity(wrb_hi_ref, wrb_lo_ref)
    den = aff_a + aff_b

    def hidden(wg_ref, wu_ref, gate):
        a = jnp.dot(x, wg_ref[0], preferred_element_type=_F32)
        b = jnp.dot(x, wu_ref[0], preferred_element_type=_F32)
        return ((a / (1.0 + jnp.exp(-a))) * b * gate).astype(_BF16)

    y = (jnp.dot(hidden(wga_ref, wua_ref, aff_a / den), wda_ref[0], preferred_element_type=_F32)
         + jnp.dot(hidden(wgb_ref, wub_ref, aff_b / den), wdb_ref[0], preferred_element_type=_F32))
    for c in range(ROW_CHUNKS):
        ys_ref[half, pl.ds(c, tile, stride=ROW_CHUNKS), :] = y[:, c * LANES:(c + 1) * LANES]


def _moe_experts(tbl, hs, wr_hi, wr_lo, wg, wu, wd, layer, n_tiles, tile):
    rows = tile * ROW_CHUNKS
    first = layer * N_EXPERTS
    a, b = TBL_EXPERT_A, TBL_EXPERT_B
    steps = n_tiles // 2
    in_specs = []
    for j in range(2):
        at = lambda k, j=j: j * steps + k
        router = lambda row, at=at: pl.BlockSpec((1, 1, D_MODEL), lambda k, tbl: (tbl[row, at(k)], 0, 0))
        up = lambda row, at=at: pl.BlockSpec((1, D_MODEL, D_EXPERT), lambda k, tbl: (first + tbl[row, at(k)], 0, 0))
        down = lambda row, at=at: pl.BlockSpec((1, D_EXPERT, D_MODEL),
                                               lambda k, tbl: (first + tbl[row, at(k)], 0, 0))
        in_specs += [pl.BlockSpec((rows, LANES), lambda k, tbl, at=at: (tbl[TBL_TILE, at(k)], 0)),
                     router(a), router(a), router(b), router(b), up(a), up(a), down(a), up(b), up(b), down(b)]
    tile_args = (hs, wr_hi, wr_lo, wr_hi, wr_lo, wg, wu, wd, wg, wu, wd)
    ys = pl.pallas_call(
        _experts_kernel,
        grid_spec=pltpu.PrefetchScalarGridSpec(
            num_scalar_prefetch=1,
            grid=(steps,),
            in_specs=in_specs,
            out_specs=pl.BlockSpec((2, rows, LANES), lambda k, tbl: (0, k, 0)),
        ),
        out_shape=jax.ShapeDtypeStruct((2, steps * rows, LANES), _F32),
        compiler_params=_params(("arbitrary",)),
        name="moe_experts",
    )(tbl, *tile_args, *tile_args)
    return ys.reshape(hs.shape)


def _combine_kernel(slot_ref, slot_next_ref, x1_ref, mod_ref, ys_ref, *refs, project, rider_between=None):
    if project:
        (modn_ref, gain_ref, w_ref, cos_ref, sin_ref, x2_ref, *out_refs) = refs[:16]
        buf_ref, sem, stage_ref = refs[16:19]
    else:
        gfin_ref, x2_ref, buf_ref, sem = refs[:4]
    tm = x1_ref.shape[1]
    step = pl.program_id(0) * pl.num_programs(1) + pl.program_id(1)
    last = pl.num_programs(0) * pl.num_programs(1) - 1
    gate = mod_ref[0][5:6]

    def gather(sref):
        def issue(g, carry):
            for k in range(ROWS_PER_TRIP):
                r = g * ROWS_PER_TRIP + k
                _row_copy(ys_ref, sref[0, 0, r], buf_ref, r, sem).start(priority=k % 2)
            return carry
        lax.fori_loop(0, tm // ROWS_PER_TRIP, issue, 0)

    @pl.when(step == 0)
    def _():
        gather(slot_ref)

    pltpu.make_async_copy(ys_ref.at[pl.ds(0, tm * ROW_CHUNKS), :], buf_ref, sem).wait()
    for c in range(ROW_CHUNKS):
        cols = slice(c * LANES, (c + 1) * LANES)
        x2_ref[0, :, cols] = x1_ref[0, :, cols] + gate[:, cols] * buf_ref[pl.ds(c, tm, stride=ROW_CHUNKS), :]

    x2 = x2_ref[0]
    if project:
        def start_slice(j, n_j):
            for r in range(j * tm // n_j, (j + 1) * tm // n_j):
                _row_copy(ys_ref, slot_next_ref[0, 0, r], buf_ref, r, sem).start(priority=r % 2)
            if rider_between is not None:
                rider_between(j, n_j)

        _project(x2, 0, modn_ref[0], gain_ref, w_ref, cos_ref, sin_ref, out_refs[0:3], out_refs[3:6],
                 out_refs[6:9], out_refs[9], stage_ref, between=start_slice)

        @pl.when(step == last)
        def _():
            pltpu.make_async_copy(ys_ref.at[pl.ds(0, tm * ROW_CHUNKS), :], buf_ref, sem).wait()
    else:
        @pl.when(step < last)
        def _():
            gather(slot_next_ref)

        if rider_between is not None:
            rider_between(0, 1)
        x2_ref[0] = x2 * lax.rsqrt(jnp.mean(x2 * x2, axis=-1, keepdims=True) + RMS_EPS) * gfin_ref[...]


def _moe_combine(slot, x1, mod_prev, ys, batch_offset, tail_args, rider=None):
    B, S, _ = x1.shape
    tm = TOKEN_TILE
    n_i = S // tm
    project = len(tail_args) > 1

    def next_step(b, i):
        n = jnp.minimum(b * n_i + i + 1, B * n_i - 1)
        return n // n_i, 0, n % n_i

    tok = pl.BlockSpec((1, tm, D_MODEL), lambda b, i: (b, i, 0))
    mod_spec = pl.BlockSpec((1, 6, D_MODEL), lambda b, i: (b + batch_offset, 0, 0))
    in_specs = [pl.BlockSpec((1, 1, tm), lambda b, i: (b, 0, i), memory_space=pltpu.SMEM),
                pl.BlockSpec((1, 1, tm), next_step, memory_space=pltpu.SMEM),
                tok, mod_spec, pl.BlockSpec(memory_space=pl.ANY)]
    out_specs, out_shape = [tok], [jax.ShapeDtypeStruct((B, S, D_MODEL), _F32)]
    scratch = [pltpu.VMEM((tm * ROW_CHUNKS, LANES), _F32), pltpu.SemaphoreType.DMA(())]
    if project:
        in_specs += [mod_spec,
                     pl.BlockSpec((1, D_MODEL), lambda b, i: (0, 0)),
                     pl.BlockSpec((D_MODEL, IN_W), lambda b, i: (0, 0)),
                     pl.BlockSpec((tm, LANES), lambda b, i: (i, 0)),
                     pl.BlockSpec((tm, LANES), lambda b, i: (i, 0))]
        out_specs += [pl.BlockSpec((1, tm // d, d * GROUP_W), lambda b, i: (b, i, 0)) for _, d in DILATED_GROUPS] * 3
        out_specs += [pl.BlockSpec((1, tm, POOL_W), lambda b, i: (b, i, 0))]
        out_shape += [jax.ShapeDtypeStruct((B, S // d, d * GROUP_W), _BF16) for _, d in DILATED_GROUPS] * 3
        out_shape += [jax.ShapeDtypeStruct((B, S, POOL_W), _F32)]
        scratch += [pltpu.VMEM((tm, LANES), _F32)]
    else:
        in_specs += [pl.BlockSpec((1, D_MODEL), lambda b, i: (0, 0))]
    return _launch(
        functools.partial(_combine_kernel, project=project),
        grid=(B, n_i),
        in_specs=in_specs,
        out_specs=out_specs,
        out_shape=out_shape,
        scratch=scratch,
        args=(slot, slot, x1, mod_prev, ys, *tail_args),
        name="moe_combine_in_proj" if project else "moe_combine_norm",
        rider=rider,
    )


def _rope_tables(S):
    inv = 1.0 / (ROPE_THETA ** (jnp.arange(0, HEAD_DIM, 2, dtype=_F32) / HEAD_DIM))
    ang = jnp.arange(S, dtype=_F32)[:, None] * inv[None, :]
    cos, sin = jnp.cos(ang), jnp.sin(ang)
    reps = LANES // HEAD_DIM
    return (jnp.tile(jnp.concatenate([cos, cos], axis=1), (1, reps)),
            jnp.tile(jnp.concatenate([-sin, sin], axis=1), (1, reps)))


def _trunks(xs, batch_offsets, mod, w):
    tables = [_rope_tables(x.shape[1]) for x in xs]
    outs = [_in_proj(x, mod[0], w["norm_mix"][0], w["w_in"][0], *tables[t], batch_offsets[t])
            for t, x in enumerate(xs)]
    xs = list(xs)
    for l in range(DEPTH):
        mod_l = mod[l]

        def mix(t, rider):
            q, k, v, u = outs[t][0:3], outs[t][3:6], outs[t][6:9], outs[t][9]
            attn, lses = [], []
            for g, (window, dilation) in enumerate(DILATED_GROUPS):
                o, lse = _attention(q[g], k[g], v[g], window, dilation)
                attn.append(o)
                lses.append(lse)
            return _mix_out(xs[t], attn, lses, u, mod_l, w["wpool_bd"][l], w["pool_scale"][l], w["w_out"][l],
                            w["norm_ffn"][l], w["wr_hi"], w["wr_lo"], w["rbias"], batch_offsets[t], rider)

        def experts(tbl, hs, n_tiles):
            return _moe_experts(tbl, hs, w["wr_hi3"], w["wr_lo3"], w["w_gate"], w["w_up"], w["w_down"], l,
                                n_tiles, MOE_TILE)

        def combine(t, slot, x1, ys, rider):
            if l + 1 < DEPTH:
                tail = (mod[l + 1], w["norm_mix"][l + 1], w["w_in"][l + 1], *tables[t])
            else:
                tail = (w["norm_final"],)
            return _moe_combine(slot, x1, mod_l, ys, batch_offsets[t], tail, rider)

        x1_0, h2_0, cls_0 = mix(0, None)
        slot_0, tbl_0, n_0 = _moe_plan(cls_0, MOE_TILE)
        x1_1, h2_1, cls_1, hs_0 = mix(1, (tbl_0, slot_0, h2_0, n_0))
        slot_1, tbl_1, n_1 = _moe_plan(cls_1, MOE_TILE)
        ys_0 = experts(tbl_0, hs_0, n_0)
        xs[0], *rest = combine(0, slot_0, x1_0, ys_0, (tbl_1, slot_1, h2_1, n_1))
        outs[0], hs_1 = rest[:-1], rest[-1]
        ys_1 = experts(tbl_1, hs_1, n_1)
        xs[1], *outs[1] = combine(1, slot_1, x1_1, ys_1, None)
    return xs


def kernel(x_prompt, x_sample, c_prompt, c_sample, norm_mix, w_mod, b_mod, w_in, w_pool, pool_scale, w_out,
           norm_ffn, w_router, router_bias, w_gate, w_up, w_down, norm_final):
    n_prompt = x_prompt.shape[0]
    c = jnp.concatenate([c_prompt, c_sample], axis=0)
    mod = _modulation(c, w_mod, b_mod).reshape(DEPTH, c.shape[0], 6, D_MODEL)
    n_pool = len(POOL_WINDOWS)
    eye = jnp.eye(n_pool, dtype=w_pool.dtype)
    wpool_bd = (w_pool[:, :, :, None, :] * eye[None, :, None, :, None]).reshape(DEPTH, POOL_W, POOL_W)
    wr_t = w_router.T.astype(_F32)
    wr_hi = wr_t.astype(_BF16)
    wr_lo = (wr_t - wr_hi.astype(_F32)).astype(_BF16)
    w = {
        "norm_mix": norm_mix.reshape(DEPTH, 1, D_MODEL),
        "norm_ffn": norm_ffn.reshape(DEPTH, 1, D_MODEL),
        "norm_final": norm_final.reshape(1, D_MODEL),
        "w_in": w_in.astype(_BF16),
        "w_out": w_out.astype(_BF16),
        "wpool_bd": wpool_bd.astype(_BF16),
        "pool_scale": pool_scale.reshape(DEPTH, 1, POOL_W),
        "wr_hi": wr_hi,
        "wr_lo": wr_lo,
        "wr_hi3": wr_hi.astype(_F32).reshape(N_EXPERTS, 1, D_MODEL),
        "wr_lo3": wr_lo.astype(_F32).reshape(N_EXPERTS, 1, D_MODEL),
        "rbias": router_bias.astype(_F32).reshape(N_EXPERTS, 1),
        "w_gate": w_gate.astype(_BF16).reshape(DEPTH * N_EXPERTS, D_MODEL, D_EXPERT),
        "w_up": w_up.astype(_BF16).reshape(DEPTH * N_EXPERTS, D_MODEL, D_EXPERT),
        "w_down": w_down.astype(_BF16).reshape(DEPTH * N_EXPERTS, D_EXPERT, D_MODEL),
    }
    y_prompt, y_sample = _trunks((x_prompt, x_sample), (0, n_prompt), mod, w)
    return (y_prompt, y_sample)
```
